```python
import jax, jax.numpy as jnp
from jax import lax
import numpy as np

D_MODEL = 1024
BATCH = 8
SEQ = 4096
DEPTH = 1

CHUNK = 64
Q_BLOCK = 128
RET_HEADS = 4
RET_DK = 128
RET_DV = 128
RET_W = RET_HEADS * RET_DK
RET_VW = RET_HEADS * RET_DV
SB_HEADS = 8
SB_DH = 64
SB_W = SB_HEADS * SB_DH
ROPE_BASE = 10000.0
PLE_DIM = 256
N_GROUPS = 4
EXPERTS_PER_GROUP = 4
N_EXPERTS = N_GROUPS * EXPERTS_PER_GROUP
TOP_K = 2
D_EXPERT = 512
EXPERT_BLOCK = 128
EPS = 1e-6
IN_WIDTHS = (RET_W, RET_W, RET_VW, RET_VW, SB_W, SB_W, SB_W, D_MODEL, D_MODEL)
IN_SPLITS = tuple(int(c) for c in np.cumsum(IN_WIDTHS)[:-1])
W_IN_COLS = sum(IN_WIDTHS)

kernel_name = "hybrid_retention_stickbreak_hmoe_block"


def rms_norm(x, g):
    xf = x.astype(jnp.float32)
    y = xf * lax.rsqrt(jnp.mean(xf * xf, axis=-1, keepdims=True) + EPS) * g.astype(jnp.float32)
    return y.astype(x.dtype)


def apply_rope(t, cos, sin):
    half = t.shape[-1] // 2
    t1, t2 = t[..., :half], t[..., half:]
    c = cos[None, :, None, :]
    s = sin[None, :, None, :]
    return jnp.concatenate([t1 * c - t2 * s, t1 * s + t2 * c], axis=-1)


def retention(q, k, v, gate, gn_gain):
    b, s, h, dk = q.shape
    dv = v.shape[-1]
    nc = s // CHUNK
    gamma = 1.0 - jnp.exp2(-5.0 - jnp.arange(h, dtype=jnp.float32))
    log_g = jnp.log(gamma)
    idx = jnp.arange(CHUNK, dtype=jnp.float32)
    d_intra = jnp.exp(log_g[:, None, None] * jnp.abs(idx[:, None] - idx[None, :]))
    q_dec = jnp.exp(log_g[:, None] * (idx + 1.0))
    k_dec = jnp.exp(log_g[:, None] * (CHUNK - 1.0 - idx))
    c_dec = jnp.exp(log_g * CHUNK)

    def to_chunks(t):
        return t.reshape(b, nc, CHUNK, h, t.shape[-1]).transpose(1, 0, 3, 2, 4)

    qc, kc, vc = to_chunks(q), to_chunks(k), to_chunks(v)

    def step(state, inp):
        qi, ki, vi = inp
        scores = jnp.einsum('bhid,bhjd->bhij', qi, ki) * d_intra
        o = (jnp.einsum('bhij,bhje->bhie', scores, vi)
             + jnp.einsum('bhid,bhde->bhie', qi * q_dec[..., None], state))
        state = c_dec[:, None, None] * state + jnp.einsum('bhjd,bhje->bhde', ki * k_dec[..., None], vi)
        return state, o

    state0 = jnp.zeros((b, h, dk, dv), jnp.float32)
    _, oc = lax.scan(step, state0, (qc, kc, vc))
    o = oc.transpose(1, 0, 3, 2, 4).reshape(b, s, h, dv)
    mu = jnp.mean(o, axis=-1, keepdims=True)
    var = jnp.mean((o - mu) ** 2, axis=-1, keepdims=True)
    o = ((o - mu) * lax.rsqrt(var + EPS)).reshape(b, s, h * dv) * gn_gain.astype(jnp.float32)
    return jax.nn.silu(gate) * o


def stick_breaking(q, k, v):
    b, s, h, d = q.shape
    q = q.transpose(0, 2, 1, 3) * (d ** -0.5)
    k = k.transpose(0, 2, 1, 3)
    v = v.transpose(0, 2, 1, 3)
    outs = []
    for i in range(s // Q_BLOCK):
        kl = (i + 1) * Q_BLOCK
        qb = q[:, :, i * Q_BLOCK:kl]
        z = jnp.einsum('bhtd,bhsd->bhts', qb, k[:, :, :kl])
        t_pos = i * Q_BLOCK + jnp.arange(Q_BLOCK)
        mask = jnp.arange(kl)[None, :] < t_pos[:, None]
        log_keep = jnp.where(mask, jax.nn.log_sigmoid(-z), 0.0)
        later = lax.cumsum(log_keep, axis=3, reverse=True) - log_keep
        a = jnp.where(mask, jnp.exp(jax.nn.log_sigmoid(z) + later), 0.0)
        outs.append(jnp.einsum('bhts,bhsd->bhtd', a, v[:, :, :kl]))
    o = jnp.concatenate(outs, axis=2)
    return o.transpose(0, 2, 1, 3).reshape(b, s, h * d)


def hier_moe(h, w_grp, b_grp, w_exp, b_exp, w_gate, w_up, w_down):
    b, s, dm = h.shape
    n = b * s
    hf = h.reshape(n, dm)
    grp_prob = jax.nn.softmax((hf @ w_grp).astype(jnp.float32) + b_grp.astype(jnp.float32), axis=-1)
    g_idx = jnp.argmax(grp_prob, axis=-1)
    p_grp = jnp.take_along_axis(grp_prob, g_idx[:, None], axis=1)[:, 0]
    exp_logits = ((hf @ w_exp).astype(jnp.float32) + b_exp.astype(jnp.float32)).reshape(n, N_GROUPS, EXPERTS_PER_GROUP)
    in_grp = jnp.take_along_axis(exp_logits, g_idx[:, None, None], axis=1)[:, 0]
    top_p, top_i = lax.top_k(jax.nn.softmax(in_grp, axis=-1), TOP_K)
    weights = top_p / jnp.sum(top_p, axis=-1, keepdims=True) * p_grp[:, None]
    expert_id = g_idx[:, None] * EXPERTS_PER_GROUP + top_i

    a = n * TOP_K
    flat_e = expert_id.reshape(a)
    flat_tok = jnp.repeat(jnp.arange(n, dtype=jnp.int32), TOP_K)
    flat_w = weights.reshape(a)
    order = jnp.argsort(flat_e)
    sorted_e = flat_e[order]
    counts = jax.ops.segment_sum(jnp.ones((a,), jnp.int32), flat_e, num_segments=N_EXPERTS)
    padded = (counts + EXPERT_BLOCK - 1) // EXPERT_BLOCK * EXPERT_BLOCK
    pad_end = jnp.cumsum(padded)
    pad_start = pad_end - padded
    start = jnp.cumsum(counts) - counts
    dest = pad_start[sorted_e] + (jnp.arange(a, dtype=jnp.int32) - start[sorted_e])
    m = a + N_EXPERTS * EXPERT_BLOCK
    nb = m // EXPERT_BLOCK
    buf_tok = jnp.zeros((m,), jnp.int32).at[dest].set(flat_tok[order])
    buf_w = jnp.zeros((m,), jnp.float32).at[dest].set(flat_w[order])
    block_e = jnp.minimum(
        jnp.searchsorted(pad_end, jnp.arange(nb, dtype=jnp.int32) * EXPERT_BLOCK, side='right'),
        N_EXPERTS - 1)
    xb = hf[buf_tok].reshape(nb, EXPERT_BLOCK, dm)

    def expert_block(args):
        xi, e = args
        hid = jax.nn.silu(xi @ w_gate[e]) * (xi @ w_up[e])
        return hid @ w_down[e]

    yb = lax.map(expert_block, (xb, block_e)).reshape(m, dm)
    y = jax.ops.segment_sum(yb.astype(jnp.float32) * buf_w[:, None], buf_tok, num_segments=n)
    return y.reshape(b, s, dm).astype(h.dtype)


def setup_inputs(seed: int = 0) -> dict:
    key = jax.random.key(seed)
    ks = jax.random.split(key, 24)
    f32 = jnp.float32

    def nrm(k, shape, fan_in):
        return jax.random.normal(k, shape, f32) * (fan_in ** -0.5)

    def gain(k, shape):
        return 1.0 + 0.02 * jax.random.normal(k, shape, f32)

    return {
        "x": jax.random.normal(ks[0], (BATCH, SEQ, D_MODEL), f32),
        "p": jax.random.normal(ks[1], (DEPTH, BATCH, SEQ, PLE_DIM), f32),
        "norm_mix": gain(ks[2], (DEPTH, D_MODEL)),
        "w_in": nrm(ks[3], (DEPTH, D_MODEL, W_IN_COLS), D_MODEL),
        "ret_gn": gain(ks[4], (DEPTH, RET_VW)),
        "w_br_ret": nrm(ks[5], (DEPTH, RET_VW, D_MODEL), RET_VW),
        "w_br_sb": nrm(ks[6], (DEPTH, SB_W, D_MODEL), SB_W),
        "w_out": nrm(ks[7], (DEPTH, D_MODEL, D_MODEL), D_MODEL),
        "norm_ffn": gain(ks[8], (DEPTH, D_MODEL)),
        "w_grp": nrm(ks[9], (DEPTH, D_MODEL, N_GROUPS), D_MODEL),
        "b_grp": 0.01 * jax.random.normal(ks[10], (DEPTH, N_GROUPS), f32),
        "w_exp": nrm(ks[11], (DEPTH, D_MODEL, N_EXPERTS), D_MODEL),
        "b_exp": 0.01 * jax.random.normal(ks[12], (DEPTH, N_EXPERTS), f32),
        "w_gate": nrm(ks[13], (DEPTH, N_EXPERTS, D_MODEL, D_EXPERT), D_MODEL),
        "w_up": nrm(ks[14], (DEPTH, N_EXPERTS, D_MODEL, D_EXPERT), D_MODEL),
        "w_down": nrm(ks[15], (DEPTH, N_EXPERTS, D_EXPERT, D_MODEL), D_EXPERT),
        "norm_ple": gain(ks[16], (DEPTH, D_MODEL)),
        "w_ple": nrm(ks[17], (DEPTH, PLE_DIM, D_MODEL), PLE_DIM),
        "w_ple_gate": nrm(ks[18], (DEPTH, D_MODEL, D_MODEL), D_MODEL),
        "norm_final": gain(ks[19], (D_MODEL,)),
    }


def reference(x, p, norm_mix, w_in, ret_gn, w_br_ret, w_br_sb, w_out, norm_ffn,
              w_grp, b_grp, w_exp, b_exp, w_gate, w_up, w_down,
              norm_ple, w_ple, w_ple_gate, norm_final):
    b, s, _ = x.shape
    f32 = jnp.float32
    pos = jnp.arange(s, dtype=f32)
    inv_freq = ROPE_BASE ** (-jnp.arange(0, RET_DK, 2, dtype=f32) / RET_DK)
    ang = pos[:, None] * inv_freq[None, :]
    cos, sin = jnp.cos(ang), jnp.sin(ang)
    for i in range(DEPTH):
        h = rms_norm(x, norm_mix[i])
        proj = h @ w_in[i]
        rq, rk, rv, rg, sq, sk, sv, g_ret, g_sb = jnp.split(proj, IN_SPLITS, axis=-1)
        rq = apply_rope(rq.reshape(b, s, RET_HEADS, RET_DK).astype(f32), cos, sin)
        rk = apply_rope(rk.reshape(b, s, RET_HEADS, RET_DK).astype(f32), cos, sin) * (RET_DK ** -0.5)
        rv = rv.reshape(b, s, RET_HEADS, RET_DV).astype(f32)
        ret = retention(rq, rk, rv, rg.astype(f32), ret_gn[i]).astype(x.dtype)
        sb = stick_breaking(sq.reshape(b, s, SB_HEADS, SB_DH).astype(f32),
                            sk.reshape(b, s, SB_HEADS, SB_DH).astype(f32),
                            sv.reshape(b, s, SB_HEADS, SB_DH).astype(f32)).astype(x.dtype)
        merged = (jax.nn.sigmoid(g_ret) * (ret @ w_br_ret[i])
                  + jax.nn.sigmoid(g_sb) * (sb @ w_br_sb[i]))
        x = x + merged @ w_out[i]
        x = x + hier_moe(rms_norm(x, norm_ffn[i]), w_grp[i], b_grp[i], w_exp[i], b_exp[i],
                         w_gate[i], w_up[i], w_down[i])
        hp = rms_norm(x, norm_ple[i])
        x = x + jax.nn.sigmoid(hp @ w_ple_gate[i]) * (p[i].astype(x.dtype) @ w_ple[i])
    return rms_norm(x, norm_final)
```

```python
import functools

import numpy as np
import jax
import jax.numpy as jnp
from jax import lax
from jax.experimental import pallas as pl
from jax.experimental.pallas import tpu as pltpu

F32 = jnp.float32
BF16 = jnp.bfloat16
I32 = jnp.int32

EPS = 1e-6
CHUNK = 64
RET_HEADS = 4
RET_DK = 128
SB_HEADS = 8
SB_DH = 64
ROPE_BASE = 10000.0
N_GROUPS = 4
EXPERTS_PER_GROUP = 4
N_EXPERTS = N_GROUPS * EXPERTS_PER_GROUP

LANES = 128
SUBLANES = 8
VMEM_LIMIT = 56 * 1024 * 1024

PROJ_ROWS = 512
RET_ROWS = 256
SB_BLOCK = 256
MERGE_ROWS = 512
EXPERT_ROWS = 256
FINAL_ROWS = 256
ROUTE_COLS = 128
EXPERT_COL0 = 8


def _rms(x, g):
    return x * lax.rsqrt(jnp.mean(x * x, axis=-1, keepdims=True) + EPS) * g


def _sigmoid(x):
    return 1.0 / (1.0 + jnp.exp(-x))


def _params(*sem):
    return pltpu.CompilerParams(dimension_semantics=sem, vmem_limit_bytes=VMEM_LIMIT)


def _proj_kernel(x_ref, g_ref, w_ref, cos_ref, sin_ref,
                 rq_ref, rk_ref, rv_ref, rg_ref, sq_ref, sk_ref, sv_ref, gr_ref, gs_ref):
    h = _rms(x_ref[...], g_ref[...]).astype(BF16)
    cos = cos_ref[...]
    sin = sin_ref[...]

    def seg(lo, hi):
        return jnp.dot(h, w_ref[:, lo:hi], preferred_element_type=F32)

    def rope(t):
        outs = []
        for hd in range(RET_HEADS):
            th = t[:, hd * RET_DK:(hd + 1) * RET_DK]
            outs.append(th * cos + pltpu.roll(th, RET_DK // 2, 1) * sin)
        return jnp.concatenate(outs, axis=1)

    rq_ref[...] = rope(seg(0, 512)).astype(BF16)
    rk_ref[...] = (rope(seg(512, 1024)) * (RET_DK ** -0.5)).astype(BF16)
    rv_ref[...] = seg(1024, 1536).astype(BF16)
    rg_ref[...] = seg(1536, 2048).astype(BF16)
    sq_ref[...] = (seg(2048, 2560) * (SB_DH ** -0.5)).astype(BF16)
    sk_ref[...] = seg(2560, 3072).astype(BF16)
    sv_ref[...] = seg(3072, 3584).astype(BF16)
    gr_ref[...] = seg(3584, 4608).astype(BF16)
    gs_ref[...] = seg(4608, 5632).astype(BF16)


def _proj(x2, g, w, cosf, sinf, seq):
    n, d = x2.shape
    tm = PROJ_ROWS
    pos_blocks = seq // tm
    row = lambda i: (i, 0)
    const = lambda i: (0, 0)
    outs = [jax.ShapeDtypeStruct((n, 512), BF16)] * 7 + [jax.ShapeDtypeStruct((n, 1024), BF16)] * 2
    out_specs = [pl.BlockSpec((tm, 512), row)] * 7 + [pl.BlockSpec((tm, 1024), row)] * 2
    return pl.pallas_call(
        _proj_kernel,
        grid=(n // tm,),
        in_specs=[
            pl.BlockSpec((tm, d), row),
            pl.BlockSpec((1, d), const),
            pl.BlockSpec(w.shape, const),
            pl.BlockSpec((tm, RET_DK), lambda i: (i % pos_blocks, 0)),
            pl.BlockSpec((tm, RET_DK), lambda i: (i % pos_blocks, 0)),
        ],
        out_specs=out_specs,
        out_shape=outs,
        compiler_params=_params("parallel"),
        name="proj",
    )(x2, g, w, cosf, sinf)


def _ret_consts(tb):
    hd = np.arange(RET_HEADS, dtype=np.float64)
    lg = np.log(1.0 - np.exp2(-5.0 - hd))
    idx = np.arange(tb, dtype=np.float64)
    dist = np.abs(idx[:, None] - idx[None, :])
    chunk = (np.arange(tb) // CHUNK)
    allowed = chunk[None, :] <= chunk[:, None]
    dm = np.exp(lg[:, None, None] * dist) * allowed
    qdec = np.exp(lg[:, None] * (idx + 1.0))[:, :, None] * np.ones((1, 1, RET_DK))
    kdec = np.exp(lg[:, None] * (tb - 1.0 - idx))[:, :, None] * np.ones((1, 1, RET_DK))
    cdec = [float(v) for v in np.exp(lg * tb)]
    return (jnp.asarray(dm, F32), jnp.asarray(qdec, F32), jnp.asarray(kdec, F32), cdec)


def _ret_kernel(cdec, q_ref, k_ref, v_ref, g_ref, dm_ref, qd_ref, kd_ref, gn_ref, o_ref, st_ref):
    @pl.when(pl.program_id(1) == 0)
    def _():
        st_ref[...] = jnp.zeros_like(st_ref)

    for hd in range(RET_HEADS):
        sl = slice(hd * RET_DK, (hd + 1) * RET_DK)
        q = q_ref[:, sl]
        k = k_ref[:, sl]
        v = v_ref[:, sl]
        st = st_ref[hd]
        scores = lax.dot_general(q, k, (((1,), (1,)), ((), ())), preferred_element_type=F32)
        scores = scores * dm_ref[hd]
        qd = (q.astype(F32) * qd_ref[hd]).astype(BF16)
        o = (jnp.dot(scores.astype(BF16), v, preferred_element_type=F32)
             + jnp.dot(qd, st.astype(BF16), preferred_element_type=F32))
        kd = (k.astype(F32) * kd_ref[hd]).astype(BF16)
        st_ref[hd] = cdec[hd] * st + lax.dot_general(
            kd, v, (((0,), (0,)), ((), ())), preferred_element_type=F32)
        mu = jnp.mean(o, axis=-1, keepdims=True)
        oc = o - mu
        var = jnp.mean(oc * oc, axis=-1, keepdims=True)
        on = oc * lax.rsqrt(var + EPS) * gn_ref[:, sl]
        gate = g_ref[:, sl].astype(F32)
        o_ref[:, sl] = (gate * _sigmoid(gate) * on).astype(BF16)


def _retention(rq, rk, rv, rg, gn, batch, seq):
    n, w = rq.shape
    tb = RET_ROWS
    nc = seq // tb
    dm, qdec, kdec, cdec = _ret_consts(tb)
    blk = pl.BlockSpec((tb, w), lambda b, c: (b * nc + c, 0))
    c3 = lambda b, c: (0, 0, 0)
    return pl.pallas_call(
        functools.partial(_ret_kernel, cdec),
        grid=(batch, nc),
        in_specs=[blk, blk, blk, blk,
                  pl.BlockSpec(dm.shape, c3), pl.BlockSpec(qdec.shape, c3), pl.BlockSpec(kdec.shape, c3),
                  pl.BlockSpec((1, w), lambda b, c: (0, 0))],
        out_specs=blk,
        out_shape=jax.ShapeDtypeStruct((n, w), BF16),
        scratch_shapes=[pltpu.VMEM((RET_HEADS, RET_DK, RET_DK), F32)],
        compiler_params=_params("parallel", "arbitrary"),
        name="retention",
    )(rq, rk, rv, rg, dm, qdec, kdec, gn)


def _sb_kernel(q_ref, k_ref, vt_ref, o_ref):
    tq = q_ref.shape[0]
    tk = tq
    grp = tk // SUBLANES
    qi = pl.program_id(2)
    q = q_ref[...]
    lane = lax.broadcasted_iota(I32, q.shape, 1)
    shp3 = (grp, SUBLANES, tq)
    key_pos = lax.broadcasted_iota(I32, shp3, 1) * grp + lax.broadcasted_iota(I32, shp3, 0)
    causal = key_pos < lax.broadcasted_iota(I32, shp3, 2)
    row8 = lax.broadcasted_iota(I32, (SUBLANES, tq), 0)

    def block(kb, qm, hh, acc, carry, diagonal):
        z = lax.dot_general(k_ref[kb], qm, (((1,), (1,)), ((), ())), preferred_element_type=F32)
        softplus = jnp.maximum(z, 0.0) + jnp.log(1.0 + jnp.exp(-jnp.abs(z)))
        z3 = z.reshape(shp3)
        sp3 = softplus.reshape(shp3)
        if diagonal:
            sp3 = jnp.where(causal, sp3, 0.0)
        run = jnp.zeros((SUBLANES, tq), F32)
        parts = [None] * grp
        for g in reversed(range(grp)):
            run = run - sp3[g]
            parts[g] = run
        tot = parts[0]
        inc = tot
        for d in (1, 2, 4):
            inc = inc + jnp.where(row8 + d < SUBLANES, pltpu.roll(inc, SUBLANES - d, 0), 0.0)
        base = inc - tot + carry
        a3 = jnp.exp(z3 + jnp.stack(parts, axis=0) + base[None])
        if diagonal:
            a3 = jnp.where(causal, a3, 0.0)
        a = a3.reshape(tk, tq).astype(BF16)
        vt = vt_ref[kb, hh * SB_DH:(hh + 1) * SB_DH, :]
        acc = acc + jnp.dot(vt, a, preferred_element_type=F32)
        carry = carry + jnp.broadcast_to(inc[0:1, :], carry.shape)
        return acc, carry

    outs = []
    for hh in range(2):
        qm = jnp.where((lane >= hh * SB_DH) & (lane < (hh + 1) * SB_DH), q, jnp.zeros_like(q))
        acc0 = jnp.zeros((SB_DH, tq), F32)
        carry0 = jnp.zeros((SUBLANES, tq), F32)
        acc, carry = block(qi, qm, hh, acc0, carry0, True)

        def body(j, c, qm=qm, hh=hh):
            return block(qi - 1 - j, qm, hh, c[0], c[1], False)

        acc, carry = lax.fori_loop(0, qi, body, (acc, carry))
        outs.append(acc)
    o_ref[...] = jnp.concatenate(outs, axis=0).T.astype(BF16)


def _stick_breaking(sq, kperm, vtperm, batch, seq):
    n, w = sq.shape
    tq = SB_BLOCK
    nq = seq // tq
    pairs = w // LANES
    return pl.pallas_call(
        _sb_kernel,
        grid=(batch, pairs, nq),
        in_specs=[
            pl.BlockSpec((tq, LANES), lambda b, hp, i: (b * nq + i, hp)),
            pl.BlockSpec((None, nq, tq, LANES), lambda b, hp, i: (b, 0, 0, hp)),
            pl.BlockSpec((None, None, nq, LANES, tq), lambda b, hp, i: (b, hp, 0, 0, 0)),
        ],
        out_specs=pl.BlockSpec((tq, LANES), lambda b, hp, i: (b * nq + i, hp)),
        out_shape=jax.ShapeDtypeStruct((n, w), BF16),
        compiler_params=_params("parallel", "parallel", "arbitrary"),
        name="stickbrk",
    )(sq, kperm, vtperm)


def _merge_kernel(ret_ref, sb_ref, gr_ref, gs_ref, x_ref, wbr_ref, wbs_ref, wo_ref, nf_ref,
                  wr_ref, br_ref, tri_ref,
                  x1_ref, h2_ref, ri_ref, rw_ref, cnt_ref, carry_ref):
    @pl.when(pl.program_id(0) == 0)
    def _():
        carry_ref[...] = jnp.zeros_like(carry_ref)

    tm = x_ref.shape[0]
    merged = (_sigmoid(gr_ref[...].astype(F32)) * jnp.dot(ret_ref[...], wbr_ref[...], preferred_element_type=F32)
              + _sigmoid(gs_ref[...].astype(F32)) * jnp.dot(sb_ref[...], wbs_ref[...], preferred_element_type=F32))
    x1 = x_ref[...] + jnp.dot(merged.astype(BF16), wo_ref[...], preferred_element_type=F32)
    x1_ref[...] = x1
    h2 = _rms(x1, nf_ref[...])
    h2_ref[...] = h2
    logits = jnp.dot(h2.astype(BF16), wr_ref[...], preferred_element_type=F32) + br_ref[...]
    lt = logits.T
    neg = jnp.float32(-jnp.inf)

    grow = lax.broadcasted_iota(I32, (SUBLANES, tm), 0)
    gl = jnp.where(grow < N_GROUPS, lt[0:SUBLANES], neg)
    gmax = jnp.max(gl, axis=0, keepdims=True)
    g_idx = jnp.min(jnp.where(gl == gmax, grow, N_GROUPS), axis=0, keepdims=True)
    p_grp = 1.0 / jnp.sum(jnp.exp(gl - gmax), axis=0, keepdims=True)

    erow = lax.broadcasted_iota(I32, (N_EXPERTS, tm), 0)
    el = lt[EXPERT_COL0:EXPERT_COL0 + N_EXPERTS]
    el = jnp.where((erow >= g_idx * EXPERTS_PER_GROUP) & (erow < (g_idx + 1) * EXPERTS_PER_GROUP), el, neg)
    m1 = jnp.max(el, axis=0, keepdims=True)
    i1 = jnp.min(jnp.where(el == m1, erow, N_EXPERTS), axis=0, keepdims=True)
    el2 = jnp.where(erow == i1, neg, el)
    m2 = jnp.max(el2, axis=0, keepdims=True)
    i2 = jnp.min(jnp.where(el2 == m2, erow, N_EXPERTS), axis=0, keepdims=True)
    e2 = jnp.exp(m2 - m1)
    w1 = p_grp / (1.0 + e2)
    w2 = p_grp * e2 / (1.0 + e2)

    sel1 = erow == i1
    sel2 = erow == i2
    onehot = (sel1 | sel2).astype(BF16)
    cinc = jnp.dot(onehot, tri_ref[...], preferred_element_type=F32)
    carry = carry_ref[...]
    pos = carry[:, 0:1] + cinc - 1.0
    r1 = jnp.sum(jnp.where(sel1, pos, 0.0), axis=0, keepdims=True)
    r2 = jnp.sum(jnp.where(sel2, pos, 0.0), axis=0, keepdims=True)
    carry_new = carry + jnp.sum(onehot.astype(F32), axis=1, keepdims=True)
    carry_ref[...] = carry_new
    cnt_ref[...] = carry_new

    ri_ref[...] = jnp.zeros_like(ri_ref)
    ri_ref[0:1, :] = i1
    ri_ref[1:2, :] = i2
    ri_ref[2:3, :] = r1.astype(I32)
    ri_ref[3:4, :] = r2.astype(I32)
    wrow = lax.broadcasted_iota(I32, (ROUTE_COLS, tm), 0)
    wt = jnp.where(wrow == 0, w1, jnp.where(wrow == 1, w2, 0.0))
    rw_ref[...] = wt.T


def _merge(ret, sb, g_ret, g_sb, x2, wbr, wbs, wo, nf, wr, br):
    n, d = x2.shape
    tm = MERGE_ROWS
    tri = jnp.asarray(np.triu(np.ones((tm, tm), np.float32)), BF16)
    row = lambda i: (i, 0)
    const = lambda i: (0, 0)
    full = lambda a: pl.BlockSpec(a.shape, const)
    return pl.pallas_call(
        _merge_kernel,
        grid=(n // tm,),
        in_specs=[pl.BlockSpec((tm, 512), row), pl.BlockSpec((tm, 512), row),
                  pl.BlockSpec((tm, d), row), pl.BlockSpec((tm, d), row), pl.BlockSpec((tm, d), row),
                  full(wbr), full(wbs), full(wo), full(nf), full(wr), full(br), full(tri)],
        out_specs=[pl.BlockSpec((tm, d), row), pl.BlockSpec((tm, d), row),
                   pl.BlockSpec((SUBLANES, tm), lambda i: (0, i)),
                   pl.BlockSpec((tm, ROUTE_COLS), row),
                   pl.BlockSpec((N_EXPERTS, LANES), const)],
        out_shape=[jax.ShapeDtypeStruct((n, d), F32), jax.ShapeDtypeStruct((n, d), F32),
                   jax.ShapeDtypeStruct((SUBLANES, n), I32),
                   jax.ShapeDtypeStruct((n, ROUTE_COLS), F32),
                   jax.ShapeDtypeStruct((N_EXPERTS, LANES), F32)],
        scratch_shapes=[pltpu.VMEM((N_EXPERTS, LANES), F32)],
        compiler_params=_params("arbitrary"),
        name="merge",
    )(ret, sb, g_ret, g_sb, x2, wbr, wbs, wo, nf, wr, br, tri)


def _issue_rows(idx_ref, src_hbm, dst, sem, rows):
    def body(r, c):
        tok = idx_ref[0, 0, r]
        pltpu.make_async_copy(src_hbm.at[pl.ds(tok, 1)], dst.at[pl.ds(r, 1)], sem).start()
        return c
    lax.fori_loop(0, rows, body, 0)


def _expert_kernel(be_ref, nu_ref, idx_ref, idxn_ref, x_hbm, wg_ref, wu_ref, wd_ref, o_ref, xbuf, sem):
    j = pl.program_id(0)
    used = nu_ref[0]
    rows = xbuf.shape[1]
    slot = j % 2

    @pl.when(j == 0)
    def _():
        _issue_rows(idx_ref, x_hbm, xbuf.at[0], sem.at[0], rows)

    @pl.when(j + 1 < used)
    def _():
        _issue_rows(idxn_ref, x_hbm, xbuf.at[1 - slot], sem.at[1 - slot], rows)

    @pl.when(j < used)
    def _():
        pltpu.make_async_copy(xbuf.at[slot], xbuf.at[slot], sem.at[slot]).wait()
        xb = xbuf[slot].astype(BF16)
        gate = jnp.dot(xb, wg_ref[...], preferred_element_type=F32)
        up = jnp.dot(xb, wu_ref[...], preferred_element_type=F32)
        hid = (gate * _sigmoid(gate) * up).astype(BF16)
        o_ref[...] = jnp.dot(hid, wd_ref[...], preferred_element_type=F32)

    @pl.when(j >= used)
    def _():
        o_ref[...] = jnp.zeros_like(o_ref)


def _experts(block_e, n_used, buf_tok3, h2, wg, wu, wd):
    nb, _, tmb = buf_tok3.shape
    d = h2.shape[1]
    de = wg.shape[2]
    wmap = lambda j, be, nu: (be[jnp.minimum(j, nu[0] - 1)], 0, 0)
    return pl.pallas_call(
        _expert_kernel,
        grid_spec=pltpu.PrefetchScalarGridSpec(
            num_scalar_prefetch=2,
            grid=(nb,),
            in_specs=[
                pl.BlockSpec((1, 1, tmb), lambda j, be, nu: (j, 0, 0), memory_space=pltpu.SMEM),
                pl.BlockSpec((1, 1, tmb), lambda j, be, nu: (jnp.minimum(j + 1, nb - 1), 0, 0),
                             memory_space=pltpu.SMEM),
                pl.BlockSpec(memory_space=pl.ANY),
                pl.BlockSpec((None, d, de), wmap),
                pl.BlockSpec((None, d, de), wmap),
                pl.BlockSpec((None, de, d), wmap),
            ],
            out_specs=pl.BlockSpec((tmb, d), lambda j, be, nu: (j, 0)),
            scratch_shapes=[pltpu.VMEM((2, tmb, d), F32), pltpu.SemaphoreType.DMA((2,))],
        ),
        out_shape=jax.ShapeDtypeStruct((nb * tmb, d), F32),
        compiler_params=_params("arbitrary"),
        name="experts",
    )(block_e, n_used, buf_tok3, buf_tok3, h2, wg, wu, wd)


def _final_kernel(idx_ref, idxn_ref, x1_ref, rw_ref, p_ref, yb_hbm, npl_ref, wpg_ref, wp_ref, nfin_ref,
                  o_ref, gbuf, sem):
    i = pl.program_id(0)
    nt = pl.num_programs(0)
    tm = x1_ref.shape[0]
    slot = i % 2

    @pl.when(i == 0)
    def _():
        _issue_rows(idx_ref, yb_hbm, gbuf.at[0], sem.at[0], 2 * tm)

    @pl.when(i + 1 < nt)
    def _():
        _issue_rows(idxn_ref, yb_hbm, gbuf.at[1 - slot], sem.at[1 - slot], 2 * tm)

    pltpu.make_async_copy(gbuf.at[slot], gbuf.at[slot], sem.at[slot]).wait()
    rw = rw_ref[...]
    x2 = (x1_ref[...] + rw[:, 0:1] * gbuf[slot, 0:tm, :] + rw[:, 1:2] * gbuf[slot, tm:2 * tm, :])
    hp = _rms(x2, npl_ref[...]).astype(BF16)
    gate = _sigmoid(jnp.dot(hp, wpg_ref[...], preferred_element_type=F32))
    emb = jnp.dot(p_ref[...].astype(BF16), wp_ref[...], preferred_element_type=F32)
    x3 = x2 + gate * emb
    o_ref[...] = _rms(x3, nfin_ref[...])


def _final(dest3, x1, rw, p2, yb, npl, wpg, wp, nfin):
    n, d = x1.shape
    tm = FINAL_ROWS
    nt = n // tm
    row = lambda i: (i, 0)
    const = lambda i: (0, 0)
    full = lambda a: pl.BlockSpec(a.shape, const)
    return pl.pallas_call(
        _final_kernel,
        grid=(nt,),
        in_specs=[
            pl.BlockSpec((1, 1, 2 * tm), lambda i: (i, 0, 0), memory_space=pltpu.SMEM),
            pl.BlockSpec((1, 1, 2 * tm), lambda i: (jnp.minimum(i + 1, nt - 1), 0, 0), memory_space=pltpu.SMEM),
            pl.BlockSpec((tm, d), row),
            pl.BlockSpec((tm, ROUTE_COLS), row),
            pl.BlockSpec((tm, p2.shape[1]), row),
            pl.BlockSpec(memory_space=pl.ANY),
            full(npl), full(wpg), full(wp), full(nfin),
        ],
        out_specs=pl.BlockSpec((tm, d), row),
        out_shape=jax.ShapeDtypeStruct((n, d), F32),
        scratch_shapes=[pltpu.VMEM((2, 2 * tm, d), F32), pltpu.SemaphoreType.DMA((2,))],
        compiler_params=_params("arbitrary"),
        name="final",
    )(dest3, dest3, x1, rw, p2, yb, npl, wpg, wp, nfin)


def _layer(x2, p2, batch, seq, cosf, sinf, norm_mix, w_in, ret_gn, w_br_ret, w_br_sb, w_out, norm_ffn,
           w_grp, b_grp, w_exp, b_exp, w_gate, w_up, w_down, norm_ple, w_ple, w_ple_gate, out_gain):
    n, d = x2.shape
    rq, rk, rv, rg, sq, sk, sv, g_ret, g_sb = _proj(
        x2, norm_mix[None, :], w_in.astype(BF16), cosf, sinf, seq)
    ret = _retention(rq, rk, rv, rg, ret_gn[None, :], batch, seq)

    tk = SB_BLOCK
    nkb = seq // tk
    grp = tk // SUBLANES
    pairs = (SB_HEADS * SB_DH) // LANES
    kperm = (sk.reshape(batch, nkb, SUBLANES, grp, pairs * LANES)
             .transpose(0, 1, 3, 2, 4).reshape(batch, nkb, tk, pairs * LANES))
    vtperm = (sv.reshape(batch, nkb, SUBLANES, grp, pairs, LANES)
              .transpose(0, 4, 1, 5, 3, 2).reshape(batch, pairs, nkb, LANES, tk))
    sb = _stick_breaking(sq, kperm, vtperm, batch, seq)

    wr = jnp.zeros((d, ROUTE_COLS), F32)
    wr = wr.at[:, 0:N_GROUPS].set(w_grp).at[:, EXPERT_COL0:EXPERT_COL0 + N_EXPERTS].set(w_exp)
    br = jnp.zeros((1, ROUTE_COLS), F32)
    br = br.at[0, 0:N_GROUPS].set(b_grp).at[0, EXPERT_COL0:EXPERT_COL0 + N_EXPERTS].set(b_exp)
    x1, h2, ri, rw, cnt = _merge(ret, sb, g_ret, g_sb, x2, w_br_ret.astype(BF16), w_br_sb.astype(BF16),
                                 w_out.astype(BF16), norm_ffn[None, :], wr.astype(BF16), br)

    tmb = EXPERT_ROWS
    counts = cnt[:, 0].astype(I32)
    padded = (counts + tmb - 1) // tmb * tmb
    pad_end = jnp.cumsum(padded)
    pad_start = pad_end - padded
    dest = pad_start[ri[0:2]] + ri[2:4]
    m = 2 * n + N_EXPERTS * tmb
    nb = m // tmb
    tok = jnp.broadcast_to(jnp.arange(n, dtype=I32)[None, :], (2, n))
    buf_tok = jnp.zeros((m,), I32).at[dest.reshape(-1)].set(tok.reshape(-1))
    block_e = jnp.minimum(
        jnp.sum((pad_end[None, :] <= (jnp.arange(nb, dtype=I32) * tmb)[:, None]).astype(I32), axis=1),
        N_EXPERTS - 1).astype(I32)
    n_used = (pad_end[-1:] // tmb).astype(I32)
    yb = _experts(block_e, n_used, buf_tok.reshape(nb, 1, tmb), h2,
                  w_gate.astype(BF16), w_up.astype(BF16), w_down.astype(BF16))

    tf = FINAL_ROWS
    dest3 = dest.reshape(2, n // tf, tf).transpose(1, 0, 2).reshape(n // tf, 1, 2 * tf)
    return _final(dest3, x1, rw, p2, yb, norm_ple[None, :], w_ple_gate.astype(BF16), w_ple.astype(BF16),
                  out_gain)


def kernel(x, p, norm_mix, w_in, ret_gn, w_br_ret, w_br_sb, w_out, norm_ffn, w_grp, b_grp, w_exp, b_exp,
           w_gate, w_up, w_down, norm_ple, w_ple, w_ple_gate, norm_final):
    batch, seq, d = x.shape
    depth = p.shape[0]
    assert depth == 1, "the final RMSNorm is fused into the last layer's kernel"
    pos = jnp.arange(seq, dtype=F32)
    inv_freq = ROPE_BASE ** (-jnp.arange(0, RET_DK, 2, dtype=F32) / RET_DK)
    ang = pos[:, None] * inv_freq[None, :]
    cos, sin = jnp.cos(ang), jnp.sin(ang)
    cosf = jnp.concatenate([cos, cos], axis=1)
    sinf = jnp.concatenate([-sin, sin], axis=1)
    x2 = x.reshape(batch * seq, d)
    i = 0
    out = _layer(x2, p[i].reshape(batch * seq, -1), batch, seq, cosf, sinf, norm_mix[i], w_in[i], ret_gn[i],
                 w_br_ret[i], w_br_sb[i], w_out[i], norm_ffn[i], w_grp[i], b_grp[i], w_exp[i], b_exp[i],
                 w_gate[i], w_up[i], w_down[i], norm_ple[i], w_ple[i], w_ple_gate[i], norm_final[None, :])
    return out.reshape(batch, seq, d)
```

```python
import functools

import numpy as np
import jax
import jax.numpy as jnp
from jax import lax
from jax.experimental import pallas as pl
from jax.experimental.pallas import tpu as pltpu

F32 = jnp.float32
BF16 = jnp.bfloat16
I32 = jnp.int32

EPS = 1e-6
CHUNK = 64
RET_HEADS = 4
RET_DK = 128
SB_HEADS = 8
SB_DH = 64
ROPE_BASE = 10000.0
N_GROUPS = 4
EXPERTS_PER_GROUP = 4
N_EXPERTS = N_GROUPS * EXPERTS_PER_GROUP

LANES = 128
SUBLANES = 8
VMEM_LIMIT = 56 * 1024 * 1024

PROJ_ROWS = 512
RET_ROWS = 256
SB_BLOCK = 256
MERGE_ROWS = 512
EXPERT_ROWS = 256
FINAL_ROWS = 256
ROUTE_COLS = 128
ROW_TILES = 8
EXPERT_COL0 = 8
LOG2E = 1.4426950408889634
SB_SKIP_LOG2 = -160.0


def _rms(x, g):
    return x * lax.rsqrt(jnp.mean(x * x, axis=-1, keepdims=True) + EPS) * g


def _sigmoid(x):
    return 1.0 / (1.0 + jnp.exp(-x))


def _to_row_tiles(ref, val):
    rows = val.shape[0]
    for s in range(ROW_TILES):
        ref[pl.ds(s, rows, stride=ROW_TILES), :] = val[:, s * LANES:(s + 1) * LANES]


def _from_row_tiles(ref, first, rows):
    return jnp.concatenate([ref[pl.ds(first * ROW_TILES + s, rows, stride=ROW_TILES), :]
                            for s in range(ROW_TILES)], axis=1)


def _params(*sem):
    return pltpu.CompilerParams(dimension_semantics=sem, vmem_limit_bytes=VMEM_LIMIT)


def _proj_kernel(x_ref, g_ref, w_ref, wvt_ref, perm_ref, cos_ref, sin_ref,
                 rq_ref, rk_ref, rv_ref, rg_ref, sq_ref, sk_ref, svt_ref, gr_ref, gs_ref):
    h = _rms(x_ref[...], g_ref[...]).astype(BF16)
    hp = jnp.dot(perm_ref[...], h, preferred_element_type=F32).astype(BF16)
    cos = cos_ref[...]
    sin = sin_ref[...]

    def seg(lo, hi):
        return jnp.dot(h, w_ref[:, lo:hi], preferred_element_type=F32)

    def rope(t):
        outs = []
        for hd in range(RET_HEADS):
            th = t[:, hd * RET_DK:(hd + 1) * RET_DK]
            outs.append(th * cos + pltpu.roll(th, RET_DK // 2, 1) * sin)
        return jnp.concatenate(outs, axis=1)

    rq_ref[...] = rope(seg(0, 512)).astype(BF16)
    rk_ref[...] = (rope(seg(512, 1024)) * (RET_DK ** -0.5)).astype(BF16)
    rv_ref[...] = seg(1024, 1536).astype(BF16)
    rg_ref[...] = seg(1536, 2048).astype(BF16)
    sq_ref[...] = (seg(2048, 2560) * (SB_DH ** -0.5)).astype(BF16)
    sk_ref[...] = jnp.dot(hp, w_ref[:, 2560:3072], preferred_element_type=F32).astype(BF16)
    svt = lax.dot_general(wvt_ref[...], hp, (((1,), (1,)), ((), ())), preferred_element_type=F32)
    tk = svt_ref.shape[2]
    for kb in range(svt_ref.shape[0]):
        svt_ref[kb] = svt[:, kb * tk:(kb + 1) * tk].astype(BF16)
    gr_ref[...] = seg(3584, 4608).astype(BF16)
    gs_ref[...] = seg(4608, 5632).astype(BF16)


def _key_block_perm(tm, tk):
    grp = tk // SUBLANES
    new = np.arange(tm)
    blk, r = new // tk, new % tk
    old = blk * tk + (r % SUBLANES) * grp + r // SUBLANES
    p = np.zeros((tm, tm), np.float32)
    p[new, old] = 1.0
    return jnp.asarray(p, BF16)


def _proj(x2, g, w, wvt, cosf, sinf, seq):
    n, d = x2.shape
    tm = PROJ_ROWS
    tk = SB_BLOCK
    pos_blocks = seq // tm
    perm = _key_block_perm(tm, tk)
    row = lambda i: (i, 0)
    const = lambda i: (0, 0)
    half = jax.ShapeDtypeStruct((n, 512), BF16)
    full = jax.ShapeDtypeStruct((n, 1024), BF16)
    svt = jax.ShapeDtypeStruct((n // tk, 512, tk), BF16)
    hspec = pl.BlockSpec((tm, 512), row)
    fspec = pl.BlockSpec((tm, 1024), row)
    return pl.pallas_call(
        _proj_kernel,
        grid=(n // tm,),
        in_specs=[
            pl.BlockSpec((tm, d), row),
            pl.BlockSpec((1, d), const),
            pl.BlockSpec(w.shape, const),
            pl.BlockSpec(wvt.shape, const),
            pl.BlockSpec(perm.shape, const),
            pl.BlockSpec((tm, RET_DK), lambda i: (i % pos_blocks, 0)),
            pl.BlockSpec((tm, RET_DK), lambda i: (i % pos_blocks, 0)),
        ],
        out_specs=[hspec] * 6 + [pl.BlockSpec((tm // tk, 512, tk), lambda i: (i, 0, 0)), fspec, fspec],
        out_shape=[half] * 6 + [svt, full, full],
        compiler_params=_params("parallel"),
        name="proj",
    )(x2, g, w, wvt, perm, cosf, sinf)


def _ret_consts(tb):
    hd = np.arange(RET_HEADS, dtype=np.float64)
    lg = np.log(1.0 - np.exp2(-5.0 - hd))
    idx = np.arange(tb, dtype=np.float64)
    dist = np.abs(idx[:, None] - idx[None, :])
    chunk = (np.arange(tb) // CHUNK)
    allowed = chunk[None, :] <= chunk[:, None]
    dm = np.exp(lg[:, None, None] * dist) * allowed
    qdec = np.exp(lg[:, None] * (idx + 1.0))[:, :, None] * np.ones((1, 1, RET_DK))
    kdec = np.exp(lg[:, None] * (tb - 1.0 - idx))[:, :, None] * np.ones((1, 1, RET_DK))
    cdec = [float(v) for v in np.exp(lg * tb)]
    return (jnp.asarray(dm, F32), jnp.asarray(qdec, F32), jnp.asarray(kdec, F32), cdec)


def _ret_kernel(cdec, q_ref, k_ref, v_ref, g_ref, dm_ref, qd_ref, kd_ref, gn_ref, o_ref, st_ref):
    @pl.when(pl.program_id(1) == 0)
    def _():
        st_ref[...] = jnp.zeros_like(st_ref)

    for hd in range(RET_HEADS):
        sl = slice(hd * RET_DK, (hd + 1) * RET_DK)
        q = q_ref[:, sl]
        k = k_ref[:, sl]
        v = v_ref[:, sl]
        st = st_ref[hd]
        scores = lax.dot_general(q, k, (((1,), (1,)), ((), ())), preferred_element_type=F32)
        scores = scores * dm_ref[hd]
        qd = (q.astype(F32) * qd_ref[hd]).astype(BF16)
        o = (jnp.dot(scores.astype(BF16), v, preferred_element_type=F32)
             + jnp.dot(qd, st.astype(BF16), preferred_element_type=F32))
        kd = (k.astype(F32) * kd_ref[hd]).astype(BF16)
        st_ref[hd] = cdec[hd] * st + lax.dot_general(
            kd, v, (((0,), (0,)), ((), ())), preferred_element_type=F32)
        mu = jnp.mean(o, axis=-1, keepdims=True)
        oc = o - mu
        var = jnp.mean(oc * oc, axis=-1, keepdims=True)
        on = oc * lax.rsqrt(var + EPS) * gn_ref[:, sl]
        gate = g_ref[:, sl].astype(F32)
        o_ref[:, sl] = (gate * _sigmoid(gate) * on).astype(BF16)


def _retention(rq, rk, rv, rg, gn, batch, seq):
    n, w = rq.shape
    tb = RET_ROWS
    nc = seq // tb
    dm, qdec, kdec, cdec = _ret_consts(tb)
    blk = pl.BlockSpec((tb, w), lambda b, c: (b * nc + c, 0))
    c3 = lambda b, c: (0, 0, 0)
    return pl.pallas_call(
        functools.partial(_ret_kernel, cdec),
        grid=(batch, nc),
        in_specs=[blk, blk, blk, blk,
                  pl.BlockSpec(dm.shape, c3), pl.BlockSpec(qdec.shape, c3), pl.BlockSpec(kdec.shape, c3),
                  pl.BlockSpec((1, w), lambda b, c: (0, 0))],
        out_specs=blk,
        out_shape=jax.ShapeDtypeStruct((n, w), BF16),
        scratch_shapes=[pltpu.VMEM((RET_HEADS, RET_DK, RET_DK), F32)],
        compiler_params=_params("parallel", "arbitrary"),
        name="retention",
    )(rq, rk, rv, rg, dm, qdec, kdec, gn)


def _sb_kernel(q_ref, k_ref, vt_ref, o_ref, acc_ref):
    tq = q_ref.shape[0]
    tk = tq
    grp = tk // SUBLANES
    heads = q_ref.shape[1] // SB_DH
    wide = heads * tq
    assert tq & (tq - 1) == 0
    qi = pl.program_id(1)
    q = q_ref[...]
    lane = lax.broadcasted_iota(I32, q.shape, 1)
    q_all = jnp.concatenate(
        [jnp.where((lane >= hh * SB_DH) & (lane < (hh + 1) * SB_DH), q, jnp.zeros_like(q))
         for hh in range(heads)], axis=0)
    shp3 = (grp, SUBLANES, wide)
    key_pos = lax.broadcasted_iota(I32, shp3, 1) * grp + lax.broadcasted_iota(I32, shp3, 0)
    causal = key_pos < (lax.broadcasted_iota(I32, shp3, 2) & (tq - 1))
    row8 = lax.broadcasted_iota(I32, (SUBLANES, wide), 0)

    def block(kb, carry, diagonal):
        z = lax.dot_general(k_ref[kb], q_all, (((1,), (1,)), ((), ())),
                            preferred_element_type=F32) * LOG2E
        softplus = jnp.maximum(z, 0.0) + jnp.log2(1.0 + jnp.exp2(-jnp.abs(z)))
        z3 = z.reshape(shp3)
        sp3 = softplus.reshape(shp3)
        if diagonal:
            sp3 = jnp.where(causal, sp3, 0.0)
        run = jnp.zeros((SUBLANES, wide), F32)
        parts = [None] * grp
        for g in reversed(range(grp)):
            run = run - sp3[g]
            parts[g] = run
        tot = parts[0]
        inc = tot
        for d in (1, 2, 4):
            inc = inc + jnp.where(row8 + d < SUBLANES, pltpu.roll(inc, SUBLANES - d, 0), 0.0)
        base = inc - tot + carry
        a3 = jnp.exp2(z3 + jnp.stack(parts, axis=0) + base[None])
        if diagonal:
            a3 = jnp.where(causal, a3, 0.0)
        a = a3.reshape(tk, wide).astype(BF16)
        for hh in range(heads):
            rows = slice(hh * SB_DH, (hh + 1) * SB_DH)
            contrib = jnp.dot(vt_ref[kb, rows, :], a[:, hh * tq:(hh + 1) * tq], preferred_element_type=F32)
            if diagonal:
                acc_ref[rows, :] = contrib
            else:
                acc_ref[rows, :] += contrib
        return carry + jnp.broadcast_to(inc[0:1, :], carry.shape)

    carry = block(qi, jnp.zeros((SUBLANES, wide), F32), True)

    def cond(c):
        return (c[0] < qi) & (c[1] > 0)

    def body(c):
        new = block(qi - 1 - c[0], c[2], False)
        go = (jnp.max(new) > SB_SKIP_LOG2).astype(I32)
        return (c[0] + 1, go, new)

    lax.while_loop(cond, body, (jnp.int32(0), jnp.int32(1), carry))
    o_ref[...] = acc_ref[...].T.astype(BF16)


def _stick_breaking(sq, kperm, vtperm, batch, seq):
    n, w = sq.shape
    tq = SB_BLOCK
    nq = seq // tq
    return pl.pallas_call(
        _sb_kernel,
        grid=(batch, nq),
        in_specs=[
            pl.BlockSpec((tq, w), lambda b, i: (b * nq + i, 0)),
            pl.BlockSpec((None, nq, tq, w), lambda b, i: (b, 0, 0, 0)),
            pl.BlockSpec((None, nq, w, tq), lambda b, i: (b, 0, 0, 0)),
        ],
        out_specs=pl.BlockSpec((tq, w), lambda b, i: (b * nq + i, 0)),
        out_shape=jax.ShapeDtypeStruct((n, w), BF16),
        scratch_shapes=[pltpu.VMEM((w, tq), F32)],
        compiler_params=_params("parallel", "arbitrary"),
        name="stickbrk",
    )(sq, kperm, vtperm)


def _merge_kernel(ret_ref, sb_ref, gr_ref, gs_ref, x_ref, wbr_ref, wbs_ref, wo_ref, nf_ref,
                  wr_ref, br_ref, tri_ref,
                  x1_ref, h2_ref, ri_ref, rw_ref, cnt_ref, carry_ref):
    @pl.when(pl.program_id(0) == 0)
    def _():
        carry_ref[...] = jnp.zeros_like(carry_ref)

    tm = x_ref.shape[0]
    merged = (_sigmoid(gr_ref[...].astype(F32)) * jnp.dot(ret_ref[...], wbr_ref[...], preferred_element_type=F32)
              + _sigmoid(gs_ref[...].astype(F32)) * jnp.dot(sb_ref[...], wbs_ref[...], preferred_element_type=F32))
    x1 = x_ref[...] + jnp.dot(merged.astype(BF16), wo_ref[...], preferred_element_type=F32)
    x1_ref[...] = x1
    h2 = _rms(x1, nf_ref[...])
    _to_row_tiles(h2_ref, h2)
    logits = jnp.dot(h2.astype(BF16), wr_ref[...], preferred_element_type=F32) + br_ref[...]
    lt = logits.T
    neg = jnp.float32(-jnp.inf)

    grow = lax.broadcasted_iota(I32, (SUBLANES, tm), 0)
    gl = jnp.where(grow < N_GROUPS, lt[0:SUBLANES], neg)
    gmax = jnp.max(gl, axis=0, keepdims=True)
    g_idx = jnp.min(jnp.where(gl == gmax, grow, N_GROUPS), axis=0, keepdims=True)
    p_grp = 1.0 / jnp.sum(jnp.exp(gl - gmax), axis=0, keepdims=True)

    erow = lax.broadcasted_iota(I32, (N_EXPERTS, tm), 0)
    el = lt[EXPERT_COL0:EXPERT_COL0 + N_EXPERTS]
    el = jnp.where((erow >= g_idx * EXPERTS_PER_GROUP) & (erow < (g_idx + 1) * EXPERTS_PER_GROUP), el, neg)
    m1 = jnp.max(el, axis=0, keepdims=True)
    i1 = jnp.min(jnp.where(el == m1, erow, N_EXPERTS), axis=0, keepdims=True)
    el2 = jnp.where(erow == i1, neg, el)
    m2 = jnp.max(el2, axis=0, keepdims=True)
    i2 = jnp.min(jnp.where(el2 == m2, erow, N_EXPERTS), axis=0, keepdims=True)
    e2 = jnp.exp(m2 - m1)
    w1 = p_grp / (1.0 + e2)
    w2 = p_grp * e2 / (1.0 + e2)

    sel1 = erow == i1
    sel2 = erow == i2
    onehot = (sel1 | sel2).astype(BF16)
    cinc = jnp.dot(onehot, tri_ref[...], preferred_element_type=F32)
    carry = carry_ref[...]
    pos = carry[:, 0:1] + cinc - 1.0
    r1 = jnp.sum(jnp.where(sel1, pos, 0.0), axis=0, keepdims=True)
    r2 = jnp.sum(jnp.where(sel2, pos, 0.0), axis=0, keepdims=True)
    carry_new = carry + jnp.sum(onehot.astype(F32), axis=1, keepdims=True)
    carry_ref[...] = carry_new
    cnt_ref[...] = carry_new

    ri_ref[...] = jnp.zeros_like(ri_ref)
    ri_ref[0:1, :] = i1
    ri_ref[1:2, :] = i2
    ri_ref[2:3, :] = r1.astype(I32)
    ri_ref[3:4, :] = r2.astype(I32)
    wrow = lax.broadcasted_iota(I32, (ROUTE_COLS, tm), 0)
    wt = jnp.where(wrow == 0, w1, jnp.where(wrow == 1, w2, 0.0))
    rw_ref[...] = wt.T


def _merge(ret, sb, g_ret, g_sb, x2, wbr, wbs, wo, nf, wr, br):
    n, d = x2.shape
    tm = MERGE_ROWS
    tri = jnp.asarray(np.triu(np.ones((tm, tm), np.float32)), BF16)
    row = lambda i: (i, 0)
    const = lambda i: (0, 0)
    full = lambda a: pl.BlockSpec(a.shape, const)
    return pl.pallas_call(
        _merge_kernel,
        grid=(n // tm,),
        in_specs=[pl.BlockSpec((tm, 512), row), pl.BlockSpec((tm, 512), row),
                  pl.BlockSpec((tm, d), row), pl.BlockSpec((tm, d), row), pl.BlockSpec((tm, d), row),
                  full(wbr), full(wbs), full(wo), full(nf), full(wr), full(br), full(tri)],
        out_specs=[pl.BlockSpec((tm, d), row), pl.BlockSpec((tm * ROW_TILES, LANES), row),
                   pl.BlockSpec((SUBLANES, tm), lambda i: (0, i)),
                   pl.BlockSpec((tm, ROUTE_COLS), row),
                   pl.BlockSpec((N_EXPERTS, LANES), const)],
        out_shape=[jax.ShapeDtypeStruct((n, d), F32), jax.ShapeDtypeStruct((n * ROW_TILES, LANES), F32),
                   jax.ShapeDtypeStruct((SUBLANES, n), I32),
                   jax.ShapeDtypeStruct((n, ROUTE_COLS), F32),
                   jax.ShapeDtypeStruct((N_EXPERTS, LANES), F32)],
        scratch_shapes=[pltpu.VMEM((N_EXPERTS, LANES), F32)],
        compiler_params=_params("arbitrary"),
        name="merge",
    )(ret, sb, g_ret, g_sb, x2, wbr, wbs, wo, nf, wr, br, tri)


def _issue_rows(idx_ref, src_hbm, dst, sem, rows):
    def body(r, c):
        tok = idx_ref[0, 0, r]
        pltpu.make_async_copy(src_hbm.at[pl.ds(pl.multiple_of(tok * ROW_TILES, ROW_TILES), ROW_TILES)],
                              dst.at[pl.ds(pl.multiple_of(r * ROW_TILES, ROW_TILES), ROW_TILES)], sem).start()
        return c
    lax.fori_loop(0, rows, body, 0, unroll=8)


def _expert_kernel(be_ref, nu_ref, idx_ref, idxn_ref, x_hbm, wg_ref, wu_ref, wd_ref, o_ref, xbuf, sem):
    j = pl.program_id(0)
    used = nu_ref[0]
    rows = xbuf.shape[1] // ROW_TILES
    slot = j % 2

    @pl.when(j == 0)
    def _():
        _issue_rows(idx_ref, x_hbm, xbuf.at[0], sem.at[0], rows)

    @pl.when(j + 1 < used)
    def _():
        _issue_rows(idxn_ref, x_hbm, xbuf.at[1 - slot], sem.at[1 - slot], rows)

    @pl.when(j < used)
    def _():
        pltpu.make_async_copy(xbuf.at[slot], xbuf.at[slot], sem.at[slot]).wait()
        xb = _from_row_tiles(xbuf.at[slot], 0, rows).astype(BF16)
        gate = jnp.dot(xb, wg_ref[...], preferred_element_type=F32)
        up = jnp.dot(xb, wu_ref[...], preferred_element_type=F32)
        hid = (gate * _sigmoid(gate) * up).astype(BF16)
        _to_row_tiles(o_ref, jnp.dot(hid, wd_ref[...], preferred_element_type=F32))

    @pl.when(j >= used)
    def _():
        o_ref[...] = jnp.zeros_like(o_ref)


def _experts(block_e, n_used, buf_tok3, h2, wg, wu, wd):
    nb, _, tmb = buf_tok3.shape
    d = wg.shape[1]
    de = wg.shape[2]
    wmap = lambda j, be, nu: (be[jnp.minimum(j, nu[0] - 1)], 0, 0)
    return pl.pallas_call(
        _expert_kernel,
        grid_spec=pltpu.PrefetchScalarGridSpec(
            num_scalar_prefetch=2,
            grid=(nb,),
            in_specs=[
                pl.BlockSpec((1, 1, tmb), lambda j, be, nu: (j, 0, 0), memory_space=pltpu.SMEM),
                pl.BlockSpec((1, 1, tmb), lambda j, be, nu: (jnp.minimum(j + 1, nb - 1), 0, 0),
                             memory_space=pltpu.SMEM),
                pl.BlockSpec(memory_space=pl.ANY),
                pl.BlockSpec((None, d, de), wmap),
                pl.BlockSpec((None, d, de), wmap),
                pl.BlockSpec((None, de, d), wmap),
            ],
            out_specs=pl.BlockSpec((tmb * ROW_TILES, LANES), lambda j, be, nu: (j, 0)),
            scratch_shapes=[pltpu.VMEM((2, tmb * ROW_TILES, LANES), F32), pltpu.SemaphoreType.DMA((2,))],
        ),
        out_shape=jax.ShapeDtypeStruct((nb * tmb * ROW_TILES, LANES), F32),
        compiler_params=_params("arbitrary"),
        name="experts",
    )(block_e, n_used, buf_tok3, buf_tok3, h2, wg, wu, wd)


def _final_kernel(idx_ref, idxn_ref, x1_ref, rw_ref, p_ref, yb_hbm, npl_ref, wpg_ref, wp_ref, nfin_ref,
                  o_ref, gbuf, sem):
    i = pl.program_id(0)
    nt = pl.num_programs(0)
    tm = x1_ref.shape[0]
    slot = i % 2

    @pl.when(i == 0)
    def _():
        _issue_rows(idx_ref, yb_hbm, gbuf.at[0], sem.at[0], 2 * tm)

    @pl.when(i + 1 < nt)
    def _():
        _issue_rows(idxn_ref, yb_hbm, gbuf.at[1 - slot], sem.at[1 - slot], 2 * tm)

    pltpu.make_async_copy(gbuf.at[slot], gbuf.at[slot], sem.at[slot]).wait()
    rw = rw_ref[...]
    x2 = (x1_ref[...] + rw[:, 0:1] * _from_row_tiles(gbuf.at[slot], 0, tm)
          + rw[:, 1:2] * _from_row_tiles(gbuf.at[slot], tm, tm))
    hp = _rms(x2, npl_ref[...]).astype(BF16)
    gate = _sigmoid(jnp.dot(hp, wpg_ref[...], preferred_element_type=F32))
    emb = jnp.dot(p_ref[...].astype(BF16), wp_ref[...], preferred_element_type=F32)
    x3 = x2 + gate * emb
    o_ref[...] = _rms(x3, nfin_ref[...])


def _final(dest3, x1, rw, p2, yb, npl, wpg, wp, nfin):
    n, d = x1.shape
    tm = FINAL_ROWS
    nt = n // tm
    row = lambda i: (i, 0)
    const = lambda i: (0, 0)
    full = lambda a: pl.BlockSpec(a.shape, const)
    return pl.pallas_call(
        _final_kernel,
        grid=(nt,),
        in_specs=[
            pl.BlockSpec((1, 1, 2 * tm), lambda i: (i, 0, 0), memory_space=pltpu.SMEM),
            pl.BlockSpec((1, 1, 2 * tm), lambda i: (jnp.minimum(i + 1, nt - 1), 0, 0), memory_space=pltpu.SMEM),
            pl.BlockSpec((tm, d), row),
            pl.BlockSpec((tm, ROUTE_COLS), row),
            pl.BlockSpec((tm, p2.shape[1]), row),
            pl.BlockSpec(memory_space=pl.ANY),
            full(npl), full(wpg), full(wp), full(nfin),
        ],
        out_specs=pl.BlockSpec((tm, d), row),
        out_shape=jax.ShapeDtypeStruct((n, d), F32),
        scratch_shapes=[pltpu.VMEM((2, 2 * tm * ROW_TILES, LANES), F32), pltpu.SemaphoreType.DMA((2,))],
        compiler_params=_params("arbitrary"),
        name="final",
    )(dest3, dest3, x1, rw, p2, yb, npl, wpg, wp, nfin)


def _layer(x2, p2, batch, seq, cosf, sinf, norm_mix, w_in, ret_gn, w_br_ret, w_br_sb, w_out, norm_ffn,
           w_grp, b_grp, w_exp, b_exp, w_gate, w_up, w_down, norm_ple, w_ple, w_ple_gate, out_gain):
    n, d = x2.shape
    wv_lo = 2 * 512 + 2 * 512 + 2 * 512
    rq, rk, rv, rg, sq, skp, svt, g_ret, g_sb = _proj(
        x2, norm_mix[None, :], w_in.astype(BF16), w_in[:, wv_lo:wv_lo + 512].T.astype(BF16), cosf, sinf, seq)
    ret = _retention(rq, rk, rv, rg, ret_gn[None, :], batch, seq)

    tk = SB_BLOCK
    nkb = seq // tk
    kperm = skp.reshape(batch, nkb, tk, skp.shape[1])
    vtperm = svt.reshape(batch, nkb, svt.shape[1], tk)
    sb = _stick_breaking(sq, kperm, vtperm, batch, seq)

    wr = jnp.zeros((d, ROUTE_COLS), F32)
    wr = wr.at[:, 0:N_GROUPS].set(w_grp).at[:, EXPERT_COL0:EXPERT_COL0 + N_EXPERTS].set(w_exp)
    br = jnp.zeros((1, ROUTE_COLS), F32)
    br = br.at[0, 0:N_GROUPS].set(b_grp).at[0, EXPERT_COL0:EXPERT_COL0 + N_EXPERTS].set(b_exp)
    x1, h2, ri, rw, cnt = _merge(ret, sb, g_ret, g_sb, x2, w_br_ret.astype(BF16), w_br_sb.astype(BF16),
                                 w_out.astype(BF16), norm_ffn[None, :], wr.astype(BF16), br)

    tmb = EXPERT_ROWS
    counts = cnt[:, 0].astype(I32)
    padded = (counts + tmb - 1) // tmb * tmb
    pad_end = jnp.cumsum(padded)
    pad_start = pad_end - padded
    eids = jnp.arange(N_EXPERTS, dtype=I32)
    dest = jnp.sum(jnp.where(ri[0:2, :, None] == eids, pad_start, 0), axis=-1) + ri[2:4]
    m = 2 * n + N_EXPERTS * tmb
    nb = m // tmb
    tok = jnp.broadcast_to(jnp.arange(n, dtype=I32)[None, :], (2, n))
    buf_tok = jnp.zeros((m,), I32).at[dest.reshape(-1)].set(tok.reshape(-1))
    block_e = jnp.minimum(
        jnp.sum((pad_end[None, :] <= (jnp.arange(nb, dtype=I32) * tmb)[:, None]).astype(I32), axis=1),
        N_EXPERTS - 1).astype(I32)
    n_used = (pad_end[-1:] // tmb).astype(I32)
    yb = _experts(block_e, n_used, buf_tok.reshape(nb, 1, tmb), h2,
                  w_gate.astype(BF16), w_up.astype(BF16), w_down.astype(BF16))

    tf = FINAL_ROWS
    dest3 = dest.reshape(2, n // tf, tf).transpose(1, 0, 2).reshape(n // tf, 1, 2 * tf)
    return _final(dest3, x1, rw, p2, yb, norm_ple[None, :], w_ple_gate.astype(BF16), w_ple.astype(BF16),
                  out_gain)


def kernel(x, p, norm_mix, w_in, ret_gn, w_br_ret, w_br_sb, w_out, norm_ffn, w_grp, b_grp, w_exp, b_exp,
           w_gate, w_up, w_down, norm_ple, w_ple, w_ple_gate, norm_final):
    batch, seq, d = x.shape
    depth = p.shape[0]
    assert depth == 1, "the final RMSNorm is fused into the last layer's kernel"
    pos = jnp.arange(seq, dtype=F32)
    inv_freq = ROPE_BASE ** (-jnp.arange(0, RET_DK, 2, dtype=F32) / RET_DK)
    ang = pos[:, None] * inv_freq[None, :]
    cos, sin = jnp.cos(ang), jnp.sin(ang)
    cosf = jnp.concatenate([cos, cos], axis=1)
    sinf = jnp.concatenate([-sin, sin], axis=1)
    x2 = x.reshape(batch * seq, d)
    i = 0
    out = _layer(x2, p[i].reshape(batch * seq, -1), batch, seq, cosf, sinf, norm_mix[i], w_in[i], ret_gn[i],
                 w_br_ret[i], w_br_sb[i], w_out[i], norm_ffn[i], w_grp[i], b_grp[i], w_exp[i], b_exp[i],
                 w_gate[i], w_up[i], w_down[i], norm_ple[i], w_ple[i], w_ple_gate[i], norm_final[None, :])
    return out.reshape(batch, seq, d)
```

```python
import functools

import numpy as np
import jax
import jax.numpy as jnp
from jax import lax
from jax.experimental import pallas as pl
from jax.experimental.pallas import tpu as pltpu

F32 = jnp.float32
BF16 = jnp.bfloat16
I32 = jnp.int32

EPS = 1e-6
CHUNK = 64
RET_HEADS = 4
RET_DK = 128
SB_HEADS = 8
SB_DH = 64
ROPE_BASE = 10000.0
N_GROUPS = 4
EXPERTS_PER_GROUP = 4
N_EXPERTS = N_GROUPS * EXPERTS_PER_GROUP
PAIRS_PER_GROUP = EXPERTS_PER_GROUP * (EXPERTS_PER_GROUP - 1) // 2
N_CLASSES = N_GROUPS * PAIRS_PER_GROUP
CLASS_ROWS = 32

LANES = 128
SUBLANES = 8
VMEM_LIMIT = 56 * 1024 * 1024

PROJ_ROWS = 512
RET_ROWS = 512
SB_BLOCK = 256
MERGE_ROWS = 512
EXPERT_ROWS = 256
FINAL_ROWS = 256
ROUTE_COLS = 128
ROW_TILES = 8
EXPERT_COL0 = 8
LOG2E = 1.4426950408889634
SB_SKIP_LOG2 = -160.0


def _rms(x, g):
    return x * lax.rsqrt(jnp.mean(x * x, axis=-1, keepdims=True) + EPS) * g


def _sigmoid(x):
    return 1.0 / (1.0 + jnp.exp(-x))


def _to_row_tiles(ref, val):
    rows = val.shape[0]
    for s in range(ROW_TILES):
        ref[pl.ds(s, rows, stride=ROW_TILES), :] = val[:, s * LANES:(s + 1) * LANES]


def _from_row_tiles(ref, first, rows):
    return jnp.concatenate([ref[pl.ds(first * ROW_TILES + s, rows, stride=ROW_TILES), :]
                            for s in range(ROW_TILES)], axis=1)


def _params(*sem):
    return pltpu.CompilerParams(dimension_semantics=sem, vmem_limit_bytes=VMEM_LIMIT)


def _proj_kernel(x_ref, g_ref, w_ref, wvt_ref, perm_ref, cos_ref, sin_ref,
                 rq_ref, rk_ref, rv_ref, rg_ref, sq_ref, sk_ref, svt_ref, gr_ref, gs_ref):
    h = _rms(x_ref[...], g_ref[...]).astype(BF16)
    hp = jnp.dot(perm_ref[...], h, preferred_element_type=F32).astype(BF16)
    cos = cos_ref[...]
    sin = sin_ref[...]

    def seg(lo, hi):
        return jnp.dot(h, w_ref[:, lo:hi], preferred_element_type=F32)

    def rope(t):
        outs = []
        for hd in range(RET_HEADS):
            th = t[:, hd * RET_DK:(hd + 1) * RET_DK]
            outs.append(th * cos + pltpu.roll(th, RET_DK // 2, 1) * sin)
        return jnp.concatenate(outs, axis=1)

    rq_ref[...] = rope(seg(0, 512)).astype(BF16)
    rk_ref[...] = (rope(seg(512, 1024)) * (RET_DK ** -0.5)).astype(BF16)
    rv_ref[...] = seg(1024, 1536).astype(BF16)
    rg_ref[...] = seg(1536, 2048).astype(BF16)
    sq_ref[...] = (seg(2048, 2560) * (SB_DH ** -0.5)).astype(BF16)
    sk_ref[...] = jnp.dot(hp, w_ref[:, 2560:3072], preferred_element_type=F32).astype(BF16)
    svt = lax.dot_general(wvt_ref[...], hp, (((1,), (1,)), ((), ())), preferred_element_type=F32)
    tk = svt_ref.shape[2]
    for kb in range(svt_ref.shape[0]):
        svt_ref[kb] = svt[:, kb * tk:(kb + 1) * tk].astype(BF16)
    gr_ref[...] = seg(3584, 4608).astype(BF16)
    gs_ref[...] = seg(4608, 5632).astype(BF16)


def _key_block_perm(tm, tk):
    grp = tk // SUBLANES
    new = np.arange(tm)
    blk, r = new // tk, new % tk
    old = blk * tk + (r % SUBLANES) * grp + r // SUBLANES
    p = np.zeros((tm, tm), np.float32)
    p[new, old] = 1.0
    return jnp.asarray(p, BF16)


def _proj(x2, g, w, wvt, cosf, sinf, seq):
    n, d = x2.shape
    tm = PROJ_ROWS
    tk = SB_BLOCK
    pos_blocks = seq // tm
    perm = _key_block_perm(tm, tk)
    row = lambda i: (i, 0)
    const = lambda i: (0, 0)
    half = jax.ShapeDtypeStruct((n, 512), BF16)
    full = jax.ShapeDtypeStruct((n, 1024), BF16)
    svt = jax.ShapeDtypeStruct((n // tk, 512, tk), BF16)
    hspec = pl.BlockSpec((tm, 512), row)
    fspec = pl.BlockSpec((tm, 1024), row)
    return pl.pallas_call(
        _proj_kernel,
        grid=(n // tm,),
        in_specs=[
            pl.BlockSpec((tm, d), row),
            pl.BlockSpec((1, d), const),
            pl.BlockSpec(w.shape, const),
            pl.BlockSpec(wvt.shape, const),
            pl.BlockSpec(perm.shape, const),
            pl.BlockSpec((tm, RET_DK), lambda i: (i % pos_blocks, 0)),
            pl.BlockSpec((tm, RET_DK), lambda i: (i % pos_blocks, 0)),
        ],
        out_specs=[hspec] * 6 + [pl.BlockSpec((tm // tk, 512, tk), lambda i: (i, 0, 0)), fspec, fspec],
        out_shape=[half] * 6 + [svt, full, full],
        compiler_params=_params("parallel"),
        name="proj",
    )(x2, g, w, wvt, perm, cosf, sinf)


def _ret_consts(tb):
    hd = np.arange(RET_HEADS, dtype=np.float64)
    lg = np.log(1.0 - np.exp2(-5.0 - hd))
    idx = np.arange(tb, dtype=np.float64)
    dist = np.abs(idx[:, None] - idx[None, :])
    chunk = (np.arange(tb) // CHUNK)
    allowed = chunk[None, :] <= chunk[:, None]
    dm = np.exp(lg[:, None, None] * dist) * allowed
    qdec = np.exp(lg[:, None] * (idx + 1.0))[:, :, None] * np.ones((1, 1, RET_DK))
    kdec = np.exp(lg[:, None] * (tb - 1.0 - idx))[:, :, None] * np.ones((1, 1, RET_DK))
    cdec = [float(v) for v in np.exp(lg * tb)]
    return (jnp.asarray(dm, F32), jnp.asarray(qdec, F32), jnp.asarray(kdec, F32), cdec)


def _ret_kernel(cdec, q_ref, k_ref, v_ref, g_ref, dm_ref, qd_ref, kd_ref, gn_ref, o_ref, st_ref):
    @pl.when(pl.program_id(1) == 0)
    def _():
        st_ref[...] = jnp.zeros_like(st_ref)

    for hd in range(RET_HEADS):
        sl = slice(hd * RET_DK, (hd + 1) * RET_DK)
        q = q_ref[:, sl]
        k = k_ref[:, sl]
        v = v_ref[:, sl]
        st = st_ref[hd]
        scores = lax.dot_general(q, k, (((1,), (1,)), ((), ())), preferred_element_type=F32)
        scores = scores * dm_ref[hd]
        qd = (q.astype(F32) * qd_ref[hd]).astype(BF16)
        o = (jnp.dot(scores.astype(BF16), v, preferred_element_type=F32)
             + jnp.dot(qd, st.astype(BF16), preferred_element_type=F32))
        kd = (k.astype(F32) * kd_ref[hd]).astype(BF16)
        st_ref[hd] = cdec[hd] * st + lax.dot_general(
            kd, v, (((0,), (0,)), ((), ())), preferred_element_type=F32)
        mu = jnp.mean(o, axis=-1, keepdims=True)
        oc = o - mu
        var = jnp.mean(oc * oc, axis=-1, keepdims=True)
        on = oc * lax.rsqrt(var + EPS) * gn_ref[:, sl]
        gate = g_ref[:, sl].astype(F32)
        o_ref[:, sl] = (gate * _sigmoid(gate) * on).astype(BF16)


def _retention(rq, rk, rv, rg, gn, batch, seq):
    n, w = rq.shape
    tb = RET_ROWS
    nc = seq // tb
    dm, qdec, kdec, cdec = _ret_consts(tb)
    blk = pl.BlockSpec((tb, w), lambda b, c: (b * nc + c, 0))
    c3 = lambda b, c: (0, 0, 0)
    return pl.pallas_call(
        functools.partial(_ret_kernel, cdec),
        grid=(batch, nc),
        in_specs=[blk, blk, blk, blk,
                  pl.BlockSpec(dm.shape, c3), pl.BlockSpec(qdec.shape, c3), pl.BlockSpec(kdec.shape, c3),
                  pl.BlockSpec((1, w), lambda b, c: (0, 0))],
        out_specs=blk,
        out_shape=jax.ShapeDtypeStruct((n, w), BF16),
        scratch_shapes=[pltpu.VMEM((RET_HEADS, RET_DK, RET_DK), F32)],
        compiler_params=_params("parallel", "arbitrary"),
        name="retention",
    )(rq, rk, rv, rg, dm, qdec, kdec, gn)


def _sb_kernel(q_ref, k_ref, vt_ref, o_ref, acc_ref):
    tq = q_ref.shape[0]
    tk = tq
    grp = tk // SUBLANES
    heads = q_ref.shape[1] // SB_DH
    wide = heads * tq
    assert tq & (tq - 1) == 0
    qi = pl.program_id(1)
    q = q_ref[...]
    lane = lax.broadcasted_iota(I32, q.shape, 1)
    q_all = jnp.concatenate(
        [jnp.where((lane >= hh * SB_DH) & (lane < (hh + 1) * SB_DH), q, jnp.zeros_like(q))
         for hh in range(heads)], axis=0)
    shp3 = (grp, SUBLANES, wide)
    key_pos = lax.broadcasted_iota(I32, shp3, 1) * grp + lax.broadcasted_iota(I32, shp3, 0)
    causal = key_pos < (lax.broadcasted_iota(I32, shp3, 2) & (tq - 1))
    row8 = lax.broadcasted_iota(I32, (SUBLANES, wide), 0)

    def block(kb, carry, diagonal):
        z = lax.dot_general(k_ref[kb], q_all, (((1,), (1,)), ((), ())),
                            preferred_element_type=F32) * LOG2E
        softplus = jnp.maximum(z, 0.0) + jnp.log2(1.0 + jnp.exp2(-jnp.abs(z)))
        z3 = z.reshape(shp3)
        sp3 = softplus.reshape(shp3)
        if diagonal:
            sp3 = jnp.where(causal, sp3, 0.0)
        run = jnp.zeros((SUBLANES, wide), F32)
        parts = [None] * grp
        for g in reversed(range(grp)):
            run = run - sp3[g]
            parts[g] = run
        tot = parts[0]
        inc = tot
        for d in (1, 2, 4):
            inc = inc + jnp.where(row8 + d < SUBLANES, pltpu.roll(inc, SUBLANES - d, 0), 0.0)
        base = inc - tot + carry
        a3 = jnp.exp2(z3 + jnp.stack(parts, axis=0) + base[None])
        if diagonal:
            a3 = jnp.where(causal, a3, 0.0)
        a = a3.reshape(tk, wide).astype(BF16)
        for hh in range(heads):
            rows = slice(hh * SB_DH, (hh + 1) * SB_DH)
            contrib = jnp.dot(vt_ref[kb, rows, :], a[:, hh * tq:(hh + 1) * tq], preferred_element_type=F32)
            if diagonal:
                acc_ref[rows, :] = contrib
            else:
                acc_ref[rows, :] += contrib
        return carry + jnp.broadcast_to(inc[0:1, :], carry.shape)

    carry = block(qi, jnp.zeros((SUBLANES, wide), F32), True)

    def cond(c):
        return (c[0] < qi) & (c[1] > 0)

    def body(c):
        new = block(qi - 1 - c[0], c[2], False)
        go = (jnp.max(new) > SB_SKIP_LOG2).astype(I32)
        return (c[0] + 1, go, new)

    lax.while_loop(cond, body, (jnp.int32(0), jnp.int32(1), carry))
    o_ref[...] = acc_ref[...].T.astype(BF16)


def _stick_breaking(sq, kperm, vtperm, batch, seq):
    n, w = sq.shape
    tq = SB_BLOCK
    nq = seq // tq
    return pl.pallas_call(
        _sb_kernel,
        grid=(batch, nq),
        in_specs=[
            pl.BlockSpec((tq, w), lambda b, i: (b * nq + i, 0)),
            pl.BlockSpec((None, nq, tq, w), lambda b, i: (b, 0, 0, 0)),
            pl.BlockSpec((None, nq, w, tq), lambda b, i: (b, 0, 0, 0)),
        ],
        out_specs=pl.BlockSpec((tq, w), lambda b, i: (b * nq + i, 0)),
        out_shape=jax.ShapeDtypeStruct((n, w), BF16),
        scratch_shapes=[pltpu.VMEM((w, tq), F32)],
        compiler_params=_params("parallel", "arbitrary"),
        name="stickbrk",
    )(sq, kperm, vtperm)


def _merge_kernel(ret_ref, sb_ref, gr_ref, gs_ref, x_ref, wbr_ref, wbs_ref, wo_ref, nf_ref,
                  wr_ref, br_ref, tri_ref,
                  x1_ref, h2_ref, ri_ref, rw_ref, cnt_ref, carry_ref):
    @pl.when(pl.program_id(0) == 0)
    def _():
        carry_ref[...] = jnp.zeros_like(carry_ref)

    tm = x_ref.shape[0]
    merged = (_sigmoid(gr_ref[...].astype(F32)) * jnp.dot(ret_ref[...], wbr_ref[...], preferred_element_type=F32)
              + _sigmoid(gs_ref[...].astype(F32)) * jnp.dot(sb_ref[...], wbs_ref[...], preferred_element_type=F32))
    x1 = x_ref[...] + jnp.dot(merged.astype(BF16), wo_ref[...], preferred_element_type=F32)
    x1_ref[...] = x1
    h2 = _rms(x1, nf_ref[...])
    _to_row_tiles(h2_ref, h2)
    logits = jnp.dot(h2.astype(BF16), wr_ref[...], preferred_element_type=F32) + br_ref[...]
    lt = logits.T
    neg = jnp.float32(-jnp.inf)

    grow = lax.broadcasted_iota(I32, (SUBLANES, tm), 0)
    gl = jnp.where(grow < N_GROUPS, lt[0:SUBLANES], neg)
    gmax = jnp.max(gl, axis=0, keepdims=True)
    g_idx = jnp.min(jnp.where(gl == gmax, grow, N_GROUPS), axis=0, keepdims=True)
    p_grp = 1.0 / jnp.sum(jnp.exp(gl - gmax), axis=0, keepdims=True)

    erow = lax.broadcasted_iota(I32, (N_EXPERTS, tm), 0)
    el = lt[EXPERT_COL0:EXPERT_COL0 + N_EXPERTS]
    el = jnp.where((erow >= g_idx * EXPERTS_PER_GROUP) & (erow < (g_idx + 1) * EXPERTS_PER_GROUP), el, neg)
    m1 = jnp.max(el, axis=0, keepdims=True)
    i1 = jnp.min(jnp.where(el == m1, erow, N_EXPERTS), axis=0, keepdims=True)
    el2 = jnp.where(erow == i1, neg, el)
    m2 = jnp.max(el2, axis=0, keepdims=True)
    i2 = jnp.min(jnp.where(el2 == m2, erow, N_EXPERTS), axis=0, keepdims=True)
    e2 = jnp.exp(m2 - m1)
    w1 = p_grp / (1.0 + e2)
    w2 = p_grp * e2 / (1.0 + e2)

    l1 = i1 - g_idx * EXPERTS_PER_GROUP
    l2 = i2 - g_idx * EXPERTS_PER_GROUP
    lo = jnp.minimum(l1, l2)
    hi = jnp.maximum(l1, l2)
    cls = g_idx * PAIRS_PER_GROUP + ((lo * (2 * EXPERTS_PER_GROUP - 1 - lo)) >> 1) + (hi - lo - 1)
    w_lo = jnp.where(l1 < l2, w1, w2)
    w_hi = jnp.where(l1 < l2, w2, w1)

    crow = lax.broadcasted_iota(I32, (CLASS_ROWS, tm), 0)
    sel = crow == cls
    onehot = sel.astype(BF16)
    cinc = jnp.dot(onehot, tri_ref[...], preferred_element_type=F32)
    carry = carry_ref[...]
    rank = jnp.sum(jnp.where(sel, carry[:, 0:1] + cinc - 1.0, 0.0), axis=0, keepdims=True)
    carry_new = carry + jnp.sum(onehot.astype(F32), axis=1, keepdims=True)
    carry_ref[...] = carry_new
    cnt_ref[...] = carry_new

    ri_ref[...] = jnp.zeros_like(ri_ref)
    ri_ref[0:1, :] = cls
    ri_ref[1:2, :] = rank.astype(I32)
    wrow = lax.broadcasted_iota(I32, (ROUTE_COLS, tm), 0)
    wt = jnp.where(wrow == 0, w_lo, jnp.where(wrow == 1, w_hi, 0.0))
    rw_ref[...] = wt.T


def _merge(ret, sb, g_ret, g_sb, x2, wbr, wbs, wo, nf, wr, br):
    n, d = x2.shape
    tm = MERGE_ROWS
    tri = jnp.asarray(np.triu(np.ones((tm, tm), np.float32)), BF16)
    row = lambda i: (i, 0)
    const = lambda i: (0, 0)
    full = lambda a: pl.BlockSpec(a.shape, const)
    return pl.pallas_call(
        _merge_kernel,
        grid=(n // tm,),
        in_specs=[pl.BlockSpec((tm, 512), row), pl.BlockSpec((tm, 512), row),
                  pl.BlockSpec((tm, d), row), pl.BlockSpec((tm, d), row), pl.BlockSpec((tm, d), row),
                  full(wbr), full(wbs), full(wo), full(nf), full(wr), full(br), full(tri)],
        out_specs=[pl.BlockSpec((tm, d), row), pl.BlockSpec((tm * ROW_TILES, LANES), row),
                   pl.BlockSpec((SUBLANES, tm), lambda i: (0, i)),
                   pl.BlockSpec((tm, ROUTE_COLS), row),
                   pl.BlockSpec((CLASS_ROWS, LANES), const)],
        out_shape=[jax.ShapeDtypeStruct((n, d), F32), jax.ShapeDtypeStruct((n * ROW_TILES, LANES), F32),
                   jax.ShapeDtypeStruct((SUBLANES, n), I32),
                   jax.ShapeDtypeStruct((n, ROUTE_COLS), F32),
                   jax.ShapeDtypeStruct((CLASS_ROWS, LANES), F32)],
        scratch_shapes=[pltpu.VMEM((CLASS_ROWS, LANES), F32)],
        compiler_params=_params("arbitrary"),
        name="merge",
    )(ret, sb, g_ret, g_sb, x2, wbr, wbs, wo, nf, wr, br, tri)


def _issue_rows(idx_ref, src_hbm, dst, sem, rows):
    def body(r, c):
        tok = idx_ref[0, 0, r]
        pltpu.make_async_copy(src_hbm.at[pl.ds(pl.multiple_of(tok * ROW_TILES, ROW_TILES), ROW_TILES)],
                              dst.at[pl.ds(pl.multiple_of(r * ROW_TILES, ROW_TILES), ROW_TILES)], sem).start()
        return c
    lax.fori_loop(0, rows, body, 0, unroll=8)


def _expert_kernel(bl_ref, bh_ref, nu_ref, idx_ref, idxn_ref, x_hbm,
                   wgl_ref, wul_ref, wdl_ref, wgh_ref, wuh_ref, wdh_ref, o_ref, xbuf, sem):
    j = pl.program_id(0)
    used = nu_ref[0]
    rows = xbuf.shape[1] // ROW_TILES
    slot = j % 2

    @pl.when(j == 0)
    def _():
        _issue_rows(idx_ref, x_hbm, xbuf.at[0], sem.at[0], rows)

    @pl.when(j + 1 < used)
    def _():
        _issue_rows(idxn_ref, x_hbm, xbuf.at[1 - slot], sem.at[1 - slot], rows)

    @pl.when(j < used)
    def _():
        pltpu.make_async_copy(xbuf.at[slot], xbuf.at[slot], sem.at[slot]).wait()
        xb = _from_row_tiles(xbuf.at[slot], 0, rows).astype(BF16)

        def ffn(wg_ref, wu_ref, wd_ref):
            gate = jnp.dot(xb, wg_ref[...], preferred_element_type=F32)
            up = jnp.dot(xb, wu_ref[...], preferred_element_type=F32)
            hid = (gate * _sigmoid(gate) * up).astype(BF16)
            return jnp.dot(hid, wd_ref[...], preferred_element_type=F32)

        _to_row_tiles(o_ref.at[0], ffn(wgl_ref, wul_ref, wdl_ref))
        _to_row_tiles(o_ref.at[1], ffn(wgh_ref, wuh_ref, wdh_ref))

    @pl.when(j >= used)
    def _():
        o_ref[...] = jnp.zeros_like(o_ref)


def _experts(block_lo, block_hi, n_used, buf_tok3, h2, wg, wu, wd):
    nb, _, tmb = buf_tok3.shape
    d = wg.shape[1]
    de = wg.shape[2]
    lo_map = lambda j, bl, bh, nu: (bl[jnp.minimum(j, nu[0] - 1)], 0, 0)
    hi_map = lambda j, bl, bh, nu: (bh[jnp.minimum(j, nu[0] - 1)], 0, 0)
    return pl.pallas_call(
        _expert_kernel,
        grid_spec=pltpu.PrefetchScalarGridSpec(
            num_scalar_prefetch=3,
            grid=(nb,),
            in_specs=[
                pl.BlockSpec((1, 1, tmb), lambda j, bl, bh, nu: (j, 0, 0), memory_space=pltpu.SMEM),
                pl.BlockSpec((1, 1, tmb), lambda j, bl, bh, nu: (jnp.minimum(j + 1, nb - 1), 0, 0),
                             memory_space=pltpu.SMEM),
                pl.BlockSpec(memory_space=pl.ANY),
                pl.BlockSpec((None, d, de), lo_map),
                pl.BlockSpec((None, d, de), lo_map),
                pl.BlockSpec((None, de, d), lo_map),
                pl.BlockSpec((None, d, de), hi_map),
                pl.BlockSpec((None, d, de), hi_map),
                pl.BlockSpec((None, de, d), hi_map),
            ],
            out_specs=pl.BlockSpec((None, 2, tmb * ROW_TILES, LANES), lambda j, bl, bh, nu: (j, 0, 0, 0)),
            scratch_shapes=[pltpu.VMEM((2, tmb * ROW_TILES, LANES), F32), pltpu.SemaphoreType.DMA((2,))],
        ),
        out_shape=jax.ShapeDtypeStruct((nb, 2, tmb * ROW_TILES, LANES), F32),
        compiler_params=_params("arbitrary"),
        name="experts",
    )(block_lo, block_hi, n_used, buf_tok3, buf_tok3, h2, wg, wu, wd, wg, wu, wd)


def _issue_pairs(idx_ref, yb_hbm, dst, sem, rows):
    tmb = yb_hbm.shape[2] // ROW_TILES
    assert tmb & (tmb - 1) == 0

    def body(r, c):
        pos = idx_ref[0, 0, r]
        blk = lax.shift_right_logical(pos, tmb.bit_length() - 1)
        row = pl.multiple_of((pos & (tmb - 1)) * ROW_TILES, ROW_TILES)
        pltpu.make_async_copy(yb_hbm.at[blk, :, pl.ds(row, ROW_TILES), :],
                              dst.at[:, pl.ds(pl.multiple_of(r * ROW_TILES, ROW_TILES), ROW_TILES), :],
                              sem).start()
        return c
    lax.fori_loop(0, rows, body, 0, unroll=8)


def _final_kernel(idx_ref, idxn_ref, x1_ref, rw_ref, p_ref, yb_hbm, npl_ref, wpg_ref, wp_ref, nfin_ref,
                  o_ref, gbuf, sem):
    i = pl.program_id(0)
    nt = pl.num_programs(0)
    tm = x1_ref.shape[0]
    slot = i % 2

    @pl.when(i == 0)
    def _():
        _issue_pairs(idx_ref, yb_hbm, gbuf.at[0], sem.at[0], tm)

    @pl.when(i + 1 < nt)
    def _():
        _issue_pairs(idxn_ref, yb_hbm, gbuf.at[1 - slot], sem.at[1 - slot], tm)

    pltpu.make_async_copy(gbuf.at[slot], gbuf.at[slot], sem.at[slot]).wait()
    rw = rw_ref[...]
    x2 = (x1_ref[...] + rw[:, 0:1] * _from_row_tiles(gbuf.at[slot, 0], 0, tm)
          + rw[:, 1:2] * _from_row_tiles(gbuf.at[slot, 1], 0, tm))
    hp = _rms(x2, npl_ref[...]).astype(BF16)
    gate = _sigmoid(jnp.dot(hp, wpg_ref[...], preferred_element_type=F32))
    emb = jnp.dot(p_ref[...].astype(BF16), wp_ref[...], preferred_element_type=F32)
    x3 = x2 + gate * emb
    o_ref[...] = _rms(x3, nfin_ref[...])


def _final(dest3, x1, rw, p2, yb, npl, wpg, wp, nfin):
    n, d = x1.shape
    tm = FINAL_ROWS
    nt = n // tm
    row = lambda i: (i, 0)
    const = lambda i: (0, 0)
    full = lambda a: pl.BlockSpec(a.shape, const)
    return pl.pallas_call(
        _final_kernel,
        grid=(nt,),
        in_specs=[
            pl.BlockSpec((1, 1, tm), lambda i: (i, 0, 0), memory_space=pltpu.SMEM),
            pl.BlockSpec((1, 1, tm), lambda i: (jnp.minimum(i + 1, nt - 1), 0, 0), memory_space=pltpu.SMEM),
            pl.BlockSpec((tm, d), row),
            pl.BlockSpec((tm, ROUTE_COLS), row),
            pl.BlockSpec((tm, p2.shape[1]), row),
            pl.BlockSpec(memory_space=pl.ANY),
            full(npl), full(wpg), full(wp), full(nfin),
        ],
        out_specs=pl.BlockSpec((tm, d), row),
        out_shape=jax.ShapeDtypeStruct((n, d), F32),
        scratch_shapes=[pltpu.VMEM((2, 2, tm * ROW_TILES, LANES), F32), pltpu.SemaphoreType.DMA((2,))],
        compiler_params=_params("arbitrary"),
        name="final",
    )(dest3, dest3, x1, rw, p2, yb, npl, wpg, wp, nfin)


def _layer(x2, p2, batch, seq, cosf, sinf, norm_mix, w_in, ret_gn, w_br_ret, w_br_sb, w_out, norm_ffn,
           w_grp, b_grp, w_exp, b_exp, w_gate, w_up, w_down, norm_ple, w_ple, w_ple_gate, out_gain):
    n, d = x2.shape
    wv_lo = 2 * 512 + 2 * 512 + 2 * 512
    rq, rk, rv, rg, sq, skp, svt, g_ret, g_sb = _proj(
        x2, norm_mix[None, :], w_in.astype(BF16), w_in[:, wv_lo:wv_lo + 512].T.astype(BF16), cosf, sinf, seq)
    ret = _retention(rq, rk, rv, rg, ret_gn[None, :], batch, seq)

    tk = SB_BLOCK
    nkb = seq // tk
    kperm = skp.reshape(batch, nkb, tk, skp.shape[1])
    vtperm = svt.reshape(batch, nkb, svt.shape[1], tk)
    sb = _stick_breaking(sq, kperm, vtperm, batch, seq)

    wr = jnp.zeros((d, ROUTE_COLS), F32)
    wr = wr.at[:, 0:N_GROUPS].set(w_grp).at[:, EXPERT_COL0:EXPERT_COL0 + N_EXPERTS].set(w_exp)
    br = jnp.zeros((1, ROUTE_COLS), F32)
    br = br.at[0, 0:N_GROUPS].set(b_grp).at[0, EXPERT_COL0:EXPERT_COL0 + N_EXPERTS].set(b_exp)
    x1, h2, ri, rw, cnt = _merge(ret, sb, g_ret, g_sb, x2, w_br_ret.astype(BF16), w_br_sb.astype(BF16),
                                 w_out.astype(BF16), norm_ffn[None, :], wr.astype(BF16), br)

    tmb = EXPERT_ROWS
    counts = cnt[:N_CLASSES, 0].astype(I32)
    padded = (counts + tmb - 1) // tmb * tmb
    pad_end = jnp.cumsum(padded)
    pad_start = pad_end - padded
    cids = jnp.arange(N_CLASSES, dtype=I32)
    dest = jnp.sum(jnp.where(ri[0, :, None] == cids, pad_start, 0), axis=-1) + ri[1]
    nb = n // tmb + N_CLASSES
    buf_tok = jnp.zeros((nb * tmb,), I32).at[dest].set(jnp.arange(n, dtype=I32))
    block_c = jnp.minimum(
        jnp.sum((pad_end[None, :] <= (jnp.arange(nb, dtype=I32) * tmb)[:, None]).astype(I32), axis=1),
        N_CLASSES - 1)
    pairs = [(lo, hi) for lo in range(EXPERTS_PER_GROUP) for hi in range(lo + 1, EXPERTS_PER_GROUP)]
    class_lo = np.array([g * EXPERTS_PER_GROUP + lo for g in range(N_GROUPS) for lo, _ in pairs], np.int32)
    class_hi = np.array([g * EXPERTS_PER_GROUP + hi for g in range(N_GROUPS) for _, hi in pairs], np.int32)
    block_lo = jnp.sum(jnp.where(block_c[:, None] == cids, class_lo, 0), axis=-1).astype(I32)
    block_hi = jnp.sum(jnp.where(block_c[:, None] == cids, class_hi, 0), axis=-1).astype(I32)
    n_used = (pad_end[-1:] // tmb).astype(I32)
    yb = _experts(block_lo, block_hi, n_used, buf_tok.reshape(nb, 1, tmb), h2,
                  w_gate.astype(BF16), w_up.astype(BF16), w_down.astype(BF16))

    tf = FINAL_ROWS
    return _final(dest.reshape(n // tf, 1, tf), x1, rw, p2, yb, norm_ple[None, :],
                  w_ple_gate.astype(BF16), w_ple.astype(BF16), out_gain)


def kernel(x, p, norm_mix, w_in, ret_gn, w_br_ret, w_br_sb, w_out, norm_ffn, w_grp, b_grp, w_exp, b_exp,
           w_gate, w_up, w_down, norm_ple, w_ple, w_ple_gate, norm_final):
    batch, seq, d = x.shape
    depth = p.shape[0]
    assert depth == 1, "the final RMSNorm is fused into the last layer's kernel"
    pos = jnp.arange(seq, dtype=F32)
    inv_freq = ROPE_BASE ** (-jnp.arange(0, RET_DK, 2, dtype=F32) / RET_DK)
    ang = pos[:, None] * inv_freq[None, :]
    cos, sin = jnp.cos(ang), jnp.sin(ang)
    cosf = jnp.concatenate([cos, cos], axis=1)
    sinf = jnp.concatenate([-sin, sin], axis=1)
    x2 = x.reshape(batch * seq, d)
    i = 0
    out = _layer(x2, p[i].reshape(batch * seq, -1), batch, seq, cosf, sinf, norm_mix[i], w_in[i], ret_gn[i],
                 w_br_ret[i], w_br_sb[i], w_out[i], norm_ffn[i], w_grp[i], b_grp[i], w_exp[i], b_exp[i],
                 w_gate[i], w_up[i], w_down[i], norm_ple[i], w_ple[i], w_ple_gate[i], norm_final[None, :])
    return out.reshape(batch, seq, d)
```

```python
import functools

import numpy as np
import jax
import jax.numpy as jnp
from jax import lax
from jax.experimental import pallas as pl
from jax.experimental.pallas import tpu as pltpu

F32 = jnp.float32
BF16 = jnp.bfloat16
I32 = jnp.int32

EPS = 1e-6
CHUNK = 64
RET_HEADS = 4
RET_DK = 128
SB_HEADS = 8
SB_DH = 64
ROPE_BASE = 10000.0
N_GROUPS = 4
EXPERTS_PER_GROUP = 4
N_EXPERTS = N_GROUPS * EXPERTS_PER_GROUP
PAIRS_PER_GROUP = EXPERTS_PER_GROUP * (EXPERTS_PER_GROUP - 1) // 2
N_CLASSES = N_GROUPS * PAIRS_PER_GROUP
CLASS_ROWS = 32

LANES = 128
SUBLANES = 8
VMEM_LIMIT = 56 * 1024 * 1024

PROJ_ROWS = 512
RET_ROWS = 512
SB_BLOCK = 256
MERGE_ROWS = 512
EXPERT_ROWS = 256
FINAL_ROWS = 512
ROUTE_COLS = 128
ROW_TILES = 8
EXPERT_COL0 = 8
LOG2E = 1.4426950408889634
SB_SKIP_LOG2 = -160.0


def _rms(x, g):
    return x * lax.rsqrt(jnp.mean(x * x, axis=-1, keepdims=True) + EPS) * g


def _sigmoid(x):
    return 1.0 / (1.0 + jnp.exp(-x))


def _to_row_tiles(ref, val):
    rows = val.shape[0]
    for s in range(ROW_TILES):
        ref[pl.ds(s, rows, stride=ROW_TILES), :] = val[:, s * LANES:(s + 1) * LANES]


def _from_row_tiles(ref, first, rows):
    return jnp.concatenate([ref[pl.ds(first * ROW_TILES + s, rows, stride=ROW_TILES), :]
                            for s in range(ROW_TILES)], axis=1)


def _params(*sem):
    return pltpu.CompilerParams(dimension_semantics=sem, vmem_limit_bytes=VMEM_LIMIT)


def _proj_kernel(x_ref, g_ref, w_ref, wvt_ref, perm_ref, cos_ref, sin_ref,
                 rq_ref, rk_ref, rv_ref, rg_ref, sq_ref, sk_ref, svt_ref, gr_ref, gs_ref):
    h = _rms(x_ref[...], g_ref[...]).astype(BF16)
    hp = jnp.dot(perm_ref[...], h, preferred_element_type=F32).astype(BF16)
    cos = cos_ref[...]
    sin = sin_ref[...]

    def seg(lo, hi):
        return jnp.dot(h, w_ref[:, lo:hi], preferred_element_type=F32)

    def rope(t):
        outs = []
        for hd in range(RET_HEADS):
            th = t[:, hd * RET_DK:(hd + 1) * RET_DK]
            outs.append(th * cos + pltpu.roll(th, RET_DK // 2, 1) * sin)
        return jnp.concatenate(outs, axis=1)

    rq_ref[...] = rope(seg(0, 512)).astype(BF16)
    rk_ref[...] = (rope(seg(512, 1024)) * (RET_DK ** -0.5)).astype(BF16)
    rv_ref[...] = seg(1024, 1536).astype(BF16)
    rg_ref[...] = seg(1536, 2048).astype(BF16)
    sq_ref[...] = (seg(2048, 2560) * (SB_DH ** -0.5)).astype(BF16)
    sk_ref[...] = jnp.dot(hp, w_ref[:, 2560:3072], preferred_element_type=F32).astype(BF16)
    svt = lax.dot_general(wvt_ref[...], hp, (((1,), (1,)), ((), ())), preferred_element_type=F32)
    tk = svt_ref.shape[2]
    for kb in range(svt_ref.shape[0]):
        svt_ref[kb] = svt[:, kb * tk:(kb + 1) * tk].astype(BF16)
    gr_ref[...] = seg(3584, 4608).astype(BF16)
    gs_ref[...] = seg(4608, 5632).astype(BF16)


def _key_block_perm(tm, tk):
    grp = tk // SUBLANES
    new = np.arange(tm)
    blk, r = new // tk, new % tk
    old = blk * tk + (r % SUBLANES) * grp + r // SUBLANES
    p = np.zeros((tm, tm), np.float32)
    p[new, old] = 1.0
    return jnp.asarray(p, BF16)


def _proj(x2, g, w, wvt, cosf, sinf, seq):
    n, d = x2.shape
    tm = PROJ_ROWS
    tk = SB_BLOCK
    pos_blocks = seq // tm
    perm = _key_block_perm(tm, tk)
    row = lambda i: (i, 0)
    const = lambda i: (0, 0)
    half = jax.ShapeDtypeStruct((n, 512), BF16)
    full = jax.ShapeDtypeStruct((n, 1024), BF16)
    svt = jax.ShapeDtypeStruct((n // tk, 512, tk), BF16)
    hspec = pl.BlockSpec((tm, 512), row)
    fspec = pl.BlockSpec((tm, 1024), row)
    return pl.pallas_call(
        _proj_kernel,
        grid=(n // tm,),
        in_specs=[
            pl.BlockSpec((tm, d), row),
            pl.BlockSpec((1, d), const),
            pl.BlockSpec(w.shape, const),
            pl.BlockSpec(wvt.shape, const),
            pl.BlockSpec(perm.shape, const),
            pl.BlockSpec((tm, RET_DK), lambda i: (i % pos_blocks, 0)),
            pl.BlockSpec((tm, RET_DK), lambda i: (i % pos_blocks, 0)),
        ],
        out_specs=[hspec] * 6 + [pl.BlockSpec((tm // tk, 512, tk), lambda i: (i, 0, 0)), fspec, fspec],
        out_shape=[half] * 6 + [svt, full, full],
        compiler_params=_params("parallel"),
        name="proj",
    )(x2, g, w, wvt, perm, cosf, sinf)


def _ret_consts(tb):
    hd = np.arange(RET_HEADS, dtype=np.float64)
    lg = np.log(1.0 - np.exp2(-5.0 - hd))
    idx = np.arange(tb, dtype=np.float64)
    dist = np.abs(idx[:, None] - idx[None, :])
    chunk = (np.arange(tb) // CHUNK)
    allowed = chunk[None, :] <= chunk[:, None]
    dm = np.exp(lg[:, None, None] * dist) * allowed
    qdec = np.exp(lg[:, None] * (idx + 1.0))[:, :, None] * np.ones((1, 1, RET_DK))
    kdec = np.exp(lg[:, None] * (tb - 1.0 - idx))[:, :, None] * np.ones((1, 1, RET_DK))
    cdec = [float(v) for v in np.exp(lg * tb)]
    return (jnp.asarray(dm, F32), jnp.asarray(qdec, F32), jnp.asarray(kdec, F32), cdec)


def _ret_kernel(cdec, q_ref, k_ref, v_ref, g_ref, dm_ref, qd_ref, kd_ref, gn_ref, o_ref, st_ref):
    @pl.when(pl.program_id(1) == 0)
    def _():
        st_ref[...] = jnp.zeros_like(st_ref)

    for hd in range(RET_HEADS):
        sl = slice(hd * RET_DK, (hd + 1) * RET_DK)
        q = q_ref[:, sl]
        k = k_ref[:, sl]
        v = v_ref[:, sl]
        st = st_ref[hd]
        scores = lax.dot_general(q, k, (((1,), (1,)), ((), ())), preferred_element_type=F32)
        scores = scores * dm_ref[hd]
        qd = (q.astype(F32) * qd_ref[hd]).astype(BF16)
        o = (jnp.dot(scores.astype(BF16), v, preferred_element_type=F32)
             + jnp.dot(qd, st.astype(BF16), preferred_element_type=F32))
        kd = (k.astype(F32) * kd_ref[hd]).astype(BF16)
        st_ref[hd] = cdec[hd] * st + lax.dot_general(
            kd, v, (((0,), (0,)), ((), ())), preferred_element_type=F32)
        mu = jnp.mean(o, axis=-1, keepdims=True)
        oc = o - mu
        var = jnp.mean(oc * oc, axis=-1, keepdims=True)
        on = oc * lax.rsqrt(var + EPS) * gn_ref[:, sl]
        gate = g_ref[:, sl].astype(F32)
        o_ref[:, sl] = (gate * _sigmoid(gate) * on).astype(BF16)


def _retention(rq, rk, rv, rg, gn, batch, seq):
    n, w = rq.shape
    tb = RET_ROWS
    nc = seq // tb
    dm, qdec, kdec, cdec = _ret_consts(tb)
    blk = pl.BlockSpec((tb, w), lambda b, c: (b * nc + c, 0))
    c3 = lambda b, c: (0, 0, 0)
    return pl.pallas_call(
        functools.partial(_ret_kernel, cdec),
        grid=(batch, nc),
        in_specs=[blk, blk, blk, blk,
                  pl.BlockSpec(dm.shape, c3), pl.BlockSpec(qdec.shape, c3), pl.BlockSpec(kdec.shape, c3),
                  pl.BlockSpec((1, w), lambda b, c: (0, 0))],
        out_specs=blk,
        out_shape=jax.ShapeDtypeStruct((n, w), BF16),
        scratch_shapes=[pltpu.VMEM((RET_HEADS, RET_DK, RET_DK), F32)],
        compiler_params=_params("parallel", "arbitrary"),
        name="retention",
    )(rq, rk, rv, rg, dm, qdec, kdec, gn)


def _sb_kernel(q_ref, k_ref, vt_ref, o_ref, acc_ref):
    tq = q_ref.shape[0]
    tk = tq
    grp = tk // SUBLANES
    heads = q_ref.shape[1] // SB_DH
    wide = heads * tq
    assert tq & (tq - 1) == 0
    qi = pl.program_id(1)
    q = q_ref[...]
    lane = lax.broadcasted_iota(I32, q.shape, 1)
    q_all = jnp.concatenate(
        [jnp.where((lane >= hh * SB_DH) & (lane < (hh + 1) * SB_DH), q, jnp.zeros_like(q))
         for hh in range(heads)], axis=0)
    shp3 = (grp, SUBLANES, wide)
    key_pos = lax.broadcasted_iota(I32, shp3, 1) * grp + lax.broadcasted_iota(I32, shp3, 0)
    causal = key_pos < (lax.broadcasted_iota(I32, shp3, 2) & (tq - 1))
    row8 = lax.broadcasted_iota(I32, (SUBLANES, wide), 0)

    def block(kb, carry, diagonal):
        z = lax.dot_general(k_ref[kb], q_all, (((1,), (1,)), ((), ())),
                            preferred_element_type=F32) * LOG2E
        softplus = jnp.maximum(z, 0.0) + jnp.log2(1.0 + jnp.exp2(-jnp.abs(z)))
        z3 = z.reshape(shp3)
        sp3 = softplus.reshape(shp3)
        if diagonal:
            sp3 = jnp.where(causal, sp3, 0.0)
        run = jnp.zeros((SUBLANES, wide), F32)
        parts = [None] * grp
        for g in reversed(range(grp)):
            run = run - sp3[g]
            parts[g] = run
        tot = parts[0]
        inc = tot
        for d in (1, 2, 4):
            inc = inc + jnp.where(row8 + d < SUBLANES, pltpu.roll(inc, SUBLANES - d, 0), 0.0)
        base = inc - tot + carry
        a3 = jnp.exp2(z3 + jnp.stack(parts, axis=0) + base[None])
        if diagonal:
            a3 = jnp.where(causal, a3, 0.0)
        a = a3.reshape(tk, wide).astype(BF16)
        for hh in range(heads):
            rows = slice(hh * SB_DH, (hh + 1) * SB_DH)
            contrib = jnp.dot(vt_ref[kb, rows, :], a[:, hh * tq:(hh + 1) * tq], preferred_element_type=F32)
            if diagonal:
                acc_ref[rows, :] = contrib
            else:
                acc_ref[rows, :] += contrib
        return carry + jnp.broadcast_to(inc[0:1, :], carry.shape)

    carry = block(qi, jnp.zeros((SUBLANES, wide), F32), True)

    def cond(c):
        return (c[0] < qi) & (c[1] > 0)

    def body(c):
        new = block(qi - 1 - c[0], c[2], False)
        go = (jnp.max(new) > SB_SKIP_LOG2).astype(I32)
        return (c[0] + 1, go, new)

    lax.while_loop(cond, body, (jnp.int32(0), jnp.int32(1), carry))
    o_ref[...] = acc_ref[...].T.astype(BF16)


def _stick_breaking(sq, kperm, vtperm, batch, seq):
    n, w = sq.shape
    tq = SB_BLOCK
    nq = seq // tq
    return pl.pallas_call(
        _sb_kernel,
        grid=(batch, nq),
        in_specs=[
            pl.BlockSpec((tq, w), lambda b, i: (b * nq + i, 0)),
            pl.BlockSpec((None, nq, tq, w), lambda b, i: (b, 0, 0, 0)),
            pl.BlockSpec((None, nq, w, tq), lambda b, i: (b, 0, 0, 0)),
        ],
        out_specs=pl.BlockSpec((tq, w), lambda b, i: (b * nq + i, 0)),
        out_shape=jax.ShapeDtypeStruct((n, w), BF16),
        scratch_shapes=[pltpu.VMEM((w, tq), F32)],
        compiler_params=_params("parallel", "arbitrary"),
        name="stickbrk",
    )(sq, kperm, vtperm)


def _merge_kernel(ret_ref, sb_ref, gr_ref, gs_ref, x_ref, wbr_ref, wbs_ref, wo_ref, nf_ref,
                  wr_ref, br_ref, tri_ref,
                  x1_ref, h2_ref, ri_ref, rw_ref, cnt_ref, carry_ref):
    @pl.when(pl.program_id(0) == 0)
    def _():
        carry_ref[...] = jnp.zeros_like(carry_ref)

    tm = x_ref.shape[0]
    merged = (_sigmoid(gr_ref[...].astype(F32)) * jnp.dot(ret_ref[...], wbr_ref[...], preferred_element_type=F32)
              + _sigmoid(gs_ref[...].astype(F32)) * jnp.dot(sb_ref[...], wbs_ref[...], preferred_element_type=F32))
    x1 = x_ref[...] + jnp.dot(merged.astype(BF16), wo_ref[...], preferred_element_type=F32)
    x1_ref[...] = x1
    h2 = _rms(x1, nf_ref[...])
    _to_row_tiles(h2_ref, h2)
    logits = jnp.dot(h2.astype(BF16), wr_ref[...], preferred_element_type=F32) + br_ref[...]
    lt = logits.T
    neg = jnp.float32(-jnp.inf)

    grow = lax.broadcasted_iota(I32, (SUBLANES, tm), 0)
    gl = jnp.where(grow < N_GROUPS, lt[0:SUBLANES], neg)
    gmax = jnp.max(gl, axis=0, keepdims=True)
    g_idx = jnp.min(jnp.where(gl == gmax, grow, N_GROUPS), axis=0, keepdims=True)
    p_grp = 1.0 / jnp.sum(jnp.exp(gl - gmax), axis=0, keepdims=True)

    erow = lax.broadcasted_iota(I32, (N_EXPERTS, tm), 0)
    el = lt[EXPERT_COL0:EXPERT_COL0 + N_EXPERTS]
    el = jnp.where((erow >= g_idx * EXPERTS_PER_GROUP) & (erow < (g_idx + 1) * EXPERTS_PER_GROUP), el, neg)
    m1 = jnp.max(el, axis=0, keepdims=True)
    i1 = jnp.min(jnp.where(el == m1, erow, N_EXPERTS), axis=0, keepdims=True)
    el2 = jnp.where(erow == i1, neg, el)
    m2 = jnp.max(el2, axis=0, keepdims=True)
    i2 = jnp.min(jnp.where(el2 == m2, erow, N_EXPERTS), axis=0, keepdims=True)
    e2 = jnp.exp(m2 - m1)
    w1 = p_grp / (1.0 + e2)
    w2 = p_grp * e2 / (1.0 + e2)

    l1 = i1 - g_idx * EXPERTS_PER_GROUP
    l2 = i2 - g_idx * EXPERTS_PER_GROUP
    lo = jnp.minimum(l1, l2)
    hi = jnp.maximum(l1, l2)
    cls = g_idx * PAIRS_PER_GROUP + ((lo * (2 * EXPERTS_PER_GROUP - 1 - lo)) >> 1) + (hi - lo - 1)
    w_lo = jnp.where(l1 < l2, w1, w2)
    w_hi = jnp.where(l1 < l2, w2, w1)

    crow = lax.broadcasted_iota(I32, (CLASS_ROWS, tm), 0)
    sel = crow == cls
    onehot = sel.astype(BF16)
    cinc = jnp.dot(onehot, tri_ref[...], preferred_element_type=F32)
    carry = carry_ref[...]
    rank = jnp.sum(jnp.where(sel, carry[:, 0:1] + cinc - 1.0, 0.0), axis=0, keepdims=True)
    carry_new = carry + jnp.sum(onehot.astype(F32), axis=1, keepdims=True)
    carry_ref[...] = carry_new
    cnt_ref[...] = carry_new

    ri_ref[...] = jnp.zeros_like(ri_ref)
    ri_ref[0:1, :] = cls
    ri_ref[1:2, :] = rank.astype(I32)
    wrow = lax.broadcasted_iota(I32, (ROUTE_COLS, tm), 0)
    wt = jnp.where(wrow == 0, w_lo, jnp.where(wrow == 1, w_hi, 0.0))
    rw_ref[...] = wt.T


def _merge(ret, sb, g_ret, g_sb, x2, wbr, wbs, wo, nf, wr, br):
    n, d = x2.shape
    tm = MERGE_ROWS
    tri = jnp.asarray(np.triu(np.ones((tm, tm), np.float32)), BF16)
    row = lambda i: (i, 0)
    const = lambda i: (0, 0)
    full = lambda a: pl.BlockSpec(a.shape, const)
    return pl.pallas_call(
        _merge_kernel,
        grid=(n // tm,),
        in_specs=[pl.BlockSpec((tm, 512), row), pl.BlockSpec((tm, 512), row),
                  pl.BlockSpec((tm, d), row), pl.BlockSpec((tm, d), row), pl.BlockSpec((tm, d), row),
                  full(wbr), full(wbs), full(wo), full(nf), full(wr), full(br), full(tri)],
        out_specs=[pl.BlockSpec((tm, d), row), pl.BlockSpec((tm * ROW_TILES, LANES), row),
                   pl.BlockSpec((SUBLANES, tm), lambda i: (0, i)),
                   pl.BlockSpec((tm, ROUTE_COLS), row),
                   pl.BlockSpec((CLASS_ROWS, LANES), const)],
        out_shape=[jax.ShapeDtypeStruct((n, d), F32), jax.ShapeDtypeStruct((n * ROW_TILES, LANES), F32),
                   jax.ShapeDtypeStruct((SUBLANES, n), I32),
                   jax.ShapeDtypeStruct((n, ROUTE_COLS), F32),
                   jax.ShapeDtypeStruct((CLASS_ROWS, LANES), F32)],
        scratch_shapes=[pltpu.VMEM((CLASS_ROWS, LANES), F32)],
        compiler_params=_params("arbitrary"),
        name="merge",
    )(ret, sb, g_ret, g_sb, x2, wbr, wbs, wo, nf, wr, br, tri)


DMA_QUEUES = 2


def _tile_rows(i):
    return pl.ds(pl.multiple_of(i * ROW_TILES, ROW_TILES), ROW_TILES)


def _issue_rows(idx_ref, src_hbm, dst, sem, rows):
    def body(i, c):
        for k in range(DMA_QUEUES):
            r = i * DMA_QUEUES + k
            pltpu.make_async_copy(src_hbm.at[_tile_rows(idx_ref[0, 0, r])], dst.at[_tile_rows(r)],
                                  sem).start(priority=k)
        return c
    lax.fori_loop(0, rows // DMA_QUEUES, body, 0, unroll=4)


def _expert_kernel(bl_ref, bh_ref, nu_ref, idx_ref, idxn_ref, x_hbm,
                   wgl_ref, wul_ref, wdl_ref, wgh_ref, wuh_ref, wdh_ref, o_ref, xbuf, sem):
    j = pl.program_id(0)
    used = nu_ref[0]
    rows = xbuf.shape[1] // ROW_TILES
    slot = j % 2

    @pl.when(j == 0)
    def _():
        _issue_rows(idx_ref, x_hbm, xbuf.at[0], sem.at[0], rows)

    @pl.when(j + 1 < used)
    def _():
        _issue_rows(idxn_ref, x_hbm, xbuf.at[1 - slot], sem.at[1 - slot], rows)

    @pl.when(j < used)
    def _():
        pltpu.make_async_copy(xbuf.at[slot], xbuf.at[slot], sem.at[slot]).wait()
        xb = _from_row_tiles(xbuf.at[slot], 0, rows).astype(BF16)

        def ffn(wg_ref, wu_ref, wd_ref):
            gate = jnp.dot(xb, wg_ref[...], preferred_element_type=F32)
            up = jnp.dot(xb, wu_ref[...], preferred_element_type=F32)
            hid = (gate * _sigmoid(gate) * up).astype(BF16)
            return jnp.dot(hid, wd_ref[...], preferred_element_type=F32)

        _to_row_tiles(o_ref.at[0], ffn(wgl_ref, wul_ref, wdl_ref))
        _to_row_tiles(o_ref.at[1], ffn(wgh_ref, wuh_ref, wdh_ref))

    @pl.when(j >= used)
    def _():
        o_ref[...] = jnp.zeros_like(o_ref)


def _experts(block_lo, block_hi, n_used, buf_tok3, h2, wg, wu, wd):
    nb, _, tmb = buf_tok3.shape
    d = wg.shape[1]
    de = wg.shape[2]
    lo_map = lambda j, bl, bh, nu: (bl[jnp.minimum(j, nu[0] - 1)], 0, 0)
    hi_map = lambda j, bl, bh, nu: (bh[jnp.minimum(j, nu[0] - 1)], 0, 0)
    return pl.pallas_call(
        _expert_kernel,
        grid_spec=pltpu.PrefetchScalarGridSpec(
            num_scalar_prefetch=3,
            grid=(nb,),
            in_specs=[
                pl.BlockSpec((1, 1, tmb), lambda j, bl, bh, nu: (j, 0, 0), memory_space=pltpu.SMEM),
                pl.BlockSpec((1, 1, tmb), lambda j, bl, bh, nu: (jnp.minimum(j + 1, nb - 1), 0, 0),
                             memory_space=pltpu.SMEM),
                pl.BlockSpec(memory_space=pl.ANY),
                pl.BlockSpec((None, d, de), lo_map),
                pl.BlockSpec((None, d, de), lo_map),
                pl.BlockSpec((None, de, d), lo_map),
                pl.BlockSpec((None, d, de), hi_map),
                pl.BlockSpec((None, d, de), hi_map),
                pl.BlockSpec((None, de, d), hi_map),
            ],
            out_specs=pl.BlockSpec((2, tmb * ROW_TILES, LANES), lambda j, bl, bh, nu: (0, j, 0)),
            scratch_shapes=[pltpu.VMEM((2, tmb * ROW_TILES, LANES), F32), pltpu.SemaphoreType.DMA((2,))],
        ),
        out_shape=jax.ShapeDtypeStruct((2, nb * tmb * ROW_TILES, LANES), F32),
        compiler_params=_params("arbitrary"),
        name="experts",
    )(block_lo, block_hi, n_used, buf_tok3, buf_tok3, h2, wg, wu, wd, wg, wu, wd)


def _issue_pairs(idx_ref, yb_hbm, dst, sem, rows):
    def body(i, c):
        for k in range(DMA_QUEUES):
            r = i * DMA_QUEUES + k
            pltpu.make_async_copy(yb_hbm.at[:, _tile_rows(idx_ref[0, 0, r]), :], dst.at[:, _tile_rows(r), :],
                                  sem).start(priority=k)
        return c
    lax.fori_loop(0, rows // DMA_QUEUES, body, 0, unroll=4)


def _final_kernel(idx_ref, idxn_ref, x1_ref, rw_ref, p_ref, yb_hbm, npl_ref, wpg_ref, wp_ref, nfin_ref,
                  o_ref, gbuf, sem, x2_ref):
    i = pl.program_id(0)
    nt = pl.num_programs(0)
    tm = x1_ref.shape[0]
    slot = i % 2

    @pl.when(i == 0)
    def _():
        _issue_pairs(idx_ref, yb_hbm, gbuf.at[0], sem.at[0], tm)

    @pl.when(i + 1 < nt)
    def _():
        _issue_pairs(idxn_ref, yb_hbm, gbuf.at[1 - slot], sem.at[1 - slot], tm)

    pltpu.make_async_copy(gbuf.at[slot], gbuf.at[slot], sem.at[slot]).wait()
    w_lo = jnp.broadcast_to(rw_ref[:, 0:1], (tm, LANES))
    w_hi = jnp.broadcast_to(rw_ref[:, 1:2], (tm, LANES))
    for s in range(ROW_TILES):
        cols = slice(s * LANES, (s + 1) * LANES)
        x2_ref[:, cols] = (x1_ref[:, cols]
                           + w_lo * gbuf[slot, 0, pl.ds(s, tm, stride=ROW_TILES), :]
                           + w_hi * gbuf[slot, 1, pl.ds(s, tm, stride=ROW_TILES), :])
    x2 = x2_ref[...]
    hp = _rms(x2, npl_ref[...]).astype(BF16)
    gate = _sigmoid(jnp.dot(hp, wpg_ref[...], preferred_element_type=F32))
    emb = jnp.dot(p_ref[...].astype(BF16), wp_ref[...], preferred_element_type=F32)
    x3 = x2 + gate * emb
    o_ref[...] = _rms(x3, nfin_ref[...])


def _final(dest3, x1, rw, p2, yb, npl, wpg, wp, nfin):
    n, d = x1.shape
    tm = FINAL_ROWS
    nt = n // tm
    row = lambda i: (i, 0)
    const = lambda i: (0, 0)
    full = lambda a: pl.BlockSpec(a.shape, const)
    return pl.pallas_call(
        _final_kernel,
        grid=(nt,),
        in_specs=[
            pl.BlockSpec((1, 1, tm), lambda i: (i, 0, 0), memory_space=pltpu.SMEM),
            pl.BlockSpec((1, 1, tm), lambda i: (jnp.minimum(i + 1, nt - 1), 0, 0), memory_space=pltpu.SMEM),
            pl.BlockSpec((tm, d), row),
            pl.BlockSpec((tm, ROUTE_COLS), row),
            pl.BlockSpec((tm, p2.shape[1]), row),
            pl.BlockSpec(memory_space=pl.ANY),
            full(npl), full(wpg), full(wp), full(nfin),
        ],
        out_specs=pl.BlockSpec((tm, d), row),
        out_shape=jax.ShapeDtypeStruct((n, d), F32),
        scratch_shapes=[pltpu.VMEM((2, 2, tm * ROW_TILES, LANES), F32), pltpu.SemaphoreType.DMA((2,)),
                        pltpu.VMEM((tm, d), F32)],
        compiler_params=_params("arbitrary"),
        name="final",
    )(dest3, dest3, x1, rw, p2, yb, npl, wpg, wp, nfin)


def _layer(x2, p2, batch, seq, cosf, sinf, norm_mix, w_in, ret_gn, w_br_ret, w_br_sb, w_out, norm_ffn,
           w_grp, b_grp, w_exp, b_exp, w_gate, w_up, w_down, norm_ple, w_ple, w_ple_gate, out_gain):
    n, d = x2.shape
    wv_lo = 2 * 512 + 2 * 512 + 2 * 512
    rq, rk, rv, rg, sq, skp, svt, g_ret, g_sb = _proj(
        x2, norm_mix[None, :], w_in.astype(BF16), w_in[:, wv_lo:wv_lo + 512].T.astype(BF16), cosf, sinf, seq)
    ret = _retention(rq, rk, rv, rg, ret_gn[None, :], batch, seq)

    tk = SB_BLOCK
    nkb = seq // tk
    kperm = skp.reshape(batch, nkb, tk, skp.shape[1])
    vtperm = svt.reshape(batch, nkb, svt.shape[1], tk)
    sb = _stick_breaking(sq, kperm, vtperm, batch, seq)

    wr = jnp.zeros((d, ROUTE_COLS), F32)
    wr = wr.at[:, 0:N_GROUPS].set(w_grp).at[:, EXPERT_COL0:EXPERT_COL0 + N_EXPERTS].set(w_exp)
    br = jnp.zeros((1, ROUTE_COLS), F32)
    br = br.at[0, 0:N_GROUPS].set(b_grp).at[0, EXPERT_COL0:EXPERT_COL0 + N_EXPERTS].set(b_exp)
    x1, h2, ri, rw, cnt = _merge(ret, sb, g_ret, g_sb, x2, w_br_ret.astype(BF16), w_br_sb.astype(BF16),
                                 w_out.astype(BF16), norm_ffn[None, :], wr.astype(BF16), br)

    tmb = EXPERT_ROWS
    counts = cnt[:N_CLASSES, 0].astype(I32)
    padded = (counts + tmb - 1) // tmb * tmb
    pad_end = jnp.cumsum(padded)
    pad_start = pad_end - padded
    cids = jnp.arange(N_CLASSES, dtype=I32)
    dest = jnp.sum(jnp.where(ri[0, :, None] == cids, pad_start, 0), axis=-1) + ri[1]
    nb = n // tmb + N_CLASSES
    buf_tok = jnp.zeros((nb * tmb,), I32).at[dest].set(jnp.arange(n, dtype=I32))
    block_c = jnp.minimum(
        jnp.sum((pad_end[None, :] <= (jnp.arange(nb, dtype=I32) * tmb)[:, None]).astype(I32), axis=1),
        N_CLASSES - 1)
    pairs = [(lo, hi) for lo in range(EXPERTS_PER_GROUP) for hi in range(lo + 1, EXPERTS_PER_GROUP)]
    class_lo = np.array([g * EXPERTS_PER_GROUP + lo for g in range(N_GROUPS) for lo, _ in pairs], np.int32)
    class_hi = np.array([g * EXPERTS_PER_GROUP + hi for g in range(N_GROUPS) for _, hi in pairs], np.int32)
    block_lo = jnp.sum(jnp.where(block_c[:, None] == cids, class_lo, 0), axis=-1).astype(I32)
    block_hi = jnp.sum(jnp.where(block_c[:, None] == cids, class_hi, 0), axis=-1).astype(I32)
    n_used = (pad_end[-1:] // tmb).astype(I32)
    yb = _experts(block_lo, block_hi, n_used, buf_tok.reshape(nb, 1, tmb), h2,
                  w_gate.astype(BF16), w_up.astype(BF16), w_down.astype(BF16))

    tf = FINAL_ROWS
    return _final(dest.reshape(n // tf, 1, tf), x1, rw, p2, yb, norm_ple[None, :],
                  w_ple_gate.astype(BF16), w_ple.astype(BF16), out_gain)


def kernel(x, p, norm_mix, w_in, ret_gn, w_br_ret, w_br_sb, w_out, norm_ffn, w_grp, b_grp, w_exp, b_exp,
           w_gate, w_up, w_down, norm_ple, w_ple, w_ple_gate, norm_final):
    batch, seq, d = x.shape
    depth = p.shape[0]
    assert depth == 1, "the final RMSNorm is fused into the last layer's kernel"
    pos = jnp.arange(seq, dtype=F32)
    inv_freq = ROPE_BASE ** (-jnp.arange(0, RET_DK, 2, dtype=F32) / RET_DK)
    ang = pos[:, None] * inv_freq[None, :]
    cos, sin = jnp.cos(ang), jnp.sin(ang)
    cosf = jnp.concatenate([cos, cos], axis=1)
    sinf = jnp.concatenate([-sin, sin], axis=1)
    x2 = x.reshape(batch * seq, d)
    i = 0
    out = _layer(x2, p[i].reshape(batch * seq, -1), batch, seq, cosf, sinf, norm_mix[i], w_in[i], ret_gn[i],
                 w_br_ret[i], w_br_sb[i], w_out[i], norm_ffn[i], w_grp[i], b_grp[i], w_exp[i], b_exp[i],
                 w_gate[i], w_up[i], w_down[i], norm_ple[i], w_ple[i], w_ple_gate[i], norm_final[None, :])
    return out.reshape(batch, seq, d)
```

```python
import functools

import numpy as np
import jax
import jax.numpy as jnp
from jax import lax
from jax.experimental import pallas as pl
from jax.experimental.pallas import tpu as pltpu

F32 = jnp.float32
BF16 = jnp.bfloat16
I32 = jnp.int32

EPS = 1e-6
CHUNK = 64
RET_HEADS = 4
RET_DK = 128
SB_HEADS = 8
SB_DH = 64
ROPE_BASE = 10000.0
N_GROUPS = 4
EXPERTS_PER_GROUP = 4
N_EXPERTS = N_GROUPS * EXPERTS_PER_GROUP
PAIRS_PER_GROUP = EXPERTS_PER_GROUP * (EXPERTS_PER_GROUP - 1) // 2
N_CLASSES = N_GROUPS * PAIRS_PER_GROUP
CLASS_ROWS = 32

LANES = 128
SUBLANES = 8
VMEM_LIMIT = 56 * 1024 * 1024

PROJ_ROWS = 512
RET_ROWS = 512
SB_BLOCK = 256
MERGE_ROWS = 512
EXPERT_ROWS = 512
FINAL_ROWS = 512
ROUTE_COLS = 128
ROW_TILES = 8
EXPERT_COL0 = 8
LOG2E = 1.4426950408889634
SB_SKIP_LOG2 = -160.0
SB_CLAMP = 126.0
SB_MASKED = -1e30


def _rms(x, g):
    return x * lax.rsqrt(jnp.mean(x * x, axis=-1, keepdims=True) + EPS) * g


def _sigmoid(x):
    return 1.0 / (1.0 + jnp.exp(-x))


def _to_row_tiles(ref, val):
    rows = val.shape[0]
    for s in range(ROW_TILES):
        ref[pl.ds(s, rows, stride=ROW_TILES), :] = val[:, s * LANES:(s + 1) * LANES]


def _from_row_tiles(ref, first, rows):
    return jnp.concatenate([ref[pl.ds(first * ROW_TILES + s, rows, stride=ROW_TILES), :]
                            for s in range(ROW_TILES)], axis=1)


def _params(*sem):
    return pltpu.CompilerParams(dimension_semantics=sem, vmem_limit_bytes=VMEM_LIMIT)


def _proj_kernel(x_ref, g_ref, w_ref, wvt_ref, perm_ref, cos_ref, sin_ref,
                 rq_ref, rk_ref, rv_ref, rg_ref, sq_ref, sk_ref, svt_ref, gr_ref, gs_ref):
    h = _rms(x_ref[...], g_ref[...]).astype(BF16)
    hp = jnp.dot(perm_ref[...], h, preferred_element_type=F32).astype(BF16)
    cos = cos_ref[...]
    sin = sin_ref[...]

    def seg(lo, hi):
        return jnp.dot(h, w_ref[:, lo:hi], preferred_element_type=F32)

    def rope(t):
        outs = []
        for hd in range(RET_HEADS):
            th = t[:, hd * RET_DK:(hd + 1) * RET_DK]
            outs.append(th * cos + pltpu.roll(th, RET_DK // 2, 1) * sin)
        return jnp.concatenate(outs, axis=1)

    rq_ref[...] = rope(seg(0, 512)).astype(BF16)
    rk_ref[...] = (rope(seg(512, 1024)) * (RET_DK ** -0.5)).astype(BF16)
    rv_ref[...] = seg(1024, 1536).astype(BF16)
    rg_ref[...] = seg(1536, 2048).astype(BF16)
    sq_ref[...] = (seg(2048, 2560) * (SB_DH ** -0.5)).astype(BF16)
    sk_ref[...] = jnp.dot(hp, w_ref[:, 2560:3072], preferred_element_type=F32).astype(BF16)
    svt = lax.dot_general(wvt_ref[...], hp, (((1,), (1,)), ((), ())), preferred_element_type=F32)
    tk = svt_ref.shape[2]
    for kb in range(svt_ref.shape[0]):
        svt_ref[kb] = svt[:, kb * tk:(kb + 1) * tk].astype(BF16)
    gr_ref[...] = seg(3584, 4608).astype(BF16)
    gs_ref[...] = seg(4608, 5632).astype(BF16)


def _key_block_perm(tm, tk):
    grp = tk // SUBLANES
    new = np.arange(tm)
    blk, r = new // tk, new % tk
    old = blk * tk + (r % SUBLANES) * grp + r // SUBLANES
    p = np.zeros((tm, tm), np.float32)
    p[new, old] = 1.0
    return jnp.asarray(p, BF16)


def _proj(x2, g, w, wvt, cosf, sinf, seq):
    n, d = x2.shape
    tm = PROJ_ROWS
    tk = SB_BLOCK
    pos_blocks = seq // tm
    perm = _key_block_perm(tm, tk)
    row = lambda i: (i, 0)
    const = lambda i: (0, 0)
    half = jax.ShapeDtypeStruct((n, 512), BF16)
    full = jax.ShapeDtypeStruct((n, 1024), BF16)
    svt = jax.ShapeDtypeStruct((n // tk, 512, tk), BF16)
    hspec = pl.BlockSpec((tm, 512), row)
    fspec = pl.BlockSpec((tm, 1024), row)
    return pl.pallas_call(
        _proj_kernel,
        grid=(n // tm,),
        in_specs=[
            pl.BlockSpec((tm, d), row),
            pl.BlockSpec((1, d), const),
            pl.BlockSpec(w.shape, const),
            pl.BlockSpec(wvt.shape, const),
            pl.BlockSpec(perm.shape, const),
            pl.BlockSpec((tm, RET_DK), lambda i: (i % pos_blocks, 0)),
            pl.BlockSpec((tm, RET_DK), lambda i: (i % pos_blocks, 0)),
        ],
        out_specs=[hspec] * 6 + [pl.BlockSpec((tm // tk, 512, tk), lambda i: (i, 0, 0)), fspec, fspec],
        out_shape=[half] * 6 + [svt, full, full],
        compiler_params=_params("parallel"),
        name="proj",
    )(x2, g, w, wvt, perm, cosf, sinf)


def _ret_consts(tb):
    hd = np.arange(RET_HEADS, dtype=np.float64)
    lg = np.log(1.0 - np.exp2(-5.0 - hd))
    idx = np.arange(tb, dtype=np.float64)
    dist = np.abs(idx[:, None] - idx[None, :])
    chunk = (np.arange(tb) // CHUNK)
    allowed = chunk[None, :] <= chunk[:, None]
    dm = np.exp(lg[:, None, None] * dist) * allowed
    qdec = np.exp(lg[:, None] * (idx + 1.0))[:, :, None] * np.ones((1, 1, RET_DK))
    kdec = np.exp(lg[:, None] * (tb - 1.0 - idx))[:, :, None] * np.ones((1, 1, RET_DK))
    cdec = [float(v) for v in np.exp(lg * tb)]
    return (jnp.asarray(dm, F32), jnp.asarray(qdec, F32), jnp.asarray(kdec, F32), cdec)


def _ret_kernel(cdec, q_ref, k_ref, v_ref, g_ref, dm_ref, qd_ref, kd_ref, gn_ref, o_ref, st_ref):
    @pl.when(pl.program_id(1) == 0)
    def _():
        st_ref[...] = jnp.zeros_like(st_ref)

    for hd in range(RET_HEADS):
        sl = slice(hd * RET_DK, (hd + 1) * RET_DK)
        q = q_ref[:, sl]
        k = k_ref[:, sl]
        v = v_ref[:, sl]
        st = st_ref[hd]
        scores = lax.dot_general(q, k, (((1,), (1,)), ((), ())), preferred_element_type=F32)
        scores = scores * dm_ref[hd]
        qd = (q.astype(F32) * qd_ref[hd]).astype(BF16)
        o = (jnp.dot(scores.astype(BF16), v, preferred_element_type=F32)
             + jnp.dot(qd, st.astype(BF16), preferred_element_type=F32))
        kd = (k.astype(F32) * kd_ref[hd]).astype(BF16)
        st_ref[hd] = cdec[hd] * st + lax.dot_general(
            kd, v, (((0,), (0,)), ((), ())), preferred_element_type=F32)
        mu = jnp.mean(o, axis=-1, keepdims=True)
        oc = o - mu
        var = jnp.mean(oc * oc, axis=-1, keepdims=True)
        on = oc * lax.rsqrt(var + EPS) * gn_ref[:, sl]
        gate = g_ref[:, sl].astype(F32)
        o_ref[:, sl] = (gate * _sigmoid(gate) * on).astype(BF16)


def _retention(rq, rk, rv, rg, gn, batch, seq):
    n, w = rq.shape
    tb = RET_ROWS
    nc = seq // tb
    dm, qdec, kdec, cdec = _ret_consts(tb)
    blk = pl.BlockSpec((tb, w), lambda b, c: (b * nc + c, 0))
    c3 = lambda b, c: (0, 0, 0)
    return pl.pallas_call(
        functools.partial(_ret_kernel, cdec),
        grid=(batch, nc),
        in_specs=[blk, blk, blk, blk,
                  pl.BlockSpec(dm.shape, c3), pl.BlockSpec(qdec.shape, c3), pl.BlockSpec(kdec.shape, c3),
                  pl.BlockSpec((1, w), lambda b, c: (0, 0))],
        out_specs=blk,
        out_shape=jax.ShapeDtypeStruct((n, w), BF16),
        scratch_shapes=[pltpu.VMEM((RET_HEADS, RET_DK, RET_DK), F32)],
        compiler_params=_params("parallel", "arbitrary"),
        name="retention",
    )(rq, rk, rv, rg, dm, qdec, kdec, gn)


def _sb_kernel(q_ref, k_ref, vt_ref, o_ref, acc_ref):
    tq = q_ref.shape[0]
    tk = tq
    grp = tk // SUBLANES
    heads = q_ref.shape[1] // SB_DH
    wide = heads * tq
    assert tq & (tq - 1) == 0
    qi = pl.program_id(1)
    q = q_ref[...]
    lane = lax.broadcasted_iota(I32, q.shape, 1)
    q_all = jnp.concatenate(
        [jnp.where((lane >= hh * SB_DH) & (lane < (hh + 1) * SB_DH), q, jnp.zeros_like(q))
         for hh in range(heads)], axis=0)
    shp3 = (grp, SUBLANES, wide)
    key_pos = lax.broadcasted_iota(I32, shp3, 1) * grp + lax.broadcasted_iota(I32, shp3, 0)
    causal = key_pos < (lax.broadcasted_iota(I32, shp3, 2) & (tq - 1))
    row8 = lax.broadcasted_iota(I32, (SUBLANES, wide), 0)

    def block(kb, carry, diagonal):
        z = lax.dot_general(k_ref[kb], q_all, (((1,), (1,)), ((), ())),
                            preferred_element_type=F32) * LOG2E
        z3 = z.reshape(shp3)
        if diagonal:
            z3 = jnp.where(causal, z3, SB_MASKED)
        run = jnp.ones((SUBLANES, wide), F32)
        local = [None] * grp
        for g in reversed(range(grp)):
            ez = jnp.exp2(jnp.minimum(z3[g], SB_CLAMP))
            run = run * (1.0 / (1.0 + ez))
            local[g] = ez * run
        log_tot = jnp.log2(run)
        off = jnp.where(row8 + 1 < SUBLANES, pltpu.roll(log_tot, SUBLANES - 1, 0), 0.0)
        for d in (1, 2, 4):
            off = off + jnp.where(row8 + d < SUBLANES, pltpu.roll(off, SUBLANES - d, 0), 0.0)
        scale = jnp.exp2(off + carry)
        a3 = jnp.stack(local, axis=0) * scale[None]
        a = a3.reshape(tk, wide).astype(BF16)
        for hh in range(heads):
            rows = slice(hh * SB_DH, (hh + 1) * SB_DH)
            contrib = jnp.dot(vt_ref[kb, rows, :], a[:, hh * tq:(hh + 1) * tq], preferred_element_type=F32)
            if diagonal:
                acc_ref[rows, :] = contrib
            else:
                acc_ref[rows, :] += contrib
        return carry + jnp.broadcast_to(off[0:1, :] + log_tot[0:1, :], carry.shape)

    carry = block(qi, jnp.zeros((SUBLANES, wide), F32), True)

    def cond(c):
        return (c[0] < qi) & (c[1] > 0)

    def body(c):
        new = block(qi - 1 - c[0], c[2], False)
        go = (jnp.max(new) > SB_SKIP_LOG2).astype(I32)
        return (c[0] + 1, go, new)

    lax.while_loop(cond, body, (jnp.int32(0), jnp.int32(1), carry))
    o_ref[...] = acc_ref[...].T.astype(BF16)


def _stick_breaking(sq, kperm, vtperm, batch, seq):
    n, w = sq.shape
    tq = SB_BLOCK
    nq = seq // tq
    return pl.pallas_call(
        _sb_kernel,
        grid=(batch, nq),
        in_specs=[
            pl.BlockSpec((tq, w), lambda b, i: (b * nq + i, 0)),
            pl.BlockSpec((None, nq, tq, w), lambda b, i: (b, 0, 0, 0)),
            pl.BlockSpec((None, nq, w, tq), lambda b, i: (b, 0, 0, 0)),
        ],
        out_specs=pl.BlockSpec((tq, w), lambda b, i: (b * nq + i, 0)),
        out_shape=jax.ShapeDtypeStruct((n, w), BF16),
        scratch_shapes=[pltpu.VMEM((w, tq), F32)],
        compiler_params=_params("parallel", "arbitrary"),
        name="stickbrk",
    )(sq, kperm, vtperm)


def _merge_kernel(ret_ref, sb_ref, gr_ref, gs_ref, x_ref, wbr_ref, wbs_ref, wo_ref, nf_ref,
                  wr_ref, br_ref, tri_ref,
                  x1_ref, h2_ref, ri_ref, rw_ref, cnt_ref, carry_ref):
    @pl.when(pl.program_id(0) == 0)
    def _():
        carry_ref[...] = jnp.zeros_like(carry_ref)

    tm = x_ref.shape[0]
    merged = (_sigmoid(gr_ref[...].astype(F32)) * jnp.dot(ret_ref[...], wbr_ref[...], preferred_element_type=F32)
              + _sigmoid(gs_ref[...].astype(F32)) * jnp.dot(sb_ref[...], wbs_ref[...], preferred_element_type=F32))
    x1 = x_ref[...] + jnp.dot(merged.astype(BF16), wo_ref[...], preferred_element_type=F32)
    x1_ref[...] = x1
    h2 = _rms(x1, nf_ref[...])
    _to_row_tiles(h2_ref, h2)
    logits = jnp.dot(h2.astype(BF16), wr_ref[...], preferred_element_type=F32) + br_ref[...]
    lt = logits.T
    neg = jnp.float32(-jnp.inf)

    grow = lax.broadcasted_iota(I32, (SUBLANES, tm), 0)
    gl = jnp.where(grow < N_GROUPS, lt[0:SUBLANES], neg)
    gmax = jnp.max(gl, axis=0, keepdims=True)
    g_idx = jnp.min(jnp.where(gl == gmax, grow, N_GROUPS), axis=0, keepdims=True)
    p_grp = 1.0 / jnp.sum(jnp.exp(gl - gmax), axis=0, keepdims=True)

    erow = lax.broadcasted_iota(I32, (N_EXPERTS, tm), 0)
    el = lt[EXPERT_COL0:EXPERT_COL0 + N_EXPERTS]
    el = jnp.where((erow >= g_idx * EXPERTS_PER_GROUP) & (erow < (g_idx + 1) * EXPERTS_PER_GROUP), el, neg)
    m1 = jnp.max(el, axis=0, keepdims=True)
    i1 = jnp.min(jnp.where(el == m1, erow, N_EXPERTS), axis=0, keepdims=True)
    el2 = jnp.where(erow == i1, neg, el)
    m2 = jnp.max(el2, axis=0, keepdims=True)
    i2 = jnp.min(jnp.where(el2 == m2, erow, N_EXPERTS), axis=0, keepdims=True)
    e2 = jnp.exp(m2 - m1)
    w1 = p_grp / (1.0 + e2)
    w2 = p_grp * e2 / (1.0 + e2)

    l1 = i1 - g_idx * EXPERTS_PER_GROUP
    l2 = i2 - g_idx * EXPERTS_PER_GROUP
    lo = jnp.minimum(l1, l2)
    hi = jnp.maximum(l1, l2)
    cls = g_idx * PAIRS_PER_GROUP + ((lo * (2 * EXPERTS_PER_GROUP - 1 - lo)) >> 1) + (hi - lo - 1)
    w_lo = jnp.where(l1 < l2, w1, w2)
    w_hi = jnp.where(l1 < l2, w2, w1)

    crow = lax.broadcasted_iota(I32, (CLASS_ROWS, tm), 0)
    sel = crow == cls
    onehot = sel.astype(BF16)
    cinc = jnp.dot(onehot, tri_ref[...], preferred_element_type=F32)
    carry = carry_ref[...]
    rank = jnp.sum(jnp.where(sel, carry[:, 0:1] + cinc - 1.0, 0.0), axis=0, keepdims=True)
    carry_new = carry + jnp.sum(onehot.astype(F32), axis=1, keepdims=True)
    carry_ref[...] = carry_new
    cnt_ref[...] = carry_new

    ri_ref[...] = jnp.zeros_like(ri_ref)
    ri_ref[0:1, :] = cls
    ri_ref[1:2, :] = rank.astype(I32)
    wrow = lax.broadcasted_iota(I32, (ROUTE_COLS, tm), 0)
    wt = jnp.where(wrow == 0, w_lo, jnp.where(wrow == 1, w_hi, 0.0))
    rw_ref[...] = wt.T


def _merge(ret, sb, g_ret, g_sb, x2, wbr, wbs, wo, nf, wr, br):
    n, d = x2.shape
    tm = MERGE_ROWS
    tri = jnp.asarray(np.triu(np.ones((tm, tm), np.float32)), BF16)
    row = lambda i: (i, 0)
    const = lambda i: (0, 0)
    full = lambda a: pl.BlockSpec(a.shape, const)
    return pl.pallas_call(
        _merge_kernel,
        grid=(n // tm,),
        in_specs=[pl.BlockSpec((tm, 512), row), pl.BlockSpec((tm, 512), row),
                  pl.BlockSpec((tm, d), row), pl.BlockSpec((tm, d), row), pl.BlockSpec((tm, d), row),
                  full(wbr), full(wbs), full(wo), full(nf), full(wr), full(br), full(tri)],
        out_specs=[pl.BlockSpec((tm, d), row), pl.BlockSpec((tm * ROW_TILES, LANES), row),
                   pl.BlockSpec((SUBLANES, tm), lambda i: (0, i)),
                   pl.BlockSpec((tm, ROUTE_COLS), row),
                   pl.BlockSpec((CLASS_ROWS, LANES), const)],
        out_shape=[jax.ShapeDtypeStruct((n, d), F32), jax.ShapeDtypeStruct((n * ROW_TILES, LANES), F32),
                   jax.ShapeDtypeStruct((SUBLANES, n), I32),
                   jax.ShapeDtypeStruct((n, ROUTE_COLS), F32),
                   jax.ShapeDtypeStruct((CLASS_ROWS, LANES), F32)],
        scratch_shapes=[pltpu.VMEM((CLASS_ROWS, LANES), F32)],
        compiler_params=_params("arbitrary"),
        name="merge",
    )(ret, sb, g_ret, g_sb, x2, wbr, wbs, wo, nf, wr, br, tri)


DMA_QUEUES = 2


def _tile_rows(i):
    return pl.ds(pl.multiple_of(i * ROW_TILES, ROW_TILES), ROW_TILES)


def _issue_rows(idx_ref, src_hbm, dst, sem, rows):
    def body(i, c):
        for k in range(DMA_QUEUES):
            r = i * DMA_QUEUES + k
            pltpu.make_async_copy(src_hbm.at[_tile_rows(idx_ref[0, 0, r])], dst.at[_tile_rows(r)],
                                  sem).start(priority=k)
        return c
    lax.fori_loop(0, rows // DMA_QUEUES, body, 0, unroll=4)


def _expert_kernel(bl_ref, bh_ref, nu_ref, idx_ref, idxn_ref, x_hbm,
                   wgl_ref, wul_ref, wdl_ref, wgh_ref, wuh_ref, wdh_ref, o_ref, xbuf, sem):
    j = pl.program_id(0)
    used = nu_ref[0]
    rows = xbuf.shape[1] // ROW_TILES
    slot = j % 2

    @pl.when(j == 0)
    def _():
        _issue_rows(idx_ref, x_hbm, xbuf.at[0], sem.at[0], rows)

    @pl.when(j + 1 < used)
    def _():
        _issue_rows(idxn_ref, x_hbm, xbuf.at[1 - slot], sem.at[1 - slot], rows)

    @pl.when(j < used)
    def _():
        pltpu.make_async_copy(xbuf.at[slot], xbuf.at[slot], sem.at[slot]).wait()
        xb = _from_row_tiles(xbuf.at[slot], 0, rows).astype(BF16)

        def ffn(wg_ref, wu_ref, wd_ref):
            gate = jnp.dot(xb, wg_ref[...], preferred_element_type=F32)
            up = jnp.dot(xb, wu_ref[...], preferred_element_type=F32)
            hid = (gate * _sigmoid(gate) * up).astype(BF16)
            return jnp.dot(hid, wd_ref[...], preferred_element_type=F32)

        _to_row_tiles(o_ref.at[0], ffn(wgl_ref, wul_ref, wdl_ref))
        _to_row_tiles(o_ref.at[1], ffn(wgh_ref, wuh_ref, wdh_ref))

    @pl.when(j >= used)
    def _():
        o_ref[...] = jnp.zeros_like(o_ref)


def _experts(block_lo, block_hi, n_used, buf_tok3, h2, wg, wu, wd):
    nb, _, tmb = buf_tok3.shape
    d = wg.shape[1]
    de = wg.shape[2]
    lo_map = lambda j, bl, bh, nu: (bl[jnp.minimum(j, nu[0] - 1)], 0, 0)
    hi_map = lambda j, bl, bh, nu: (bh[jnp.minimum(j, nu[0] - 1)], 0, 0)
    return pl.pallas_call(
        _expert_kernel,
        grid_spec=pltpu.PrefetchScalarGridSpec(
            num_scalar_prefetch=3,
            grid=(nb,),
            in_specs=[
                pl.BlockSpec((1, 1, tmb), lambda j, bl, bh, nu: (j, 0, 0), memory_space=pltpu.SMEM),
                pl.BlockSpec((1, 1, tmb), lambda j, bl, bh, nu: (jnp.minimum(j + 1, nb - 1), 0, 0),
                             memory_space=pltpu.SMEM),
                pl.BlockSpec(memory_space=pl.ANY),
                pl.BlockSpec((None, d, de), lo_map),
                pl.BlockSpec((None, d, de), lo_map),
                pl.BlockSpec((None, de, d), lo_map),
                pl.BlockSpec((None, d, de), hi_map),
                pl.BlockSpec((None, d, de), hi_map),
                pl.BlockSpec((None, de, d), hi_map),
            ],
            out_specs=pl.BlockSpec((2, tmb * ROW_TILES, LANES), lambda j, bl, bh, nu: (0, j, 0)),
            scratch_shapes=[pltpu.VMEM((2, tmb * ROW_TILES, LANES), F32), pltpu.SemaphoreType.DMA((2,))],
        ),
        out_shape=jax.ShapeDtypeStruct((2, nb * tmb * ROW_TILES, LANES), F32),
        compiler_params=_params("arbitrary"),
        name="experts",
    )(block_lo, block_hi, n_used, buf_tok3, buf_tok3, h2, wg, wu, wd, wg, wu, wd)


def _issue_pairs(idx_ref, yb_hbm, dst, sem, rows):
    def body(i, c):
        for k in range(DMA_QUEUES):
            r = i * DMA_QUEUES + k
            pltpu.make_async_copy(yb_hbm.at[:, _tile_rows(idx_ref[0, 0, r]), :], dst.at[:, _tile_rows(r), :],
                                  sem).start(priority=k)
        return c
    lax.fori_loop(0, rows // DMA_QUEUES, body, 0, unroll=4)


def _final_kernel(idx_ref, idxn_ref, x1_ref, rw_ref, p_ref, yb_hbm, npl_ref, wpg_ref, wp_ref, nfin_ref,
                  o_ref, gbuf, sem, x2_ref):
    i = pl.program_id(0)
    nt = pl.num_programs(0)
    tm = x1_ref.shape[0]
    slot = i % 2

    @pl.when(i == 0)
    def _():
        _issue_pairs(idx_ref, yb_hbm, gbuf.at[0], sem.at[0], tm)

    @pl.when(i + 1 < nt)
    def _():
        _issue_pairs(idxn_ref, yb_hbm, gbuf.at[1 - slot], sem.at[1 - slot], tm)

    pltpu.make_async_copy(gbuf.at[slot], gbuf.at[slot], sem.at[slot]).wait()
    w_lo = jnp.broadcast_to(rw_ref[:, 0:1], (tm, LANES))
    w_hi = jnp.broadcast_to(rw_ref[:, 1:2], (tm, LANES))
    for s in range(ROW_TILES):
        cols = slice(s * LANES, (s + 1) * LANES)
        x2_ref[:, cols] = (x1_ref[:, cols]
                           + w_lo * gbuf[slot, 0, pl.ds(s, tm, stride=ROW_TILES), :]
                           + w_hi * gbuf[slot, 1, pl.ds(s, tm, stride=ROW_TILES), :])
    x2 = x2_ref[...]
    hp = _rms(x2, npl_ref[...]).astype(BF16)
    gate = _sigmoid(jnp.dot(hp, wpg_ref[...], preferred_element_type=F32))
    emb = jnp.dot(p_ref[...].astype(BF16), wp_ref[...], preferred_element_type=F32)
    x3 = x2 + gate * emb
    o_ref[...] = _rms(x3, nfin_ref[...])


def _final(dest3, x1, rw, p2, yb, npl, wpg, wp, nfin):
    n, d = x1.shape
    tm = FINAL_ROWS
    nt = n // tm
    row = lambda i: (i, 0)
    const = lambda i: (0, 0)
    full = lambda a: pl.BlockSpec(a.shape, const)
    return pl.pallas_call(
        _final_kernel,
        grid=(nt,),
        in_specs=[
            pl.BlockSpec((1, 1, tm), lambda i: (i, 0, 0), memory_space=pltpu.SMEM),
            pl.BlockSpec((1, 1, tm), lambda i: (jnp.minimum(i + 1, nt - 1), 0, 0), memory_space=pltpu.SMEM),
            pl.BlockSpec((tm, d), row),
            pl.BlockSpec((tm, ROUTE_COLS), row),
            pl.BlockSpec((tm, p2.shape[1]), row),
            pl.BlockSpec(memory_space=pl.ANY),
            full(npl), full(wpg), full(wp), full(nfin),
        ],
        out_specs=pl.BlockSpec((tm, d), row),
        out_shape=jax.ShapeDtypeStruct((n, d), F32),
        scratch_shapes=[pltpu.VMEM((2, 2, tm * ROW_TILES, LANES), F32), pltpu.SemaphoreType.DMA((2,)),
                        pltpu.VMEM((tm, d), F32)],
        compiler_params=_params("arbitrary"),
        name="final",
    )(dest3, dest3, x1, rw, p2, yb, npl, wpg, wp, nfin)


def _layer(x2, p2, batch, seq, cosf, sinf, norm_mix, w_in, ret_gn, w_br_ret, w_br_sb, w_out, norm_ffn,
           w_grp, b_grp, w_exp, b_exp, w_gate, w_up, w_down, norm_ple, w_ple, w_ple_gate, out_gain):
    n, d = x2.shape
    wv_lo = 2 * 512 + 2 * 512 + 2 * 512
    rq, rk, rv, rg, sq, skp, svt, g_ret, g_sb = _proj(
        x2, norm_mix[None, :], w_in.astype(BF16), w_in[:, wv_lo:wv_lo + 512].T.astype(BF16), cosf, sinf, seq)
    ret = _retention(rq, rk, rv, rg, ret_gn[None, :], batch, seq)

    tk = SB_BLOCK
    nkb = seq // tk
    kperm = skp.reshape(batch, nkb, tk, skp.shape[1])
    vtperm = svt.reshape(batch, nkb, svt.shape[1], tk)
    sb = _stick_breaking(sq, kperm, vtperm, batch, seq)

    wr = jnp.zeros((d, ROUTE_COLS), F32)
    wr = wr.at[:, 0:N_GROUPS].set(w_grp).at[:, EXPERT_COL0:EXPERT_COL0 + N_EXPERTS].set(w_exp)
    br = jnp.zeros((1, ROUTE_COLS), F32)
    br = br.at[0, 0:N_GROUPS].set(b_grp).at[0, EXPERT_COL0:EXPERT_COL0 + N_EXPERTS].set(b_exp)
    x1, h2, ri, rw, cnt = _merge(ret, sb, g_ret, g_sb, x2, w_br_ret.astype(BF16), w_br_sb.astype(BF16),
                                 w_out.astype(BF16), norm_ffn[None, :], wr.astype(BF16), br)

    tmb = EXPERT_ROWS
    counts = cnt[:N_CLASSES, 0].astype(I32)
    padded = (counts + tmb - 1) // tmb * tmb
    pad_end = jnp.cumsum(padded)
    pad_start = pad_end - padded
    cids = jnp.arange(N_CLASSES, dtype=I32)
    dest = jnp.sum(jnp.where(ri[0, :, None] == cids, pad_start, 0), axis=-1) + ri[1]
    nb = n // tmb + N_CLASSES
    buf_tok = jnp.zeros((nb * tmb,), I32).at[dest].set(jnp.arange(n, dtype=I32))
    block_c = jnp.minimum(
        jnp.sum((pad_end[None, :] <= (jnp.arange(nb, dtype=I32) * tmb)[:, None]).astype(I32), axis=1),
        N_CLASSES - 1)
    pairs = [(lo, hi) for lo in range(EXPERTS_PER_GROUP) for hi in range(lo + 1, EXPERTS_PER_GROUP)]
    class_lo = np.array([g * EXPERTS_PER_GROUP + lo for g in range(N_GROUPS) for lo, _ in pairs], np.int32)
    class_hi = np.array([g * EXPERTS_PER_GROUP + hi for g in range(N_GROUPS) for _, hi in pairs], np.int32)
    block_lo = jnp.sum(jnp.where(block_c[:, None] == cids, class_lo, 0), axis=-1).astype(I32)
    block_hi = jnp.sum(jnp.where(block_c[:, None] == cids, class_hi, 0), axis=-1).astype(I32)
    n_used = (pad_end[-1:] // tmb).astype(I32)
    yb = _experts(block_lo, block_hi, n_used, buf_tok.reshape(nb, 1, tmb), h2,
                  w_gate.astype(BF16), w_up.astype(BF16), w_down.astype(BF16))

    tf = FINAL_ROWS
    return _final(dest.reshape(n // tf, 1, tf), x1, rw, p2, yb, norm_ple[None, :],
                  w_ple_gate.astype(BF16), w_ple.astype(BF16), out_gain)


def kernel(x, p, norm_mix, w_in, ret_gn, w_br_ret, w_br_sb, w_out, norm_ffn, w_grp, b_grp, w_exp, b_exp,
           w_gate, w_up, w_down, norm_ple, w_ple, w_ple_gate, norm_final):
    batch, seq, d = x.shape
    depth = p.shape[0]
    assert depth == 1, "the final RMSNorm is fused into the last layer's kernel"
    pos = jnp.arange(seq, dtype=F32)
    inv_freq = ROPE_BASE ** (-jnp.arange(0, RET_DK, 2, dtype=F32) / RET_DK)
    ang = pos[:, None] * inv_freq[None, :]
    cos, sin = jnp.cos(ang), jnp.sin(ang)
    cosf = jnp.concatenate([cos, cos], axis=1)
    sinf = jnp.concatenate([-sin, sin], axis=1)
    x2 = x.reshape(batch * seq, d)
    i = 0
    out = _layer(x2, p[i].reshape(batch * seq, -1), batch, seq, cosf, sinf, norm_mix[i], w_in[i], ret_gn[i],
                 w_br_ret[i], w_br_sb[i], w_out[i], norm_ffn[i], w_grp[i], b_grp[i], w_exp[i], b_exp[i],
                 w_gate[i], w_up[i], w_down[i], norm_ple[i], w_ple[i], w_ple_gate[i], norm_final[None, :])
    return out.reshape(batch, seq, d)
```

```python
import functools

import numpy as np
import jax
import jax.numpy as jnp
from jax import lax
from jax.experimental import pallas as pl
from jax.experimental.pallas import tpu as pltpu

F32 = jnp.float32
BF16 = jnp.bfloat16
I32 = jnp.int32

EPS = 1e-6
CHUNK = 64
RET_HEADS = 4
RET_DK = 128
SB_HEADS = 8
SB_DH = 64
ROPE_BASE = 10000.0
N_GROUPS = 4
EXPERTS_PER_GROUP = 4
N_EXPERTS = N_GROUPS * EXPERTS_PER_GROUP
PAIRS_PER_GROUP = EXPERTS_PER_GROUP * (EXPERTS_PER_GROUP - 1) // 2
N_CLASSES = N_GROUPS * PAIRS_PER_GROUP
CLASS_ROWS = 32

LANES = 128
SUBLANES = 8
VMEM_LIMIT = 56 * 1024 * 1024

PROJ_ROWS = 512
RET_ROWS = 512
SB_BLOCK = 256
MERGE_ROWS = 512
EXPERT_ROWS = 512
FINAL_ROWS = 512
ROUTE_COLS = 128
ROW_TILES = 8
EXPERT_COL0 = 8
LOG2E = 1.4426950408889634
SB_SKIP_LOG2 = -160.0
SB_CLAMP = 126.0
SB_MASKED = -1e30


def _rms(x, g):
    return x * lax.rsqrt(jnp.mean(x * x, axis=-1, keepdims=True) + EPS) * g


def _sigmoid(x):
    return 1.0 / (1.0 + jnp.exp(-x))


def _to_row_tiles(ref, val):
    rows = val.shape[0]
    for s in range(ROW_TILES):
        ref[pl.ds(s, rows, stride=ROW_TILES), :] = val[:, s * LANES:(s + 1) * LANES]


def _from_row_tiles(ref, first, rows):
    return jnp.concatenate([ref[pl.ds(first * ROW_TILES + s, rows, stride=ROW_TILES), :]
                            for s in range(ROW_TILES)], axis=1)


def _params(*sem):
    return pltpu.CompilerParams(dimension_semantics=sem, vmem_limit_bytes=VMEM_LIMIT)


def _proj_kernel(x_ref, g_ref, w_ref, wvt_ref, perm_ref, cos_ref, sin_ref,
                 rq_ref, rk_ref, rv_ref, rg_ref, sq_ref, sk_ref, svt_ref, gr_ref, gs_ref):
    h = _rms(x_ref[...], g_ref[...]).astype(BF16)
    hp = jnp.dot(perm_ref[...], h, preferred_element_type=F32).astype(BF16)
    cos = cos_ref[...]
    sin = sin_ref[...]

    def seg(lo, hi):
        return jnp.dot(h, w_ref[:, lo:hi], preferred_element_type=F32)

    def rope(t):
        outs = []
        for hd in range(RET_HEADS):
            th = t[:, hd * RET_DK:(hd + 1) * RET_DK]
            outs.append(th * cos + pltpu.roll(th, RET_DK // 2, 1) * sin)
        return jnp.concatenate(outs, axis=1)

    rq_ref[...] = rope(seg(0, 512)).astype(BF16)
    rk_ref[...] = (rope(seg(512, 1024)) * (RET_DK ** -0.5)).astype(BF16)
    rv_ref[...] = seg(1024, 1536).astype(BF16)
    rg_ref[...] = seg(1536, 2048).astype(BF16)
    sq_ref[...] = (seg(2048, 2560) * (SB_DH ** -0.5)).astype(BF16)
    sk_ref[...] = jnp.dot(hp, w_ref[:, 2560:3072], preferred_element_type=F32).astype(BF16)
    svt = lax.dot_general(wvt_ref[...], hp, (((1,), (1,)), ((), ())), preferred_element_type=F32)
    tk = svt_ref.shape[2]
    for kb in range(svt_ref.shape[0]):
        svt_ref[kb] = svt[:, kb * tk:(kb + 1) * tk].astype(BF16)
    gr_ref[...] = seg(3584, 4608).astype(BF16)
    gs_ref[...] = seg(4608, 5632).astype(BF16)


def _key_block_perm(tm, tk):
    grp = tk // SUBLANES
    new = np.arange(tm)
    blk, r = new // tk, new % tk
    old = blk * tk + (r % SUBLANES) * grp + r // SUBLANES
    p = np.zeros((tm, tm), np.float32)
    p[new, old] = 1.0
    return jnp.asarray(p, BF16)


def _proj(x2, g, w, wvt, cosf, sinf, seq):
    n, d = x2.shape
    tm = PROJ_ROWS
    tk = SB_BLOCK
    pos_blocks = seq // tm
    perm = _key_block_perm(tm, tk)
    row = lambda i: (i, 0)
    const = lambda i: (0, 0)
    half = jax.ShapeDtypeStruct((n, 512), BF16)
    full = jax.ShapeDtypeStruct((n, 1024), BF16)
    svt = jax.ShapeDtypeStruct((n // tk, 512, tk), BF16)
    hspec = pl.BlockSpec((tm, 512), row)
    fspec = pl.BlockSpec((tm, 1024), row)
    return pl.pallas_call(
        _proj_kernel,
        grid=(n // tm,),
        in_specs=[
            pl.BlockSpec((tm, d), row),
            pl.BlockSpec((1, d), const),
            pl.BlockSpec(w.shape, const),
            pl.BlockSpec(wvt.shape, const),
            pl.BlockSpec(perm.shape, const),
            pl.BlockSpec((tm, RET_DK), lambda i: (i % pos_blocks, 0)),
            pl.BlockSpec((tm, RET_DK), lambda i: (i % pos_blocks, 0)),
        ],
        out_specs=[hspec] * 6 + [pl.BlockSpec((tm // tk, 512, tk), lambda i: (i, 0, 0)), fspec, fspec],
        out_shape=[half] * 6 + [svt, full, full],
        compiler_params=_params("parallel"),
        name="proj",
    )(x2, g, w, wvt, perm, cosf, sinf)


def _ret_consts(tb):
    hd = np.arange(RET_HEADS, dtype=np.float64)
    lg = np.log(1.0 - np.exp2(-5.0 - hd))
    idx = np.arange(tb, dtype=np.float64)
    dist = np.abs(idx[:, None] - idx[None, :])
    chunk = (np.arange(tb) // CHUNK)
    allowed = chunk[None, :] <= chunk[:, None]
    dm = np.exp(lg[:, None, None] * dist) * allowed
    qdec = np.exp(lg[:, None] * (idx + 1.0))[:, :, None] * np.ones((1, 1, RET_DK))
    kdec = np.exp(lg[:, None] * (tb - 1.0 - idx))[:, :, None] * np.ones((1, 1, RET_DK))
    cdec = [float(v) for v in np.exp(lg * tb)]
    return (jnp.asarray(dm, F32), jnp.asarray(qdec, F32), jnp.asarray(kdec, F32), cdec)


def _ret_kernel(cdec, q_ref, k_ref, v_ref, g_ref, dm_ref, qd_ref, kd_ref, gn_ref, o_ref, st_ref):
    @pl.when(pl.program_id(1) == 0)
    def _():
        st_ref[...] = jnp.zeros_like(st_ref)

    for hd in range(RET_HEADS):
        sl = slice(hd * RET_DK, (hd + 1) * RET_DK)
        q = q_ref[:, sl]
        k = k_ref[:, sl]
        v = v_ref[:, sl]
        st = st_ref[hd]
        scores = lax.dot_general(q, k, (((1,), (1,)), ((), ())), preferred_element_type=F32)
        scores = scores * dm_ref[hd]
        qd = (q.astype(F32) * qd_ref[hd]).astype(BF16)
        o = (jnp.dot(scores.astype(BF16), v, preferred_element_type=F32)
             + jnp.dot(qd, st.astype(BF16), preferred_element_type=F32))
        kd = (k.astype(F32) * kd_ref[hd]).astype(BF16)
        st_ref[hd] = cdec[hd] * st + lax.dot_general(
            kd, v, (((0,), (0,)), ((), ())), preferred_element_type=F32)
        mu = jnp.mean(o, axis=-1, keepdims=True)
        oc = o - mu
        var = jnp.mean(oc * oc, axis=-1, keepdims=True)
        on = oc * lax.rsqrt(var + EPS) * gn_ref[:, sl]
        gate = g_ref[:, sl].astype(F32)
        o_ref[:, sl] = (gate * _sigmoid(gate) * on).astype(BF16)


def _retention(rq, rk, rv, rg, gn, batch, seq):
    n, w = rq.shape
    tb = RET_ROWS
    nc = seq // tb
    dm, qdec, kdec, cdec = _ret_consts(tb)
    blk = pl.BlockSpec((tb, w), lambda b, c: (b * nc + c, 0))
    c3 = lambda b, c: (0, 0, 0)
    return pl.pallas_call(
        functools.partial(_ret_kernel, cdec),
        grid=(batch, nc),
        in_specs=[blk, blk, blk, blk,
                  pl.BlockSpec(dm.shape, c3), pl.BlockSpec(qdec.shape, c3), pl.BlockSpec(kdec.shape, c3),
                  pl.BlockSpec((1, w), lambda b, c: (0, 0))],
        out_specs=blk,
        out_shape=jax.ShapeDtypeStruct((n, w), BF16),
        scratch_shapes=[pltpu.VMEM((RET_HEADS, RET_DK, RET_DK), F32)],
        compiler_params=_params("parallel", "arbitrary"),
        name="retention",
    )(rq, rk, rv, rg, dm, qdec, kdec, gn)


def _sb_kernel(q_ref, k_ref, vt_ref, o_ref, acc_ref):
    tq = q_ref.shape[0]
    tk = tq
    grp = tk // SUBLANES
    heads = q_ref.shape[1] // SB_DH
    wide = heads * tq
    assert tq & (tq - 1) == 0
    qi = pl.program_id(1)
    per_group = LANES // SB_DH
    lane = lax.broadcasted_iota(I32, (tq, LANES), 1)
    q_groups = []
    for gi in range(heads // per_group):
        qg = q_ref[:, gi * LANES:(gi + 1) * LANES]
        q_groups.append(jnp.concatenate(
            [jnp.where((lane >= hh * SB_DH) & (lane < (hh + 1) * SB_DH), qg, jnp.zeros_like(qg))
             for hh in range(per_group)], axis=0))
    shp3 = (grp, SUBLANES, wide)
    key_pos = lax.broadcasted_iota(I32, shp3, 1) * grp + lax.broadcasted_iota(I32, shp3, 0)
    causal = key_pos < (lax.broadcasted_iota(I32, shp3, 2) & (tq - 1))
    row8 = lax.broadcasted_iota(I32, (SUBLANES, wide), 0)

    def block(kb, carry, diagonal):
        z = jnp.concatenate(
            [lax.dot_general(k_ref[kb, :, gi * LANES:(gi + 1) * LANES], qg, (((1,), (1,)), ((), ())),
                             preferred_element_type=F32) for gi, qg in enumerate(q_groups)],
            axis=1) * LOG2E
        z3 = z.reshape(shp3)
        if diagonal:
            z3 = jnp.where(causal, z3, SB_MASKED)
        run = jnp.ones((SUBLANES, wide), F32)
        local = [None] * grp
        for g in reversed(range(grp)):
            ez = jnp.exp2(jnp.minimum(z3[g], SB_CLAMP))
            run = run * (1.0 / (1.0 + ez))
            local[g] = ez * run
        log_tot = jnp.log2(run)
        off = jnp.where(row8 + 1 < SUBLANES, pltpu.roll(log_tot, SUBLANES - 1, 0), 0.0)
        for d in (1, 2, 4):
            off = off + jnp.where(row8 + d < SUBLANES, pltpu.roll(off, SUBLANES - d, 0), 0.0)
        scale = jnp.exp2(off + carry)
        a3 = jnp.stack(local, axis=0) * scale[None]
        a = a3.reshape(tk, wide).astype(BF16)
        for hh in range(heads):
            rows = slice(hh * SB_DH, (hh + 1) * SB_DH)
            contrib = jnp.dot(vt_ref[kb, rows, :], a[:, hh * tq:(hh + 1) * tq], preferred_element_type=F32)
            if diagonal:
                acc_ref[rows, :] = contrib
            else:
                acc_ref[rows, :] += contrib
        return carry + jnp.broadcast_to(off[0:1, :] + log_tot[0:1, :], carry.shape)

    carry = block(qi, jnp.zeros((SUBLANES, wide), F32), True)

    def cond(c):
        return (c[0] < qi) & (c[1] > 0)

    def body(c):
        new = block(qi - 1 - c[0], c[2], False)
        go = (jnp.max(new) > SB_SKIP_LOG2).astype(I32)
        return (c[0] + 1, go, new)

    lax.while_loop(cond, body, (jnp.int32(0), jnp.int32(1), carry))
    o_ref[...] = acc_ref[...].T.astype(BF16)


def _stick_breaking(sq, kperm, vtperm, batch, seq):
    n, w = sq.shape
    tq = SB_BLOCK
    nq = seq // tq
    return pl.pallas_call(
        _sb_kernel,
        grid=(batch, nq),
        in_specs=[
            pl.BlockSpec((tq, w), lambda b, i: (b * nq + i, 0)),
            pl.BlockSpec((None, nq, tq, w), lambda b, i: (b, 0, 0, 0)),
            pl.BlockSpec((None, nq, w, tq), lambda b, i: (b, 0, 0, 0)),
        ],
        out_specs=pl.BlockSpec((tq, w), lambda b, i: (b * nq + i, 0)),
        out_shape=jax.ShapeDtypeStruct((n, w), BF16),
        scratch_shapes=[pltpu.VMEM((w, tq), F32)],
        compiler_params=_params("parallel", "arbitrary"),
        name="stickbrk",
    )(sq, kperm, vtperm)


def _merge_kernel(ret_ref, sb_ref, gr_ref, gs_ref, x_ref, wbr_ref, wbs_ref, wo_ref, nf_ref,
                  wr_ref, br_ref, tri_ref,
                  x1_ref, h2_ref, ri_ref, rw_ref, cnt_ref, carry_ref):
    @pl.when(pl.program_id(0) == 0)
    def _():
        carry_ref[...] = jnp.zeros_like(carry_ref)

    tm = x_ref.shape[0]
    merged = (_sigmoid(gr_ref[...].astype(F32)) * jnp.dot(ret_ref[...], wbr_ref[...], preferred_element_type=F32)
              + _sigmoid(gs_ref[...].astype(F32)) * jnp.dot(sb_ref[...], wbs_ref[...], preferred_element_type=F32))
    x1 = x_ref[...] + jnp.dot(merged.astype(BF16), wo_ref[...], preferred_element_type=F32)
    x1_ref[...] = x1
    h2 = _rms(x1, nf_ref[...])
    _to_row_tiles(h2_ref, h2)
    logits = jnp.dot(h2.astype(BF16), wr_ref[...], preferred_element_type=F32) + br_ref[...]
    lt = logits.T
    neg = jnp.float32(-jnp.inf)

    grow = lax.broadcasted_iota(I32, (SUBLANES, tm), 0)
    gl = jnp.where(grow < N_GROUPS, lt[0:SUBLANES], neg)
    gmax = jnp.max(gl, axis=0, keepdims=True)
    g_idx = jnp.min(jnp.where(gl == gmax, grow, N_GROUPS), axis=0, keepdims=True)
    p_grp = 1.0 / jnp.sum(jnp.exp(gl - gmax), axis=0, keepdims=True)

    erow = lax.broadcasted_iota(I32, (N_EXPERTS, tm), 0)
    el = lt[EXPERT_COL0:EXPERT_COL0 + N_EXPERTS]
    el = jnp.where((erow >= g_idx * EXPERTS_PER_GROUP) & (erow < (g_idx + 1) * EXPERTS_PER_GROUP), el, neg)
    m1 = jnp.max(el, axis=0, keepdims=True)
    i1 = jnp.min(jnp.where(el == m1, erow, N_EXPERTS), axis=0, keepdims=True)
    el2 = jnp.where(erow == i1, neg, el)
    m2 = jnp.max(el2, axis=0, keepdims=True)
    i2 = jnp.min(jnp.where(el2 == m2, erow, N_EXPERTS), axis=0, keepdims=True)
    e2 = jnp.exp(m2 - m1)
    w1 = p_grp / (1.0 + e2)
    w2 = p_grp * e2 / (1.0 + e2)

    l1 = i1 - g_idx * EXPERTS_PER_GROUP
    l2 = i2 - g_idx * EXPERTS_PER_GROUP
    lo = jnp.minimum(l1, l2)
    hi = jnp.maximum(l1, l2)
    cls = g_idx * PAIRS_PER_GROUP + ((lo * (2 * EXPERTS_PER_GROUP - 1 - lo)) >> 1) + (hi - lo - 1)
    w_lo = jnp.where(l1 < l2, w1, w2)
    w_hi = jnp.where(l1 < l2, w2, w1)

    crow = lax.broadcasted_iota(I32, (CLASS_ROWS, tm), 0)
    sel = crow == cls
    onehot = sel.astype(BF16)
    cinc = jnp.dot(onehot, tri_ref[...], preferred_element_type=F32)
    carry = carry_ref[...]
    rank = jnp.sum(jnp.where(sel, carry[:, 0:1] + cinc - 1.0, 0.0), axis=0, keepdims=True)
    carry_new = carry + jnp.sum(onehot.astype(F32), axis=1, keepdims=True)
    carry_ref[...] = carry_new
    cnt_ref[...] = carry_new

    ri_ref[...] = jnp.zeros_like(ri_ref)
    ri_ref[0:1, :] = cls
    ri_ref[1:2, :] = rank.astype(I32)
    wrow = lax.broadcasted_iota(I32, (ROUTE_COLS, tm), 0)
    wt = jnp.where(wrow == 0, w_lo, jnp.where(wrow == 1, w_hi, 0.0))
    rw_ref[...] = wt.T


def _merge(ret, sb, g_ret, g_sb, x2, wbr, wbs, wo, nf, wr, br):
    n, d = x2.shape
    tm = MERGE_ROWS
    tri = jnp.asarray(np.triu(np.ones((tm, tm), np.float32)), BF16)
    row = lambda i: (i, 0)
    const = lambda i: (0, 0)
    full = lambda a: pl.BlockSpec(a.shape, const)
    return pl.pallas_call(
        _merge_kernel,
        grid=(n // tm,),
        in_specs=[pl.BlockSpec((tm, 512), row), pl.BlockSpec((tm, 512), row),
                  pl.BlockSpec((tm, d), row), pl.BlockSpec((tm, d), row), pl.BlockSpec((tm, d), row),
                  full(wbr), full(wbs), full(wo), full(nf), full(wr), full(br), full(tri)],
        out_specs=[pl.BlockSpec((tm, d), row), pl.BlockSpec((tm * ROW_TILES, LANES), row),
                   pl.BlockSpec((SUBLANES, tm), lambda i: (0, i)),
                   pl.BlockSpec((tm, ROUTE_COLS), row),
                   pl.BlockSpec((CLASS_ROWS, LANES), const)],
        out_shape=[jax.ShapeDtypeStruct((n, d), F32), jax.ShapeDtypeStruct((n * ROW_TILES, LANES), F32),
                   jax.ShapeDtypeStruct((SUBLANES, n), I32),
                   jax.ShapeDtypeStruct((n, ROUTE_COLS), F32),
                   jax.ShapeDtypeStruct((CLASS_ROWS, LANES), F32)],
        scratch_shapes=[pltpu.VMEM((CLASS_ROWS, LANES), F32)],
        compiler_params=_params("arbitrary"),
        name="merge",
    )(ret, sb, g_ret, g_sb, x2, wbr, wbs, wo, nf, wr, br, tri)


DMA_QUEUES = 2


def _tile_rows(i):
    return pl.ds(pl.multiple_of(i * ROW_TILES, ROW_TILES), ROW_TILES)


def _issue_rows(idx_ref, src_hbm, dst, sem, rows):
    def body(i, c):
        for k in range(DMA_QUEUES):
            r = i * DMA_QUEUES + k
            pltpu.make_async_copy(src_hbm.at[_tile_rows(idx_ref[0, 0, r])], dst.at[_tile_rows(r)],
                                  sem).start(priority=k)
        return c
    lax.fori_loop(0, rows // DMA_QUEUES, body, 0, unroll=4)


def _expert_kernel(bl_ref, bh_ref, nu_ref, idx_ref, idxn_ref, x_hbm,
                   wgl_ref, wul_ref, wdl_ref, wgh_ref, wuh_ref, wdh_ref, o_ref, xbuf, sem):
    j = pl.program_id(0)
    used = nu_ref[0]
    rows = xbuf.shape[1] // ROW_TILES
    slot = j % 2

    @pl.when(j == 0)
    def _():
        _issue_rows(idx_ref, x_hbm, xbuf.at[0], sem.at[0], rows)

    @pl.when(j + 1 < used)
    def _():
        _issue_rows(idxn_ref, x_hbm, xbuf.at[1 - slot], sem.at[1 - slot], rows)

    @pl.when(j < used)
    def _():
        pltpu.make_async_copy(xbuf.at[slot], xbuf.at[slot], sem.at[slot]).wait()
        xb = _from_row_tiles(xbuf.at[slot], 0, rows).astype(BF16)

        def ffn(wg_ref, wu_ref, wd_ref):
            gate = jnp.dot(xb, wg_ref[...].astype(BF16), preferred_element_type=F32)
            up = jnp.dot(xb, wu_ref[...].astype(BF16), preferred_element_type=F32)
            hid = (gate * _sigmoid(gate) * up).astype(BF16)
            return jnp.dot(hid, wd_ref[...].astype(BF16), preferred_element_type=F32)

        _to_row_tiles(o_ref.at[0], ffn(wgl_ref, wul_ref, wdl_ref))
        _to_row_tiles(o_ref.at[1], ffn(wgh_ref, wuh_ref, wdh_ref))

    @pl.when(j >= used)
    def _():
        o_ref[...] = jnp.zeros_like(o_ref)


def _experts(block_lo, block_hi, n_used, buf_tok3, h2, wg, wu, wd):
    nb, _, tmb = buf_tok3.shape
    d = wg.shape[1]
    de = wg.shape[2]
    lo_map = lambda j, bl, bh, nu: (bl[jnp.minimum(j, nu[0] - 1)], 0, 0)
    hi_map = lambda j, bl, bh, nu: (bh[jnp.minimum(j, nu[0] - 1)], 0, 0)
    return pl.pallas_call(
        _expert_kernel,
        grid_spec=pltpu.PrefetchScalarGridSpec(
            num_scalar_prefetch=3,
            grid=(nb,),
            in_specs=[
                pl.BlockSpec((1, 1, tmb), lambda j, bl, bh, nu: (j, 0, 0), memory_space=pltpu.SMEM),
                pl.BlockSpec((1, 1, tmb), lambda j, bl, bh, nu: (jnp.minimum(j + 1, nb - 1), 0, 0),
                             memory_space=pltpu.SMEM),
                pl.BlockSpec(memory_space=pl.ANY),
                pl.BlockSpec((None, d, de), lo_map),
                pl.BlockSpec((None, d, de), lo_map),
                pl.BlockSpec((None, de, d), lo_map),
                pl.BlockSpec((None, d, de), hi_map),
                pl.BlockSpec((None, d, de), hi_map),
                pl.BlockSpec((None, de, d), hi_map),
            ],
            out_specs=pl.BlockSpec((2, tmb * ROW_TILES, LANES), lambda j, bl, bh, nu: (0, j, 0)),
            scratch_shapes=[pltpu.VMEM((2, tmb * ROW_TILES, LANES), F32), pltpu.SemaphoreType.DMA((2,))],
        ),
        out_shape=jax.ShapeDtypeStruct((2, nb * tmb * ROW_TILES, LANES), F32),
        compiler_params=_params("arbitrary"),
        name="experts",
    )(block_lo, block_hi, n_used, buf_tok3, buf_tok3, h2, wg, wu, wd, wg, wu, wd)


def _issue_pairs(idx_ref, yb_hbm, dst, sem, rows):
    def body(i, c):
        for k in range(DMA_QUEUES):
            r = i * DMA_QUEUES + k
            pltpu.make_async_copy(yb_hbm.at[:, _tile_rows(idx_ref[0, 0, r]), :], dst.at[:, _tile_rows(r), :],
                                  sem).start(priority=k)
        return c
    lax.fori_loop(0, rows // DMA_QUEUES, body, 0, unroll=4)


def _final_kernel(idx_ref, idxn_ref, x1_ref, rw_ref, p_ref, yb_hbm, npl_ref, wpg_ref, wp_ref, nfin_ref,
                  o_ref, gbuf, sem, x2_ref):
    i = pl.program_id(0)
    nt = pl.num_programs(0)
    tm = x1_ref.shape[0]
    slot = i % 2

    @pl.when(i == 0)
    def _():
        _issue_pairs(idx_ref, yb_hbm, gbuf.at[0], sem.at[0], tm)

    @pl.when(i + 1 < nt)
    def _():
        _issue_pairs(idxn_ref, yb_hbm, gbuf.at[1 - slot], sem.at[1 - slot], tm)

    pltpu.make_async_copy(gbuf.at[slot], gbuf.at[slot], sem.at[slot]).wait()
    w_lo = jnp.broadcast_to(rw_ref[:, 0:1], (tm, LANES))
    w_hi = jnp.broadcast_to(rw_ref[:, 1:2], (tm, LANES))
    for s in range(ROW_TILES):
        cols = slice(s * LANES, (s + 1) * LANES)
        x2_ref[:, cols] = (x1_ref[:, cols]
                           + w_lo * gbuf[slot, 0, pl.ds(s, tm, stride=ROW_TILES), :]
                           + w_hi * gbuf[slot, 1, pl.ds(s, tm, stride=ROW_TILES), :])
    x2 = x2_ref[...]
    hp = _rms(x2, npl_ref[...]).astype(BF16)
    gate = _sigmoid(jnp.dot(hp, wpg_ref[...], preferred_element_type=F32))
    emb = jnp.dot(p_ref[...].astype(BF16), wp_ref[...], preferred_element_type=F32)
    x3 = x2 + gate * emb
    o_ref[...] = _rms(x3, nfin_ref[...])


def _final(dest3, x1, rw, p2, yb, npl, wpg, wp, nfin):
    n, d = x1.shape
    tm = FINAL_ROWS
    nt = n // tm
    row = lambda i: (i, 0)
    const = lambda i: (0, 0)
    full = lambda a: pl.BlockSpec(a.shape, const)
    return pl.pallas_call(
        _final_kernel,
        grid=(nt,),
        in_specs=[
            pl.BlockSpec((1, 1, tm), lambda i: (i, 0, 0), memory_space=pltpu.SMEM),
            pl.BlockSpec((1, 1, tm), lambda i: (jnp.minimum(i + 1, nt - 1), 0, 0), memory_space=pltpu.SMEM),
            pl.BlockSpec((tm, d), row),
            pl.BlockSpec((tm, ROUTE_COLS), row),
            pl.BlockSpec((tm, p2.shape[1]), row),
            pl.BlockSpec(memory_space=pl.ANY),
            full(npl), full(wpg), full(wp), full(nfin),
        ],
        out_specs=pl.BlockSpec((tm, d), row),
        out_shape=jax.ShapeDtypeStruct((n, d), F32),
        scratch_shapes=[pltpu.VMEM((2, 2, tm * ROW_TILES, LANES), F32), pltpu.SemaphoreType.DMA((2,)),
                        pltpu.VMEM((tm, d), F32)],
        compiler_params=_params("arbitrary"),
        name="final",
    )(dest3, dest3, x1, rw, p2, yb, npl, wpg, wp, nfin)


def _layer(x2, p2, batch, seq, cosf, sinf, norm_mix, w_in, ret_gn, w_br_ret, w_br_sb, w_out, norm_ffn,
           w_grp, b_grp, w_exp, b_exp, w_gate, w_up, w_down, norm_ple, w_ple, w_ple_gate, out_gain):
    n, d = x2.shape
    wv_lo = 2 * 512 + 2 * 512 + 2 * 512
    rq, rk, rv, rg, sq, skp, svt, g_ret, g_sb = _proj(
        x2, norm_mix[None, :], w_in.astype(BF16), w_in[:, wv_lo:wv_lo + 512].T.astype(BF16), cosf, sinf, seq)
    ret = _retention(rq, rk, rv, rg, ret_gn[None, :], batch, seq)

    tk = SB_BLOCK
    nkb = seq // tk
    kperm = skp.reshape(batch, nkb, tk, skp.shape[1])
    vtperm = svt.reshape(batch, nkb, svt.shape[1], tk)
    sb = _stick_breaking(sq, kperm, vtperm, batch, seq)

    wr = jnp.zeros((d, ROUTE_COLS), F32)
    wr = wr.at[:, 0:N_GROUPS].set(w_grp).at[:, EXPERT_COL0:EXPERT_COL0 + N_EXPERTS].set(w_exp)
    br = jnp.zeros((1, ROUTE_COLS), F32)
    br = br.at[0, 0:N_GROUPS].set(b_grp).at[0, EXPERT_COL0:EXPERT_COL0 + N_EXPERTS].set(b_exp)
    x1, h2, ri, rw, cnt = _merge(ret, sb, g_ret, g_sb, x2, w_br_ret.astype(BF16), w_br_sb.astype(BF16),
                                 w_out.astype(BF16), norm_ffn[None, :], wr.astype(BF16), br)

    tmb = EXPERT_ROWS
    counts = cnt[:N_CLASSES, 0].astype(I32)
    padded = (counts + tmb - 1) // tmb * tmb
    pad_end = jnp.cumsum(padded)
    pad_start = pad_end - padded
    cids = jnp.arange(N_CLASSES, dtype=I32)
    dest = jnp.sum(jnp.where(ri[0, :, None] == cids, pad_start, 0), axis=-1) + ri[1]
    nb = n // tmb + N_CLASSES
    buf_tok = jnp.zeros((nb * tmb,), I32).at[dest].set(jnp.arange(n, dtype=I32), unique_indices=True,
                                                        mode="promise_in_bounds")
    block_c = jnp.minimum(
        jnp.sum((pad_end[None, :] <= (jnp.arange(nb, dtype=I32) * tmb)[:, None]).astype(I32), axis=1),
        N_CLASSES - 1)
    pairs = [(lo, hi) for lo in range(EXPERTS_PER_GROUP) for hi in range(lo + 1, EXPERTS_PER_GROUP)]
    class_lo = np.array([g * EXPERTS_PER_GROUP + lo for g in range(N_GROUPS) for lo, _ in pairs], np.int32)
    class_hi = np.array([g * EXPERTS_PER_GROUP + hi for g in range(N_GROUPS) for _, hi in pairs], np.int32)
    block_lo = jnp.sum(jnp.where(block_c[:, None] == cids, class_lo, 0), axis=-1).astype(I32)
    block_hi = jnp.sum(jnp.where(block_c[:, None] == cids, class_hi, 0), axis=-1).astype(I32)
    n_used = (pad_end[-1:] // tmb).astype(I32)
    yb = _experts(block_lo, block_hi, n_used, buf_tok.reshape(nb, 1, tmb), h2, w_gate, w_up, w_down)

    tf = FINAL_ROWS
    return _final(dest.reshape(n // tf, 1, tf), x1, rw, p2, yb, norm_ple[None, :],
                  w_ple_gate.astype(BF16), w_ple.astype(BF16), out_gain)


def kernel(x, p, norm_mix, w_in, ret_gn, w_br_ret, w_br_sb, w_out, norm_ffn, w_grp, b_grp, w_exp, b_exp,
           w_gate, w_up, w_down, norm_ple, w_ple, w_ple_gate, norm_final):
    batch, seq, d = x.shape
    depth = p.shape[0]
    assert depth == 1, "the final RMSNorm is fused into the last layer's kernel"
    pos = jnp.arange(seq, dtype=F32)
    inv_freq = ROPE_BASE ** (-jnp.arange(0, RET_DK, 2, dtype=F32) / RET_DK)
    ang = pos[:, None] * inv_freq[None, :]
    cos, sin = jnp.cos(ang), jnp.sin(ang)
    cosf = jnp.concatenate([cos, cos], axis=1)
    sinf = jnp.concatenate([-sin, sin], axis=1)
    x2 = x.reshape(batch * seq, d)
    i = 0
    out = _layer(x2, p[i].reshape(batch * seq, -1), batch, seq, cosf, sinf, norm_mix[i], w_in[i], ret_gn[i],
                 w_br_ret[i], w_br_sb[i], w_out[i], norm_ffn[i], w_grp[i], b_grp[i], w_exp[i], b_exp[i],
                 w_gate[i], w_up[i], w_down[i], norm_ple[i], w_ple[i], w_ple_gate[i], norm_final[None, :])
    return out.reshape(batch, seq, d)
```

```python
import functools

import numpy as np
import jax
import jax.numpy as jnp
from jax import lax
from jax.experimental import pallas as pl
from jax.experimental.pallas import tpu as pltpu

F32 = jnp.float32
BF16 = jnp.bfloat16
I32 = jnp.int32

EPS = 1e-6
CHUNK = 64
RET_HEADS = 4
RET_DK = 128
SB_HEADS = 8
SB_DH = 64
ROPE_BASE = 10000.0
N_GROUPS = 4
EXPERTS_PER_GROUP = 4
N_EXPERTS = N_GROUPS * EXPERTS_PER_GROUP
PAIRS_PER_GROUP = EXPERTS_PER_GROUP * (EXPERTS_PER_GROUP - 1) // 2
N_CLASSES = N_GROUPS * PAIRS_PER_GROUP
CLASS_ROWS = 32

LANES = 128
SUBLANES = 8
VMEM_LIMIT = 56 * 1024 * 1024

PROJ_ROWS = 512
RET_ROWS = 512
SB_BLOCK = 256
MERGE_ROWS = 512
EXPERT_ROWS = 512
FINAL_ROWS = 512
INVERT_TOKENS = 2048
ROUTE_COLS = 128
ROW_TILES = 8
EXPERT_COL0 = 8
LOG2E = 1.4426950408889634
SB_SKIP_LOG2 = -160.0
SB_CLAMP = 126.0
SB_MASKED = -1e30


def _rms(x, g):
    return x * lax.rsqrt(jnp.mean(x * x, axis=-1, keepdims=True) + EPS) * g


def _sigmoid(x):
    return 1.0 / (1.0 + jnp.exp(-x))


def _to_row_tiles(ref, val):
    rows = val.shape[0]
    for s in range(ROW_TILES):
        ref[pl.ds(s, rows, stride=ROW_TILES), :] = val[:, s * LANES:(s + 1) * LANES]


def _from_row_tiles(ref, first, rows):
    return jnp.concatenate([ref[pl.ds(first * ROW_TILES + s, rows, stride=ROW_TILES), :]
                            for s in range(ROW_TILES)], axis=1)


def _params(*sem):
    return pltpu.CompilerParams(dimension_semantics=sem, vmem_limit_bytes=VMEM_LIMIT)


def _proj_kernel(x_ref, g_ref, w_ref, perm_ref, cos_ref, sin_ref,
                 rq_ref, rk_ref, rv_ref, rg_ref, sq_ref, sk_ref, svt_ref, gr_ref, gs_ref):
    h = _rms(x_ref[...], g_ref[...]).astype(BF16)
    hp = jnp.dot(perm_ref[...], h, preferred_element_type=F32).astype(BF16)
    cos = cos_ref[...]
    sin = sin_ref[...]

    def seg(lo, hi):
        return jnp.dot(h, w_ref[:, lo:hi], preferred_element_type=F32)

    def rope(t):
        outs = []
        for hd in range(RET_HEADS):
            th = t[:, hd * RET_DK:(hd + 1) * RET_DK]
            outs.append(th * cos + pltpu.roll(th, RET_DK // 2, 1) * sin)
        return jnp.concatenate(outs, axis=1)

    rq_ref[...] = rope(seg(0, 512)).astype(BF16)
    rk_ref[...] = (rope(seg(512, 1024)) * (RET_DK ** -0.5)).astype(BF16)
    rv_ref[...] = seg(1024, 1536).astype(BF16)
    rg_ref[...] = seg(1536, 2048).astype(BF16)
    sq_ref[...] = (seg(2048, 2560) * (SB_DH ** -0.5)).astype(BF16)
    sk_ref[...] = jnp.dot(hp, w_ref[:, 2560:3072], preferred_element_type=F32).astype(BF16)
    svt = lax.dot_general(w_ref[:, 3072:3584], hp, (((0,), (1,)), ((), ())), preferred_element_type=F32)
    tk = svt_ref.shape[2]
    for kb in range(svt_ref.shape[0]):
        svt_ref[kb] = svt[:, kb * tk:(kb + 1) * tk].astype(BF16)
    gr_ref[...] = seg(3584, 4608).astype(BF16)
    gs_ref[...] = seg(4608, 5632).astype(BF16)


def _key_block_perm(tm, tk):
    grp = tk // SUBLANES
    new = np.arange(tm)
    blk, r = new // tk, new % tk
    old = blk * tk + (r % SUBLANES) * grp + r // SUBLANES
    p = np.zeros((tm, tm), np.float32)
    p[new, old] = 1.0
    return jnp.asarray(p, BF16)


def _proj(x2, g, w, cosf, sinf, seq):
    n, d = x2.shape
    tm = PROJ_ROWS
    tk = SB_BLOCK
    pos_blocks = seq // tm
    perm = _key_block_perm(tm, tk)
    row = lambda i: (i, 0)
    const = lambda i: (0, 0)
    half = jax.ShapeDtypeStruct((n, 512), BF16)
    full = jax.ShapeDtypeStruct((n, 1024), BF16)
    svt = jax.ShapeDtypeStruct((n // tk, 512, tk), BF16)
    hspec = pl.BlockSpec((tm, 512), row)
    fspec = pl.BlockSpec((tm, 1024), row)
    return pl.pallas_call(
        _proj_kernel,
        grid=(n // tm,),
        in_specs=[
            pl.BlockSpec((tm, d), row),
            pl.BlockSpec((1, d), const),
            pl.BlockSpec(w.shape, const),
            pl.BlockSpec(perm.shape, const),
            pl.BlockSpec((tm, RET_DK), lambda i: (i % pos_blocks, 0)),
            pl.BlockSpec((tm, RET_DK), lambda i: (i % pos_blocks, 0)),
        ],
        out_specs=[hspec] * 6 + [pl.BlockSpec((tm // tk, 512, tk), lambda i: (i, 0, 0)), fspec, fspec],
        out_shape=[half] * 6 + [svt, full, full],
        compiler_params=_params("parallel"),
        name="proj",
    )(x2, g, w, perm, cosf, sinf)


def _ret_consts(tb):
    hd = np.arange(RET_HEADS, dtype=np.float64)
    lg = np.log(1.0 - np.exp2(-5.0 - hd))
    idx = np.arange(tb, dtype=np.float64)
    dist = np.abs(idx[:, None] - idx[None, :])
    chunk = (np.arange(tb) // CHUNK)
    allowed = chunk[None, :] <= chunk[:, None]
    dm = np.exp(lg[:, None, None] * dist) * allowed
    qdec = np.exp(lg[:, None] * (idx + 1.0))[:, :, None] * np.ones((1, 1, RET_DK))
    kdec = np.exp(lg[:, None] * (tb - 1.0 - idx))[:, :, None] * np.ones((1, 1, RET_DK))
    cdec = [float(v) for v in np.exp(lg * tb)]
    return (jnp.asarray(dm, F32), jnp.asarray(qdec, F32), jnp.asarray(kdec, F32), cdec)


def _ret_kernel(cdec, q_ref, k_ref, v_ref, g_ref, dm_ref, qd_ref, kd_ref, gn_ref, o_ref, st_ref):
    @pl.when(pl.program_id(1) == 0)
    def _():
        st_ref[...] = jnp.zeros_like(st_ref)

    for hd in range(RET_HEADS):
        sl = slice(hd * RET_DK, (hd + 1) * RET_DK)
        q = q_ref[:, sl]
        k = k_ref[:, sl]
        v = v_ref[:, sl]
        st = st_ref[hd]
        scores = lax.dot_general(q, k, (((1,), (1,)), ((), ())), preferred_element_type=F32)
        scores = scores * dm_ref[hd]
        qd = (q.astype(F32) * qd_ref[hd]).astype(BF16)
        o = (jnp.dot(scores.astype(BF16), v, preferred_element_type=F32)
             + jnp.dot(qd, st.astype(BF16), preferred_element_type=F32))
        kd = (k.astype(F32) * kd_ref[hd]).astype(BF16)
        st_ref[hd] = cdec[hd] * st + lax.dot_general(
            kd, v, (((0,), (0,)), ((), ())), preferred_element_type=F32)
        mu = jnp.mean(o, axis=-1, keepdims=True)
        oc = o - mu
        var = jnp.mean(oc * oc, axis=-1, keepdims=True)
        on = oc * lax.rsqrt(var + EPS) * gn_ref[:, sl]
        gate = g_ref[:, sl].astype(F32)
        o_ref[:, sl] = (gate * _sigmoid(gate) * on).astype(BF16)


def _retention(rq, rk, rv, rg, gn, batch, seq):
    n, w = rq.shape
    tb = RET_ROWS
    nc = seq // tb
    dm, qdec, kdec, cdec = _ret_consts(tb)
    blk = pl.BlockSpec((tb, w), lambda b, c: (b * nc + c, 0))
    c3 = lambda b, c: (0, 0, 0)
    return pl.pallas_call(
        functools.partial(_ret_kernel, cdec),
        grid=(batch, nc),
        in_specs=[blk, blk, blk, blk,
                  pl.BlockSpec(dm.shape, c3), pl.BlockSpec(qdec.shape, c3), pl.BlockSpec(kdec.shape, c3),
                  pl.BlockSpec((1, w), lambda b, c: (0, 0))],
        out_specs=blk,
        out_shape=jax.ShapeDtypeStruct((n, w), BF16),
        scratch_shapes=[pltpu.VMEM((RET_HEADS, RET_DK, RET_DK), F32)],
        compiler_params=_params("parallel", "arbitrary"),
        name="retention",
    )(rq, rk, rv, rg, dm, qdec, kdec, gn)


def _sb_kernel(q_ref, k_ref, vt_ref, o_ref, acc_ref):
    tq = q_ref.shape[0]
    tk = tq
    grp = tk // SUBLANES
    heads = q_ref.shape[1] // SB_DH
    wide = heads * tq
    assert tq & (tq - 1) == 0
    qi = pl.program_id(1)
    per_group = LANES // SB_DH
    lane = lax.broadcasted_iota(I32, (tq, LANES), 1)
    q_groups = []
    for gi in range(heads // per_group):
        qg = q_ref[:, gi * LANES:(gi + 1) * LANES]
        q_groups.append(jnp.concatenate(
            [jnp.where((lane >= hh * SB_DH) & (lane < (hh + 1) * SB_DH), qg, jnp.zeros_like(qg))
             for hh in range(per_group)], axis=0))
    shp3 = (grp, SUBLANES, wide)
    key_pos = lax.broadcasted_iota(I32, shp3, 1) * grp + lax.broadcasted_iota(I32, shp3, 0)
    causal = key_pos < (lax.broadcasted_iota(I32, shp3, 2) & (tq - 1))
    row8 = lax.broadcasted_iota(I32, (SUBLANES, wide), 0)

    def block(kb, carry, diagonal):
        z = jnp.concatenate(
            [lax.dot_general(k_ref[kb, :, gi * LANES:(gi + 1) * LANES], qg, (((1,), (1,)), ((), ())),
                             preferred_element_type=F32) for gi, qg in enumerate(q_groups)],
            axis=1) * LOG2E
        z3 = z.reshape(shp3)
        if diagonal:
            z3 = jnp.where(causal, z3, SB_MASKED)
        run = jnp.ones((SUBLANES, wide), F32)
        local = [None] * grp
        for g in reversed(range(grp)):
            ez = jnp.exp2(jnp.minimum(z3[g], SB_CLAMP))
            run = run * (1.0 / (1.0 + ez))
            local[g] = ez * run
        log_tot = jnp.log2(run)
        off = jnp.where(row8 + 1 < SUBLANES, pltpu.roll(log_tot, SUBLANES - 1, 0), 0.0)
        for d in (1, 2, 4):
            off = off + jnp.where(row8 + d < SUBLANES, pltpu.roll(off, SUBLANES - d, 0), 0.0)
        scale = jnp.exp2(off + carry)
        a3 = jnp.stack(local, axis=0) * scale[None]
        a = a3.reshape(tk, wide).astype(BF16)
        for hh in range(heads):
            rows = slice(hh * SB_DH, (hh + 1) * SB_DH)
            contrib = jnp.dot(vt_ref[kb, rows, :], a[:, hh * tq:(hh + 1) * tq], preferred_element_type=F32)
            if diagonal:
                acc_ref[rows, :] = contrib
            else:
                acc_ref[rows, :] += contrib
        return carry + jnp.broadcast_to(off[0:1, :] + log_tot[0:1, :], carry.shape)

    carry = block(qi, jnp.zeros((SUBLANES, wide), F32), True)

    def cond(c):
        return (c[0] < qi) & (c[1] > 0)

    def body(c):
        new = block(qi - 1 - c[0], c[2], False)
        go = (jnp.max(new) > SB_SKIP_LOG2).astype(I32)
        return (c[0] + 1, go, new)

    lax.while_loop(cond, body, (jnp.int32(0), jnp.int32(1), carry))
    o_ref[...] = acc_ref[...].T.astype(BF16)


def _stick_breaking(sq, kperm, vtperm, batch, seq):
    n, w = sq.shape
    tq = SB_BLOCK
    nq = seq // tq
    return pl.pallas_call(
        _sb_kernel,
        grid=(batch, nq),
        in_specs=[
            pl.BlockSpec((tq, w), lambda b, i: (b * nq + i, 0)),
            pl.BlockSpec((None, nq, tq, w), lambda b, i: (b, 0, 0, 0)),
            pl.BlockSpec((None, nq, w, tq), lambda b, i: (b, 0, 0, 0)),
        ],
        out_specs=pl.BlockSpec((tq, w), lambda b, i: (b * nq + i, 0)),
        out_shape=jax.ShapeDtypeStruct((n, w), BF16),
        scratch_shapes=[pltpu.VMEM((w, tq), F32)],
        compiler_params=_params("parallel", "arbitrary"),
        name="stickbrk",
    )(sq, kperm, vtperm)


def _merge_kernel(ret_ref, sb_ref, gr_ref, gs_ref, x_ref, wbr_ref, wbs_ref, wo_ref, nf_ref,
                  wr_ref, br_ref, tri_ref,
                  x1_ref, h2_ref, ri_ref, rw_ref, cnt_ref, carry_ref):
    @pl.when(pl.program_id(0) == 0)
    def _():
        carry_ref[...] = jnp.zeros_like(carry_ref)

    tm = x_ref.shape[0]
    merged = (_sigmoid(gr_ref[...].astype(F32)) * jnp.dot(ret_ref[...], wbr_ref[...], preferred_element_type=F32)
              + _sigmoid(gs_ref[...].astype(F32)) * jnp.dot(sb_ref[...], wbs_ref[...], preferred_element_type=F32))
    x1 = x_ref[...] + jnp.dot(merged.astype(BF16), wo_ref[...], preferred_element_type=F32)
    x1_ref[...] = x1
    h2 = _rms(x1, nf_ref[...])
    _to_row_tiles(h2_ref, h2)
    logits = jnp.dot(h2.astype(BF16), wr_ref[...], preferred_element_type=F32) + br_ref[...]
    lt = logits.T
    neg = jnp.float32(-jnp.inf)

    grow = lax.broadcasted_iota(I32, (SUBLANES, tm), 0)
    gl = jnp.where(grow < N_GROUPS, lt[0:SUBLANES], neg)
    gmax = jnp.max(gl, axis=0, keepdims=True)
    g_idx = jnp.min(jnp.where(gl == gmax, grow, N_GROUPS), axis=0, keepdims=True)
    p_grp = 1.0 / jnp.sum(jnp.exp(gl - gmax), axis=0, keepdims=True)

    erow = lax.broadcasted_iota(I32, (N_EXPERTS, tm), 0)
    el = lt[EXPERT_COL0:EXPERT_COL0 + N_EXPERTS]
    el = jnp.where((erow >= g_idx * EXPERTS_PER_GROUP) & (erow < (g_idx + 1) * EXPERTS_PER_GROUP), el, neg)
    m1 = jnp.max(el, axis=0, keepdims=True)
    i1 = jnp.min(jnp.where(el == m1, erow, N_EXPERTS), axis=0, keepdims=True)
    el2 = jnp.where(erow == i1, neg, el)
    m2 = jnp.max(el2, axis=0, keepdims=True)
    i2 = jnp.min(jnp.where(el2 == m2, erow, N_EXPERTS), axis=0, keepdims=True)
    e2 = jnp.exp(m2 - m1)
    w1 = p_grp / (1.0 + e2)
    w2 = p_grp * e2 / (1.0 + e2)

    l1 = i1 - g_idx * EXPERTS_PER_GROUP
    l2 = i2 - g_idx * EXPERTS_PER_GROUP
    lo = jnp.minimum(l1, l2)
    hi = jnp.maximum(l1, l2)
    cls = g_idx * PAIRS_PER_GROUP + ((lo * (2 * EXPERTS_PER_GROUP - 1 - lo)) >> 1) + (hi - lo - 1)
    w_lo = jnp.where(l1 < l2, w1, w2)
    w_hi = jnp.where(l1 < l2, w2, w1)

    crow = lax.broadcasted_iota(I32, (CLASS_ROWS, tm), 0)
    sel = crow == cls
    onehot = sel.astype(BF16)
    cinc = jnp.dot(onehot, tri_ref[...], preferred_element_type=F32)
    carry = carry_ref[...]
    rank = jnp.sum(jnp.where(sel, carry[:, 0:1] + cinc - 1.0, 0.0), axis=0, keepdims=True)
    carry_new = carry + jnp.sum(onehot.astype(F32), axis=1, keepdims=True)
    carry_ref[...] = carry_new
    cnt_ref[...] = carry_new

    ri_ref[...] = jnp.zeros_like(ri_ref)
    ri_ref[0:1, :] = cls
    ri_ref[1:2, :] = rank.astype(I32)
    wrow = lax.broadcasted_iota(I32, (ROUTE_COLS, tm), 0)
    wt = jnp.where(wrow == 0, w_lo, jnp.where(wrow == 1, w_hi, 0.0))
    rw_ref[...] = wt.T


def _merge(ret, sb, g_ret, g_sb, x2, wbr, wbs, wo, nf, wr, br):
    n, d = x2.shape
    tm = MERGE_ROWS
    tri = jnp.asarray(np.triu(np.ones((tm, tm), np.float32)), BF16)
    row = lambda i: (i, 0)
    const = lambda i: (0, 0)
    full = lambda a: pl.BlockSpec(a.shape, const)
    return pl.pallas_call(
        _merge_kernel,
        grid=(n // tm,),
        in_specs=[pl.BlockSpec((tm, 512), row), pl.BlockSpec((tm, 512), row),
                  pl.BlockSpec((tm, d), row), pl.BlockSpec((tm, d), row), pl.BlockSpec((tm, d), row),
                  full(wbr), full(wbs), full(wo), full(nf), full(wr), full(br), full(tri)],
        out_specs=[pl.BlockSpec((tm, d), row), pl.BlockSpec((tm * ROW_TILES, LANES), row),
                   pl.BlockSpec((SUBLANES, tm), lambda i: (0, i)),
                   pl.BlockSpec((tm, ROUTE_COLS), row),
                   pl.BlockSpec((CLASS_ROWS, LANES), const)],
        out_shape=[jax.ShapeDtypeStruct((n, d), F32), jax.ShapeDtypeStruct((n * ROW_TILES, LANES), F32),
                   jax.ShapeDtypeStruct((SUBLANES, n), I32),
                   jax.ShapeDtypeStruct((n, ROUTE_COLS), F32),
                   jax.ShapeDtypeStruct((CLASS_ROWS, LANES), F32)],
        scratch_shapes=[pltpu.VMEM((CLASS_ROWS, LANES), F32)],
        compiler_params=_params("arbitrary"),
        name="merge",
    )(ret, sb, g_ret, g_sb, x2, wbr, wbs, wo, nf, wr, br, tri)


def _invert_kernel(fill_ref, end_ref, dest_ref, out_ref):
    i = pl.program_id(0)
    per_step = dest_ref.shape[2]

    @pl.when(i == 0)
    def _():
        def clear(s, c):
            out_ref[s] = 0
            return c
        for cl in range(fill_ref.shape[0]):
            lax.fori_loop(fill_ref[cl], end_ref[cl], clear, 0)

    def place(k, c):
        out_ref[dest_ref[0, 0, k]] = i * per_step + k
        return c
    lax.fori_loop(0, per_step, place, 0, unroll=8)


def _invert(dest, fill, end, slots):
    n = dest.shape[0]
    per_step = min(INVERT_TOKENS, n)
    return pl.pallas_call(
        _invert_kernel,
        grid_spec=pltpu.PrefetchScalarGridSpec(
            num_scalar_prefetch=2,
            grid=(n // per_step,),
            in_specs=[pl.BlockSpec((1, 1, per_step), lambda i, f, e: (i, 0, 0), memory_space=pltpu.SMEM)],
            out_specs=pl.BlockSpec(memory_space=pltpu.SMEM),
        ),
        out_shape=jax.ShapeDtypeStruct((slots,), I32),
        compiler_params=_params("arbitrary"),
        name="invert",
    )(fill, end, dest.reshape(n // per_step, 1, per_step))


DMA_QUEUES = 2


def _tile_rows(i):
    return pl.ds(pl.multiple_of(i * ROW_TILES, ROW_TILES), ROW_TILES)


def _issue_rows(idx_ref, src_hbm, dst, sem, rows):
    def body(i, c):
        for k in range(DMA_QUEUES):
            r = i * DMA_QUEUES + k
            pltpu.make_async_copy(src_hbm.at[_tile_rows(idx_ref[0, 0, r])], dst.at[_tile_rows(r)],
                                  sem).start(priority=k)
        return c
    lax.fori_loop(0, rows // DMA_QUEUES, body, 0, unroll=4)


def _expert_kernel(bl_ref, bh_ref, nu_ref, idx_ref, idxn_ref, x_hbm,
                   wgl_ref, wul_ref, wdl_ref, wgh_ref, wuh_ref, wdh_ref, o_ref, xbuf, sem):
    j = pl.program_id(0)
    used = nu_ref[0]
    rows = xbuf.shape[1] // ROW_TILES
    slot = j % 2

    @pl.when(j == 0)
    def _():
        _issue_rows(idx_ref, x_hbm, xbuf.at[0], sem.at[0], rows)

    @pl.when(j + 1 < used)
    def _():
        _issue_rows(idxn_ref, x_hbm, xbuf.at[1 - slot], sem.at[1 - slot], rows)

    @pl.when(j < used)
    def _():
        pltpu.make_async_copy(xbuf.at[slot], xbuf.at[slot], sem.at[slot]).wait()
        xb = _from_row_tiles(xbuf.at[slot], 0, rows).astype(BF16)

        def ffn(wg_ref, wu_ref, wd_ref):
            gate = jnp.dot(xb, wg_ref[...].astype(BF16), preferred_element_type=F32)
            up = jnp.dot(xb, wu_ref[...].astype(BF16), preferred_element_type=F32)
            hid = (gate * _sigmoid(gate) * up).astype(BF16)
            return jnp.dot(hid, wd_ref[...].astype(BF16), preferred_element_type=F32)

        _to_row_tiles(o_ref.at[0], ffn(wgl_ref, wul_ref, wdl_ref))
        _to_row_tiles(o_ref.at[1], ffn(wgh_ref, wuh_ref, wdh_ref))

    @pl.when(j >= used)
    def _():
        o_ref[...] = jnp.zeros_like(o_ref)


def _experts(block_lo, block_hi, n_used, buf_tok3, h2, wg, wu, wd):
    nb, _, tmb = buf_tok3.shape
    d = wg.shape[1]
    de = wg.shape[2]
    lo_map = lambda j, bl, bh, nu: (bl[jnp.minimum(j, nu[0] - 1)], 0, 0)
    hi_map = lambda j, bl, bh, nu: (bh[jnp.minimum(j, nu[0] - 1)], 0, 0)
    return pl.pallas_call(
        _expert_kernel,
        grid_spec=pltpu.PrefetchScalarGridSpec(
            num_scalar_prefetch=3,
            grid=(nb,),
            in_specs=[
                pl.BlockSpec((1, 1, tmb), lambda j, bl, bh, nu: (j, 0, 0), memory_space=pltpu.SMEM),
                pl.BlockSpec((1, 1, tmb), lambda j, bl, bh, nu: (jnp.minimum(j + 1, nb - 1), 0, 0),
                             memory_space=pltpu.SMEM),
                pl.BlockSpec(memory_space=pl.ANY),
                pl.BlockSpec((None, d, de), lo_map),
                pl.BlockSpec((None, d, de), lo_map),
                pl.BlockSpec((None, de, d), lo_map),
                pl.BlockSpec((None, d, de), hi_map),
                pl.BlockSpec((None, d, de), hi_map),
                pl.BlockSpec((None, de, d), hi_map),
            ],
            out_specs=pl.BlockSpec((2, tmb * ROW_TILES, LANES), lambda j, bl, bh, nu: (0, j, 0)),
            scratch_shapes=[pltpu.VMEM((2, tmb * ROW_TILES, LANES), F32), pltpu.SemaphoreType.DMA((2,))],
        ),
        out_shape=jax.ShapeDtypeStruct((2, nb * tmb * ROW_TILES, LANES), F32),
        compiler_params=_params("arbitrary"),
        name="experts",
    )(block_lo, block_hi, n_used, buf_tok3, buf_tok3, h2, wg, wu, wd, wg, wu, wd)


def _issue_pairs(idx_ref, yb_hbm, dst, sem, rows):
    def body(i, c):
        for k in range(DMA_QUEUES):
            r = i * DMA_QUEUES + k
            pltpu.make_async_copy(yb_hbm.at[:, _tile_rows(idx_ref[0, 0, r]), :], dst.at[:, _tile_rows(r), :],
                                  sem).start(priority=k)
        return c
    lax.fori_loop(0, rows // DMA_QUEUES, body, 0, unroll=4)


def _final_kernel(idx_ref, idxn_ref, x1_ref, rw_ref, p_ref, yb_hbm, npl_ref, wpg_ref, wp_ref, nfin_ref,
                  o_ref, gbuf, sem, x2_ref):
    i = pl.program_id(0)
    nt = pl.num_programs(0)
    tm = x1_ref.shape[0]
    slot = i % 2

    @pl.when(i == 0)
    def _():
        _issue_pairs(idx_ref, yb_hbm, gbuf.at[0], sem.at[0], tm)

    @pl.when(i + 1 < nt)
    def _():
        _issue_pairs(idxn_ref, yb_hbm, gbuf.at[1 - slot], sem.at[1 - slot], tm)

    pltpu.make_async_copy(gbuf.at[slot], gbuf.at[slot], sem.at[slot]).wait()
    w_lo = jnp.broadcast_to(rw_ref[:, 0:1], (tm, LANES))
    w_hi = jnp.broadcast_to(rw_ref[:, 1:2], (tm, LANES))
    for s in range(ROW_TILES):
        cols = slice(s * LANES, (s + 1) * LANES)
        x2_ref[:, cols] = (x1_ref[:, cols]
                           + w_lo * gbuf[slot, 0, pl.ds(s, tm, stride=ROW_TILES), :]
                           + w_hi * gbuf[slot, 1, pl.ds(s, tm, stride=ROW_TILES), :])
    x2 = x2_ref[...]
    hp = _rms(x2, npl_ref[...]).astype(BF16)
    gate = _sigmoid(jnp.dot(hp, wpg_ref[...], preferred_element_type=F32))
    emb = jnp.dot(p_ref[...].astype(BF16), wp_ref[...], preferred_element_type=F32)
    x3 = x2 + gate * emb
    o_ref[...] = _rms(x3, nfin_ref[...])


def _final(dest3, x1, rw, p2, yb, npl, wpg, wp, nfin):
    n, d = x1.shape
    tm = FINAL_ROWS
    nt = n // tm
    row = lambda i: (i, 0)
    const = lambda i: (0, 0)
    full = lambda a: pl.BlockSpec(a.shape, const)
    return pl.pallas_call(
        _final_kernel,
        grid=(nt,),
        in_specs=[
            pl.BlockSpec((1, 1, tm), lambda i: (i, 0, 0), memory_space=pltpu.SMEM),
            pl.BlockSpec((1, 1, tm), lambda i: (jnp.minimum(i + 1, nt - 1), 0, 0), memory_space=pltpu.SMEM),
            pl.BlockSpec((tm, d), row),
            pl.BlockSpec((tm, ROUTE_COLS), row),
            pl.BlockSpec((tm, p2.shape[1]), row),
            pl.BlockSpec(memory_space=pl.ANY),
            full(npl), full(wpg), full(wp), full(nfin),
        ],
        out_specs=pl.BlockSpec((tm, d), row),
        out_shape=jax.ShapeDtypeStruct((n, d), F32),
        scratch_shapes=[pltpu.VMEM((2, 2, tm * ROW_TILES, LANES), F32), pltpu.SemaphoreType.DMA((2,)),
                        pltpu.VMEM((tm, d), F32)],
        compiler_params=_params("arbitrary"),
        name="final",
    )(dest3, dest3, x1, rw, p2, yb, npl, wpg, wp, nfin)


def _layer(x2, p2, batch, seq, cosf, sinf, norm_mix, w_in, ret_gn, w_br_ret, w_br_sb, w_out, norm_ffn,
           w_grp, b_grp, w_exp, b_exp, w_gate, w_up, w_down, norm_ple, w_ple, w_ple_gate, out_gain):
    n, d = x2.shape
    rq, rk, rv, rg, sq, skp, svt, g_ret, g_sb = _proj(x2, norm_mix[None, :], w_in.astype(BF16), cosf, sinf, seq)
    ret = _retention(rq, rk, rv, rg, ret_gn[None, :], batch, seq)

    tk = SB_BLOCK
    nkb = seq // tk
    kperm = skp.reshape(batch, nkb, tk, skp.shape[1])
    vtperm = svt.reshape(batch, nkb, svt.shape[1], tk)
    sb = _stick_breaking(sq, kperm, vtperm, batch, seq)

    wr = jnp.zeros((d, ROUTE_COLS), F32)
    wr = wr.at[:, 0:N_GROUPS].set(w_grp).at[:, EXPERT_COL0:EXPERT_COL0 + N_EXPERTS].set(w_exp)
    br = jnp.zeros((1, ROUTE_COLS), F32)
    br = br.at[0, 0:N_GROUPS].set(b_grp).at[0, EXPERT_COL0:EXPERT_COL0 + N_EXPERTS].set(b_exp)
    x1, h2, ri, rw, cnt = _merge(ret, sb, g_ret, g_sb, x2, w_br_ret.astype(BF16), w_br_sb.astype(BF16),
                                 w_out.astype(BF16), norm_ffn[None, :], wr.astype(BF16), br)

    tmb = EXPERT_ROWS
    counts = cnt[:N_CLASSES, 0].astype(I32)
    padded = (counts + tmb - 1) // tmb * tmb
    pad_end = jnp.cumsum(padded)
    pad_start = pad_end - padded
    cids = jnp.arange(N_CLASSES, dtype=I32)
    dest = jnp.sum(jnp.where(ri[0, :, None] == cids, pad_start, 0), axis=-1) + ri[1]
    nb = n // tmb + N_CLASSES
    slots = jnp.full((1,), nb * tmb, I32)
    buf_tok = _invert(dest, jnp.concatenate([pad_start + counts, pad_end[-1:]]),
                      jnp.concatenate([pad_end, slots]), nb * tmb)
    block_c = jnp.minimum(
        jnp.sum((pad_end[None, :] <= (jnp.arange(nb, dtype=I32) * tmb)[:, None]).astype(I32), axis=1),
        N_CLASSES - 1)
    pairs = [(lo, hi) for lo in range(EXPERTS_PER_GROUP) for hi in range(lo + 1, EXPERTS_PER_GROUP)]
    class_lo = np.array([g * EXPERTS_PER_GROUP + lo for g in range(N_GROUPS) for lo, _ in pairs], np.int32)
    class_hi = np.array([g * EXPERTS_PER_GROUP + hi for g in range(N_GROUPS) for _, hi in pairs], np.int32)
    block_lo = jnp.sum(jnp.where(block_c[:, None] == cids, class_lo, 0), axis=-1).astype(I32)
    block_hi = jnp.sum(jnp.where(block_c[:, None] == cids, class_hi, 0), axis=-1).astype(I32)
    n_used = (pad_end[-1:] // tmb).astype(I32)
    yb = _experts(block_lo, block_hi, n_used, buf_tok.reshape(nb, 1, tmb), h2, w_gate, w_up, w_down)

    tf = FINAL_ROWS
    return _final(dest.reshape(n // tf, 1, tf), x1, rw, p2, yb, norm_ple[None, :],
                  w_ple_gate.astype(BF16), w_ple.astype(BF16), out_gain)


def kernel(x, p, norm_mix, w_in, ret_gn, w_br_ret, w_br_sb, w_out, norm_ffn, w_grp, b_grp, w_exp, b_exp,
           w_gate, w_up, w_down, norm_ple, w_ple, w_ple_gate, norm_final):
    batch, seq, d = x.shape
    depth = p.shape[0]
    assert depth == 1, "the final RMSNorm is fused into the last layer's kernel"
    pos = jnp.arange(seq, dtype=F32)
    inv_freq = ROPE_BASE ** (-jnp.arange(0, RET_DK, 2, dtype=F32) / RET_DK)
    ang = pos[:, None] * inv_freq[None, :]
    cos, sin = jnp.cos(ang), jnp.sin(ang)
    cosf = jnp.concatenate([cos, cos], axis=1)
    sinf = jnp.concatenate([-sin, sin], axis=1)
    x2 = x.reshape(batch * seq, d)
    i = 0
    out = _layer(x2, p[i].reshape(batch * seq, -1), batch, seq, cosf, sinf, norm_mix[i], w_in[i], ret_gn[i],
                 w_br_ret[i], w_br_sb[i], w_out[i], norm_ffn[i], w_grp[i], b_grp[i], w_exp[i], b_exp[i],
                 w_gate[i], w_up[i], w_down[i], norm_ple[i], w_ple[i], w_ple_gate[i], norm_final[None, :])
    return out.reshape(batch, seq, d)
```

```python
import functools

import numpy as np
import jax
import jax.numpy as jnp
from jax import lax
from jax.experimental import pallas as pl
from jax.experimental.pallas import tpu as pltpu

F32 = jnp.float32
BF16 = jnp.bfloat16
I32 = jnp.int32

EPS = 1e-6
CHUNK = 64
RET_HEADS = 4
RET_DK = 128
SB_HEADS = 8
SB_DH = 64
ROPE_BASE = 10000.0
N_GROUPS = 4
EXPERTS_PER_GROUP = 4
N_EXPERTS = N_GROUPS * EXPERTS_PER_GROUP
PAIRS_PER_GROUP = EXPERTS_PER_GROUP * (EXPERTS_PER_GROUP - 1) // 2
N_CLASSES = N_GROUPS * PAIRS_PER_GROUP
CLASS_ROWS = 32

LANES = 128
SUBLANES = 8
VMEM_LIMIT = 56 * 1024 * 1024

PROJ_ROWS = 512
RET_ROWS = 512
SB_BLOCK = 256
MERGE_ROWS = 512
EXPERT_ROWS = 512
FINAL_ROWS = 512
DISPATCH_TOKENS = 1024
ROUTE_COLS = 128
ROW_TILES = 8
EXPERT_COL0 = 8
LOG2E = 1.4426950408889634
SB_SKIP_LOG2 = -160.0
SB_CLAMP = 126.0
SB_MASKED = -1e30


def _rms(x, g):
    return x * lax.rsqrt(jnp.mean(x * x, axis=-1, keepdims=True) + EPS) * g


def _sigmoid(x):
    return 1.0 / (1.0 + jnp.exp(-x))


def _to_row_tiles(ref, val):
    rows = val.shape[0]
    for s in range(ROW_TILES):
        ref[pl.ds(s, rows, stride=ROW_TILES), :] = val[:, s * LANES:(s + 1) * LANES]


def _from_row_tiles(ref, first, rows):
    return jnp.concatenate([ref[pl.ds(first * ROW_TILES + s, rows, stride=ROW_TILES), :]
                            for s in range(ROW_TILES)], axis=1)


def _params(*sem):
    return pltpu.CompilerParams(dimension_semantics=sem, vmem_limit_bytes=VMEM_LIMIT)


def _proj_kernel(x_ref, g_ref, w_ref, perm_ref, cos_ref, sin_ref,
                 rq_ref, rk_ref, rv_ref, rg_ref, sq_ref, sk_ref, svt_ref, gr_ref, gs_ref):
    h = _rms(x_ref[...], g_ref[...]).astype(BF16)
    hp = jnp.dot(perm_ref[...], h, preferred_element_type=F32).astype(BF16)
    cos = cos_ref[...]
    sin = sin_ref[...]

    def seg(lo, hi):
        return jnp.dot(h, w_ref[:, lo:hi], preferred_element_type=F32)

    def rope(t):
        outs = []
        for hd in range(RET_HEADS):
            th = t[:, hd * RET_DK:(hd + 1) * RET_DK]
            outs.append(th * cos + pltpu.roll(th, RET_DK // 2, 1) * sin)
        return jnp.concatenate(outs, axis=1)

    rq_ref[...] = rope(seg(0, 512)).astype(BF16)
    rk_ref[...] = (rope(seg(512, 1024)) * (RET_DK ** -0.5)).astype(BF16)
    rv_ref[...] = seg(1024, 1536).astype(BF16)
    rg_ref[...] = seg(1536, 2048).astype(BF16)
    sq_ref[...] = (seg(2048, 2560) * (SB_DH ** -0.5)).astype(BF16)
    sk_ref[...] = jnp.dot(hp, w_ref[:, 2560:3072], preferred_element_type=F32).astype(BF16)
    svt = lax.dot_general(w_ref[:, 3072:3584], hp, (((0,), (1,)), ((), ())), preferred_element_type=F32)
    tk = svt_ref.shape[2]
    for kb in range(svt_ref.shape[0]):
        svt_ref[kb] = svt[:, kb * tk:(kb + 1) * tk].astype(BF16)
    gr_ref[...] = seg(3584, 4608).astype(BF16)
    gs_ref[...] = seg(4608, 5632).astype(BF16)


def _key_block_perm(tm, tk):
    grp = tk // SUBLANES
    new = np.arange(tm)
    blk, r = new // tk, new % tk
    old = blk * tk + (r % SUBLANES) * grp + r // SUBLANES
    p = np.zeros((tm, tm), np.float32)
    p[new, old] = 1.0
    return jnp.asarray(p, BF16)


def _proj(x2, g, w, cosf, sinf, seq):
    n, d = x2.shape
    tm = PROJ_ROWS
    tk = SB_BLOCK
    pos_blocks = seq // tm
    perm = _key_block_perm(tm, tk)
    row = lambda i: (i, 0)
    const = lambda i: (0, 0)
    half = jax.ShapeDtypeStruct((n, 512), BF16)
    full = jax.ShapeDtypeStruct((n, 1024), BF16)
    svt = jax.ShapeDtypeStruct((n // tk, 512, tk), BF16)
    hspec = pl.BlockSpec((tm, 512), row)
    fspec = pl.BlockSpec((tm, 1024), row)
    return pl.pallas_call(
        _proj_kernel,
        grid=(n // tm,),
        in_specs=[
            pl.BlockSpec((tm, d), row),
            pl.BlockSpec((1, d), const),
            pl.BlockSpec(w.shape, const),
            pl.BlockSpec(perm.shape, const),
            pl.BlockSpec((tm, RET_DK), lambda i: (i % pos_blocks, 0)),
            pl.BlockSpec((tm, RET_DK), lambda i: (i % pos_blocks, 0)),
        ],
        out_specs=[hspec] * 6 + [pl.BlockSpec((tm // tk, 512, tk), lambda i: (i, 0, 0)), fspec, fspec],
        out_shape=[half] * 6 + [svt, full, full],
        compiler_params=_params("parallel"),
        name="proj",
    )(x2, g, w, perm, cosf, sinf)


def _ret_consts(tb):
    hd = np.arange(RET_HEADS, dtype=np.float64)
    lg = np.log(1.0 - np.exp2(-5.0 - hd))
    idx = np.arange(tb, dtype=np.float64)
    dist = np.abs(idx[:, None] - idx[None, :])
    chunk = (np.arange(tb) // CHUNK)
    allowed = chunk[None, :] <= chunk[:, None]
    dm = np.exp(lg[:, None, None] * dist) * allowed
    qdec = np.exp(lg[:, None] * (idx + 1.0))[:, :, None] * np.ones((1, 1, RET_DK))
    kdec = np.exp(lg[:, None] * (tb - 1.0 - idx))[:, :, None] * np.ones((1, 1, RET_DK))
    cdec = [float(v) for v in np.exp(lg * tb)]
    return (jnp.asarray(dm, F32), jnp.asarray(qdec, F32), jnp.asarray(kdec, F32), cdec)


def _ret_kernel(cdec, q_ref, k_ref, v_ref, g_ref, dm_ref, qd_ref, kd_ref, gn_ref, o_ref, st_ref):
    @pl.when(pl.program_id(1) == 0)
    def _():
        st_ref[...] = jnp.zeros_like(st_ref)

    for hd in range(RET_HEADS):
        sl = slice(hd * RET_DK, (hd + 1) * RET_DK)
        q = q_ref[:, sl]
        k = k_ref[:, sl]
        v = v_ref[:, sl]
        st = st_ref[hd]
        scores = lax.dot_general(q, k, (((1,), (1,)), ((), ())), preferred_element_type=F32)
        scores = scores * dm_ref[hd]
        qd = (q.astype(F32) * qd_ref[hd]).astype(BF16)
        o = (jnp.dot(scores.astype(BF16), v, preferred_element_type=F32)
             + jnp.dot(qd, st.astype(BF16), preferred_element_type=F32))
        kd = (k.astype(F32) * kd_ref[hd]).astype(BF16)
        st_ref[hd] = cdec[hd] * st + lax.dot_general(
            kd, v, (((0,), (0,)), ((), ())), preferred_element_type=F32)
        mu = jnp.mean(o, axis=-1, keepdims=True)
        oc = o - mu
        var = jnp.mean(oc * oc, axis=-1, keepdims=True)
        on = oc * lax.rsqrt(var + EPS) * gn_ref[:, sl]
        gate = g_ref[:, sl].astype(F32)
        o_ref[:, sl] = (gate * _sigmoid(gate) * on).astype(BF16)


def _retention(rq, rk, rv, rg, gn, batch, seq):
    n, w = rq.shape
    tb = RET_ROWS
    nc = seq // tb
    dm, qdec, kdec, cdec = _ret_consts(tb)
    blk = pl.BlockSpec((tb, w), lambda b, c: (b * nc + c, 0))
    c3 = lambda b, c: (0, 0, 0)
    return pl.pallas_call(
        functools.partial(_ret_kernel, cdec),
        grid=(batch, nc),
        in_specs=[blk, blk, blk, blk,
                  pl.BlockSpec(dm.shape, c3), pl.BlockSpec(qdec.shape, c3), pl.BlockSpec(kdec.shape, c3),
                  pl.BlockSpec((1, w), lambda b, c: (0, 0))],
        out_specs=blk,
        out_shape=jax.ShapeDtypeStruct((n, w), BF16),
        scratch_shapes=[pltpu.VMEM((RET_HEADS, RET_DK, RET_DK), F32)],
        compiler_params=_params("parallel", "arbitrary"),
        name="retention",
    )(rq, rk, rv, rg, dm, qdec, kdec, gn)


def _sb_kernel(q_ref, k_ref, vt_ref, o_ref, acc_ref):
    tq = q_ref.shape[0]
    tk = tq
    grp = tk // SUBLANES
    heads = q_ref.shape[1] // SB_DH
    wide = heads * tq
    assert tq & (tq - 1) == 0
    qi = pl.program_id(1)
    per_group = LANES // SB_DH
    lane = lax.broadcasted_iota(I32, (tq, LANES), 1)
    q_groups = []
    for gi in range(heads // per_group):
        qg = q_ref[:, gi * LANES:(gi + 1) * LANES]
        q_groups.append(jnp.concatenate(
            [jnp.where((lane >= hh * SB_DH) & (lane < (hh + 1) * SB_DH), qg, jnp.zeros_like(qg))
             for hh in range(per_group)], axis=0))
    shp3 = (grp, SUBLANES, wide)
    key_pos = lax.broadcasted_iota(I32, shp3, 1) * grp + lax.broadcasted_iota(I32, shp3, 0)
    causal = key_pos < (lax.broadcasted_iota(I32, shp3, 2) & (tq - 1))
    row8 = lax.broadcasted_iota(I32, (SUBLANES, wide), 0)

    def block(kb, carry, diagonal):
        z = jnp.concatenate(
            [lax.dot_general(k_ref[kb, :, gi * LANES:(gi + 1) * LANES], qg, (((1,), (1,)), ((), ())),
                             preferred_element_type=F32) for gi, qg in enumerate(q_groups)],
            axis=1) * LOG2E
        z3 = z.reshape(shp3)
        if diagonal:
            z3 = jnp.where(causal, z3, SB_MASKED)
        run = jnp.ones((SUBLANES, wide), F32)
        local = [None] * grp
        for g in reversed(range(grp)):
            ez = jnp.exp2(jnp.minimum(z3[g], SB_CLAMP))
            run = run * (1.0 / (1.0 + ez))
            local[g] = ez * run
        log_tot = jnp.log2(run)
        off = jnp.where(row8 + 1 < SUBLANES, pltpu.roll(log_tot, SUBLANES - 1, 0), 0.0)
        for d in (1, 2, 4):
            off = off + jnp.where(row8 + d < SUBLANES, pltpu.roll(off, SUBLANES - d, 0), 0.0)
        scale = jnp.exp2(off + carry)
        a3 = jnp.stack(local, axis=0) * scale[None]
        a = a3.reshape(tk, wide).astype(BF16)
        for hh in range(heads):
            rows = slice(hh * SB_DH, (hh + 1) * SB_DH)
            contrib = jnp.dot(vt_ref[kb, rows, :], a[:, hh * tq:(hh + 1) * tq], preferred_element_type=F32)
            if diagonal:
                acc_ref[rows, :] = contrib
            else:
                acc_ref[rows, :] += contrib
        return carry + jnp.broadcast_to(off[0:1, :] + log_tot[0:1, :], carry.shape)

    carry = block(qi, jnp.zeros((SUBLANES, wide), F32), True)

    def cond(c):
        return (c[0] < qi) & (c[1] > 0)

    def body(c):
        new = block(qi - 1 - c[0], c[2], False)
        go = (jnp.max(new) > SB_SKIP_LOG2).astype(I32)
        return (c[0] + 1, go, new)

    lax.while_loop(cond, body, (jnp.int32(0), jnp.int32(1), carry))
    o_ref[...] = acc_ref[...].T.astype(BF16)


def _stick_breaking(sq, kperm, vtperm, batch, seq):
    n, w = sq.shape
    tq = SB_BLOCK
    nq = seq // tq
    return pl.pallas_call(
        _sb_kernel,
        grid=(batch, nq),
        in_specs=[
            pl.BlockSpec((tq, w), lambda b, i: (b * nq + i, 0)),
            pl.BlockSpec((None, nq, tq, w), lambda b, i: (b, 0, 0, 0)),
            pl.BlockSpec((None, nq, w, tq), lambda b, i: (b, 0, 0, 0)),
        ],
        out_specs=pl.BlockSpec((tq, w), lambda b, i: (b * nq + i, 0)),
        out_shape=jax.ShapeDtypeStruct((n, w), BF16),
        scratch_shapes=[pltpu.VMEM((w, tq), F32)],
        compiler_params=_params("parallel", "arbitrary"),
        name="stickbrk",
    )(sq, kperm, vtperm)


def _merge_kernel(ret_ref, sb_ref, gr_ref, gs_ref, x_ref, wbr_ref, wbs_ref, wo_ref, nf_ref,
                  wr_ref, br_ref, tri_ref,
                  x1_ref, h2_ref, ri_ref, rw_ref, cnt_ref, carry_ref):
    @pl.when(pl.program_id(0) == 0)
    def _():
        carry_ref[...] = jnp.zeros_like(carry_ref)

    tm = x_ref.shape[0]
    merged = (_sigmoid(gr_ref[...].astype(F32)) * jnp.dot(ret_ref[...], wbr_ref[...], preferred_element_type=F32)
              + _sigmoid(gs_ref[...].astype(F32)) * jnp.dot(sb_ref[...], wbs_ref[...], preferred_element_type=F32))
    x1 = x_ref[...] + jnp.dot(merged.astype(BF16), wo_ref[...], preferred_element_type=F32)
    x1_ref[...] = x1
    h2 = _rms(x1, nf_ref[...])
    _to_row_tiles(h2_ref, h2)
    logits = jnp.dot(h2.astype(BF16), wr_ref[...], preferred_element_type=F32) + br_ref[...]
    lt = logits.T
    neg = jnp.float32(-jnp.inf)

    grow = lax.broadcasted_iota(I32, (SUBLANES, tm), 0)
    gl = jnp.where(grow < N_GROUPS, lt[0:SUBLANES], neg)
    gmax = jnp.max(gl, axis=0, keepdims=True)
    g_idx = jnp.min(jnp.where(gl == gmax, grow, N_GROUPS), axis=0, keepdims=True)
    p_grp = 1.0 / jnp.sum(jnp.exp(gl - gmax), axis=0, keepdims=True)

    erow = lax.broadcasted_iota(I32, (N_EXPERTS, tm), 0)
    el = lt[EXPERT_COL0:EXPERT_COL0 + N_EXPERTS]
    el = jnp.where((erow >= g_idx * EXPERTS_PER_GROUP) & (erow < (g_idx + 1) * EXPERTS_PER_GROUP), el, neg)
    m1 = jnp.max(el, axis=0, keepdims=True)
    i1 = jnp.min(jnp.where(el == m1, erow, N_EXPERTS), axis=0, keepdims=True)
    el2 = jnp.where(erow == i1, neg, el)
    m2 = jnp.max(el2, axis=0, keepdims=True)
    i2 = jnp.min(jnp.where(el2 == m2, erow, N_EXPERTS), axis=0, keepdims=True)
    e2 = jnp.exp(m2 - m1)
    w1 = p_grp / (1.0 + e2)
    w2 = p_grp * e2 / (1.0 + e2)

    l1 = i1 - g_idx * EXPERTS_PER_GROUP
    l2 = i2 - g_idx * EXPERTS_PER_GROUP
    lo = jnp.minimum(l1, l2)
    hi = jnp.maximum(l1, l2)
    cls = g_idx * PAIRS_PER_GROUP + ((lo * (2 * EXPERTS_PER_GROUP - 1 - lo)) >> 1) + (hi - lo - 1)
    w_lo = jnp.where(l1 < l2, w1, w2)
    w_hi = jnp.where(l1 < l2, w2, w1)

    crow = lax.broadcasted_iota(I32, (CLASS_ROWS, tm), 0)
    sel = crow == cls
    onehot = sel.astype(BF16)
    cinc = jnp.dot(onehot, tri_ref[...], preferred_element_type=F32)
    carry = carry_ref[...]
    rank = jnp.sum(jnp.where(sel, carry[:, 0:1] + cinc - 1.0, 0.0), axis=0, keepdims=True)
    carry_new = carry + jnp.sum(onehot.astype(F32), axis=1, keepdims=True)
    carry_ref[...] = carry_new
    cnt_ref[...] = carry_new

    ri_ref[...] = jnp.zeros_like(ri_ref)
    ri_ref[0:1, :] = cls
    ri_ref[1:2, :] = rank.astype(I32)
    wrow = lax.broadcasted_iota(I32, (ROUTE_COLS, tm), 0)
    wt = jnp.where(wrow == 0, w_lo, jnp.where(wrow == 1, w_hi, 0.0))
    rw_ref[...] = wt.T


def _merge(ret, sb, g_ret, g_sb, x2, wbr, wbs, wo, nf, wr, br):
    n, d = x2.shape
    tm = MERGE_ROWS
    tri = jnp.asarray(np.triu(np.ones((tm, tm), np.float32)), BF16)
    row = lambda i: (i, 0)
    const = lambda i: (0, 0)
    full = lambda a: pl.BlockSpec(a.shape, const)
    return pl.pallas_call(
        _merge_kernel,
        grid=(n // tm,),
        in_specs=[pl.BlockSpec((tm, 512), row), pl.BlockSpec((tm, 512), row),
                  pl.BlockSpec((tm, d), row), pl.BlockSpec((tm, d), row), pl.BlockSpec((tm, d), row),
                  full(wbr), full(wbs), full(wo), full(nf), full(wr), full(br), full(tri)],
        out_specs=[pl.BlockSpec((tm, d), row), pl.BlockSpec((tm * ROW_TILES, LANES), row),
                   pl.BlockSpec((SUBLANES, tm), lambda i: (0, i)),
                   pl.BlockSpec((tm, ROUTE_COLS), row),
                   pl.BlockSpec((CLASS_ROWS, LANES), const)],
        out_shape=[jax.ShapeDtypeStruct((n, d), F32), jax.ShapeDtypeStruct((n * ROW_TILES, LANES), F32),
                   jax.ShapeDtypeStruct((SUBLANES, n), I32),
                   jax.ShapeDtypeStruct((n, ROUTE_COLS), F32),
                   jax.ShapeDtypeStruct((CLASS_ROWS, LANES), F32)],
        scratch_shapes=[pltpu.VMEM((CLASS_ROWS, LANES), F32)],
        compiler_params=_params("arbitrary"),
        name="merge",
    )(ret, sb, g_ret, g_sb, x2, wbr, wbs, wo, nf, wr, br, tri)


DMA_QUEUES = 2


def _tile_rows(i):
    return pl.ds(pl.multiple_of(i * ROW_TILES, ROW_TILES), ROW_TILES)


def _dispatch_kernel(fill_ref, end_ref, dest_ref, src_hbm, zero_hbm, dst_hbm, sem):
    block_rows = zero_hbm.shape[0]
    i = pl.program_id(0)
    per_step = dest_ref.shape[2]

    def send(i2, c):
        for k in range(DMA_QUEUES):
            r = i2 * DMA_QUEUES + k
            pltpu.make_async_copy(src_hbm.at[_tile_rows(i * per_step + r)],
                                  dst_hbm.at[_tile_rows(dest_ref[0, 0, r])], sem).start(priority=k)
        return c
    lax.fori_loop(0, per_step // DMA_QUEUES, send, 0, unroll=4)
    span = pl.ds(0, per_step * ROW_TILES)
    pltpu.make_async_copy(src_hbm.at[span], dst_hbm.at[span], sem).wait()

    @pl.when(i == pl.num_programs(0) - 1)
    def _():
        def pad(s):
            return pltpu.make_async_copy(zero_hbm.at[pl.ds(0, ROW_TILES)], dst_hbm.at[_tile_rows(s)], sem)

        def unused(b, c):
            blk = pltpu.make_async_copy(
                zero_hbm, dst_hbm.at[pl.ds(pl.multiple_of(b * block_rows, block_rows), block_rows)], sem)
            blk.start()
            blk.wait()
            return c
        last = end_ref.shape[0] - 1
        lax.fori_loop(end_ref[last] // (block_rows // ROW_TILES), dst_hbm.shape[0] // block_rows, unused, 0)

        def start(s, c):
            pad(s).start()
            return c

        def wait(s, c):
            pad(s).wait()
            return c

        for cl in range(fill_ref.shape[0]):
            lax.fori_loop(fill_ref[cl], end_ref[cl], start, 0)
        for cl in range(fill_ref.shape[0]):
            lax.fori_loop(fill_ref[cl], end_ref[cl], wait, 0)


def _dispatch(dest, fill, end, h2, slots, tmb):
    n = dest.shape[0]
    per_step = min(DISPATCH_TOKENS, n)
    return pl.pallas_call(
        _dispatch_kernel,
        grid_spec=pltpu.PrefetchScalarGridSpec(
            num_scalar_prefetch=2,
            grid=(n // per_step,),
            in_specs=[pl.BlockSpec((1, 1, per_step), lambda i, f, e: (i, 0, 0), memory_space=pltpu.SMEM),
                      pl.BlockSpec(memory_space=pl.ANY),
                      pl.BlockSpec(memory_space=pl.ANY)],
            out_specs=pl.BlockSpec(memory_space=pl.ANY),
            scratch_shapes=[pltpu.SemaphoreType.DMA(())],
        ),
        out_shape=jax.ShapeDtypeStruct((slots * ROW_TILES, LANES), F32),
        compiler_params=_params("arbitrary"),
        name="dispatch",
    )(fill, end, dest.reshape(n // per_step, 1, per_step), h2, jnp.zeros((tmb * ROW_TILES, LANES), F32))


def _expert_kernel(bl_ref, bh_ref, nu_ref, x_ref,
                   wgl_ref, wul_ref, wdl_ref, wgh_ref, wuh_ref, wdh_ref, o_ref):
    j = pl.program_id(0)
    used = nu_ref[0]
    rows = x_ref.shape[0] // ROW_TILES

    @pl.when(j < used)
    def _():
        xb = _from_row_tiles(x_ref, 0, rows).astype(BF16)

        def ffn(wg_ref, wu_ref, wd_ref):
            gate = jnp.dot(xb, wg_ref[...].astype(BF16), preferred_element_type=F32)
            up = jnp.dot(xb, wu_ref[...].astype(BF16), preferred_element_type=F32)
            hid = (gate * _sigmoid(gate) * up).astype(BF16)
            return jnp.dot(hid, wd_ref[...].astype(BF16), preferred_element_type=F32)

        _to_row_tiles(o_ref.at[0], ffn(wgl_ref, wul_ref, wdl_ref))
        _to_row_tiles(o_ref.at[1], ffn(wgh_ref, wuh_ref, wdh_ref))

    @pl.when(j >= used)
    def _():
        o_ref[...] = jnp.zeros_like(o_ref)


def _experts(block_lo, block_hi, n_used, xs, tmb, wg, wu, wd):
    nb = xs.shape[0] // (tmb * ROW_TILES)
    d = wg.shape[1]
    de = wg.shape[2]
    last = lambda j, nu: jnp.minimum(j, nu[0] - 1)
    lo_map = lambda j, bl, bh, nu: (bl[last(j, nu)], 0, 0)
    hi_map = lambda j, bl, bh, nu: (bh[last(j, nu)], 0, 0)
    return pl.pallas_call(
        _expert_kernel,
        grid_spec=pltpu.PrefetchScalarGridSpec(
            num_scalar_prefetch=3,
            grid=(nb,),
            in_specs=[
                pl.BlockSpec((tmb * ROW_TILES, LANES), lambda j, bl, bh, nu: (last(j, nu), 0)),
                pl.BlockSpec((None, d, de), lo_map),
                pl.BlockSpec((None, d, de), lo_map),
                pl.BlockSpec((None, de, d), lo_map),
                pl.BlockSpec((None, d, de), hi_map),
                pl.BlockSpec((None, d, de), hi_map),
                pl.BlockSpec((None, de, d), hi_map),
            ],
            out_specs=pl.BlockSpec((2, tmb * ROW_TILES, LANES), lambda j, bl, bh, nu: (0, j, 0)),
        ),
        out_shape=jax.ShapeDtypeStruct((2, nb * tmb * ROW_TILES, LANES), F32),
        compiler_params=_params("arbitrary"),
        name="experts",
    )(block_lo, block_hi, n_used, xs, wg, wu, wd, wg, wu, wd)


def _issue_pairs(idx_ref, yb_hbm, dst, sem, rows):
    def body(i, c):
        for k in range(DMA_QUEUES):
            r = i * DMA_QUEUES + k
            pltpu.make_async_copy(yb_hbm.at[:, _tile_rows(idx_ref[0, 0, r]), :], dst.at[:, _tile_rows(r), :],
                                  sem).start(priority=k)
        return c
    lax.fori_loop(0, rows // DMA_QUEUES, body, 0, unroll=4)


def _final_kernel(idx_ref, idxn_ref, x1_ref, rw_ref, p_ref, yb_hbm, npl_ref, wpg_ref, wp_ref, nfin_ref,
                  o_ref, gbuf, sem, x2_ref):
    i = pl.program_id(0)
    nt = pl.num_programs(0)
    tm = x1_ref.shape[0]
    slot = i % 2

    @pl.when(i == 0)
    def _():
        _issue_pairs(idx_ref, yb_hbm, gbuf.at[0], sem.at[0], tm)

    @pl.when(i + 1 < nt)
    def _():
        _issue_pairs(idxn_ref, yb_hbm, gbuf.at[1 - slot], sem.at[1 - slot], tm)

    pltpu.make_async_copy(gbuf.at[slot], gbuf.at[slot], sem.at[slot]).wait()
    w_lo = jnp.broadcast_to(rw_ref[:, 0:1], (tm, LANES))
    w_hi = jnp.broadcast_to(rw_ref[:, 1:2], (tm, LANES))
    for s in range(ROW_TILES):
        cols = slice(s * LANES, (s + 1) * LANES)
        x2_ref[:, cols] = (x1_ref[:, cols]
                           + w_lo * gbuf[slot, 0, pl.ds(s, tm, stride=ROW_TILES), :]
                           + w_hi * gbuf[slot, 1, pl.ds(s, tm, stride=ROW_TILES), :])
    x2 = x2_ref[...]
    hp = _rms(x2, npl_ref[...]).astype(BF16)
    gate = _sigmoid(jnp.dot(hp, wpg_ref[...], preferred_element_type=F32))
    emb = jnp.dot(p_ref[...].astype(BF16), wp_ref[...], preferred_element_type=F32)
    x3 = x2 + gate * emb
    o_ref[...] = _rms(x3, nfin_ref[...])


def _final(dest3, x1, rw, p2, yb, npl, wpg, wp, nfin):
    n, d = x1.shape
    tm = FINAL_ROWS
    nt = n // tm
    row = lambda i: (i, 0)
    const = lambda i: (0, 0)
    full = lambda a: pl.BlockSpec(a.shape, const)
    return pl.pallas_call(
        _final_kernel,
        grid=(nt,),
        in_specs=[
            pl.BlockSpec((1, 1, tm), lambda i: (i, 0, 0), memory_space=pltpu.SMEM),
            pl.BlockSpec((1, 1, tm), lambda i: (jnp.minimum(i + 1, nt - 1), 0, 0), memory_space=pltpu.SMEM),
            pl.BlockSpec((tm, d), row),
            pl.BlockSpec((tm, ROUTE_COLS), row),
            pl.BlockSpec((tm, p2.shape[1]), row),
            pl.BlockSpec(memory_space=pl.ANY),
            full(npl), full(wpg), full(wp), full(nfin),
        ],
        out_specs=pl.BlockSpec((tm, d), row),
        out_shape=jax.ShapeDtypeStruct((n, d), F32),
        scratch_shapes=[pltpu.VMEM((2, 2, tm * ROW_TILES, LANES), F32), pltpu.SemaphoreType.DMA((2,)),
                        pltpu.VMEM((tm, d), F32)],
        compiler_params=_params("arbitrary"),
        name="final",
    )(dest3, dest3, x1, rw, p2, yb, npl, wpg, wp, nfin)


def _layer(x2, p2, batch, seq, cosf, sinf, norm_mix, w_in, ret_gn, w_br_ret, w_br_sb, w_out, norm_ffn,
           w_grp, b_grp, w_exp, b_exp, w_gate, w_up, w_down, norm_ple, w_ple, w_ple_gate, out_gain):
    n, d = x2.shape
    rq, rk, rv, rg, sq, skp, svt, g_ret, g_sb = _proj(x2, norm_mix[None, :], w_in.astype(BF16), cosf, sinf, seq)
    ret = _retention(rq, rk, rv, rg, ret_gn[None, :], batch, seq)

    tk = SB_BLOCK
    nkb = seq // tk
    kperm = skp.reshape(batch, nkb, tk, skp.shape[1])
    vtperm = svt.reshape(batch, nkb, svt.shape[1], tk)
    sb = _stick_breaking(sq, kperm, vtperm, batch, seq)

    wr = jnp.zeros((d, ROUTE_COLS), F32)
    wr = wr.at[:, 0:N_GROUPS].set(w_grp).at[:, EXPERT_COL0:EXPERT_COL0 + N_EXPERTS].set(w_exp)
    br = jnp.zeros((1, ROUTE_COLS), F32)
    br = br.at[0, 0:N_GROUPS].set(b_grp).at[0, EXPERT_COL0:EXPERT_COL0 + N_EXPERTS].set(b_exp)
    x1, h2, ri, rw, cnt = _merge(ret, sb, g_ret, g_sb, x2, w_br_ret.astype(BF16), w_br_sb.astype(BF16),
                                 w_out.astype(BF16), norm_ffn[None, :], wr.astype(BF16), br)

    tmb = EXPERT_ROWS
    counts = cnt[:N_CLASSES, 0].astype(I32)
    padded = (counts + tmb - 1) // tmb * tmb
    pad_end = jnp.cumsum(padded)
    pad_start = pad_end - padded
    cids = jnp.arange(N_CLASSES, dtype=I32)
    dest = jnp.sum(jnp.where(ri[0, :, None] == cids, pad_start, 0), axis=-1) + ri[1]
    nb = n // tmb + N_CLASSES
    block_c = jnp.minimum(
        jnp.sum((pad_end[None, :] <= (jnp.arange(nb, dtype=I32) * tmb)[:, None]).astype(I32), axis=1),
        N_CLASSES - 1)
    pairs = [(lo, hi) for lo in range(EXPERTS_PER_GROUP) for hi in range(lo + 1, EXPERTS_PER_GROUP)]
    class_lo = np.array([g * EXPERTS_PER_GROUP + lo for g in range(N_GROUPS) for lo, _ in pairs], np.int32)
    class_hi = np.array([g * EXPERTS_PER_GROUP + hi for g in range(N_GROUPS) for _, hi in pairs], np.int32)
    block_lo = jnp.sum(jnp.where(block_c[:, None] == cids, class_lo, 0), axis=-1).astype(I32)
    block_hi = jnp.sum(jnp.where(block_c[:, None] == cids, class_hi, 0), axis=-1).astype(I32)
    n_used = (pad_end[-1:] // tmb).astype(I32)
    xs = _dispatch(dest, pad_start + counts, pad_end, h2, nb * tmb, tmb)
    yb = _experts(block_lo, block_hi, n_used, xs, tmb, w_gate, w_up, w_down)

    tf = FINAL_ROWS
    return _final(dest.reshape(n // tf, 1, tf), x1, rw, p2, yb, norm_ple[None, :],
                  w_ple_gate.astype(BF16), w_ple.astype(BF16), out_gain)


def kernel(x, p, norm_mix, w_in, ret_gn, w_br_ret, w_br_sb, w_out, norm_ffn, w_grp, b_grp, w_exp, b_exp,
           w_gate, w_up, w_down, norm_ple, w_ple, w_ple_gate, norm_final):
    batch, seq, d = x.shape
    depth = p.shape[0]
    assert depth == 1, "the final RMSNorm is fused into the last layer's kernel"
    pos = jnp.arange(seq, dtype=F32)
    inv_freq = ROPE_BASE ** (-jnp.arange(0, RET_DK, 2, dtype=F32) / RET_DK)
    ang = pos[:, None] * inv_freq[None, :]
    cos, sin = jnp.cos(ang), jnp.sin(ang)
    cosf = jnp.concatenate([cos, cos], axis=1)
    sinf = jnp.concatenate([-sin, sin], axis=1)
    x2 = x.reshape(batch * seq, d)
    i = 0
    out = _layer(x2, p[i].reshape(batch * seq, -1), batch, seq, cosf, sinf, norm_mix[i], w_in[i], ret_gn[i],
                 w_br_ret[i], w_br_sb[i], w_out[i], norm_ffn[i], w_grp[i], b_grp[i], w_exp[i], b_exp[i],
                 w_gate[i], w_up[i], w_down[i], norm_ple[i], w_ple[i], w_ple_gate[i], norm_final[None, :])
    return out.reshape(batch, seq, d)
```

```python
import functools

import numpy as np
import jax
import jax.numpy as jnp
from jax import lax
from jax.experimental import pallas as pl
from jax.experimental.pallas import tpu as pltpu

F32 = jnp.float32
BF16 = jnp.bfloat16
I32 = jnp.int32

EPS = 1e-6
CHUNK = 64
RET_HEADS = 4
RET_DK = 128
SB_HEADS = 8
SB_DH = 64
ROPE_BASE = 10000.0
N_GROUPS = 4
EXPERTS_PER_GROUP = 4
N_EXPERTS = N_GROUPS * EXPERTS_PER_GROUP
PAIRS_PER_GROUP = EXPERTS_PER_GROUP * (EXPERTS_PER_GROUP - 1) // 2
N_CLASSES = N_GROUPS * PAIRS_PER_GROUP
CLASS_ROWS = 32

LANES = 128
SUBLANES = 8
VMEM_LIMIT = 56 * 1024 * 1024

PROJ_ROWS = 512
RET_ROWS = 512
SB_BLOCK = 256
MERGE_ROWS = 512
EXPERT_ROWS = 512
FINAL_ROWS = 512
DISPATCH_TOKENS = 1024
ROUTE_COLS = 128
ROW_TILES = 8
EXPERT_COL0 = 8
LOG2E = 1.4426950408889634
SB_SKIP_LOG2 = -160.0
SB_CLAMP = 126.0
SB_MASKED = -1e30


def _rms(x, g):
    return x * lax.rsqrt(jnp.mean(x * x, axis=-1, keepdims=True) + EPS) * g


def _sigmoid(x):
    return 1.0 / (1.0 + jnp.exp(-x))


def _to_row_tiles(ref, val):
    rows = val.shape[0]
    for s in range(ROW_TILES):
        ref[pl.ds(s, rows, stride=ROW_TILES), :] = val[:, s * LANES:(s + 1) * LANES]


def _from_row_tiles(ref, first, rows):
    return jnp.concatenate([ref[pl.ds(first * ROW_TILES + s, rows, stride=ROW_TILES), :]
                            for s in range(ROW_TILES)], axis=1)


def _params(*sem):
    return pltpu.CompilerParams(dimension_semantics=sem, vmem_limit_bytes=VMEM_LIMIT)


def _proj_kernel(x_ref, g_ref, w_ref, perm_ref, cos_ref, sin_ref,
                 rq_ref, rk_ref, rv_ref, rg_ref, sq_ref, sk_ref, svt_ref, gr_ref, gs_ref):
    h = _rms(x_ref[...], g_ref[...]).astype(BF16)
    hp = jnp.dot(perm_ref[...], h, preferred_element_type=F32).astype(BF16)
    cos = cos_ref[...]
    sin = sin_ref[...]

    def seg(lo, hi):
        return jnp.dot(h, w_ref[:, lo:hi], preferred_element_type=F32)

    def rope(t):
        outs = []
        for hd in range(RET_HEADS):
            th = t[:, hd * RET_DK:(hd + 1) * RET_DK]
            outs.append(th * cos + pltpu.roll(th, RET_DK // 2, 1) * sin)
        return jnp.concatenate(outs, axis=1)

    rq_ref[...] = rope(seg(0, 512)).astype(BF16)
    rk_ref[...] = (rope(seg(512, 1024)) * (RET_DK ** -0.5)).astype(BF16)
    rv_ref[...] = seg(1024, 1536).astype(BF16)
    rg_ref[...] = seg(1536, 2048).astype(BF16)
    sq_ref[...] = (seg(2048, 2560) * (SB_DH ** -0.5)).astype(BF16)
    sk_ref[...] = jnp.dot(hp, w_ref[:, 2560:3072], preferred_element_type=F32).astype(BF16)
    svt = lax.dot_general(w_ref[:, 3072:3584], hp, (((0,), (1,)), ((), ())), preferred_element_type=F32)
    tk = svt_ref.shape[2]
    for kb in range(svt_ref.shape[0]):
        svt_ref[kb] = svt[:, kb * tk:(kb + 1) * tk].astype(BF16)
    gr_ref[...] = seg(3584, 4608).astype(BF16)
    gs_ref[...] = seg(4608, 5632).astype(BF16)


def _key_block_perm(tm, tk):
    grp = tk // SUBLANES
    new = np.arange(tm)
    blk, r = new // tk, new % tk
    old = blk * tk + (r % SUBLANES) * grp + r // SUBLANES
    p = np.zeros((tm, tm), np.float32)
    p[new, old] = 1.0
    return jnp.asarray(p, BF16)


def _proj(x2, g, w, cosf, sinf, seq):
    n, d = x2.shape
    tm = PROJ_ROWS
    tk = SB_BLOCK
    pos_blocks = seq // tm
    perm = _key_block_perm(tm, tk)
    row = lambda i: (i, 0)
    const = lambda i: (0, 0)
    half = jax.ShapeDtypeStruct((n, 512), BF16)
    full = jax.ShapeDtypeStruct((n, 1024), BF16)
    svt = jax.ShapeDtypeStruct((n // tk, 512, tk), BF16)
    hspec = pl.BlockSpec((tm, 512), row)
    fspec = pl.BlockSpec((tm, 1024), row)
    return pl.pallas_call(
        _proj_kernel,
        grid=(n // tm,),
        in_specs=[
            pl.BlockSpec((tm, d), row),
            pl.BlockSpec((1, d), const),
            pl.BlockSpec(w.shape, const),
            pl.BlockSpec(perm.shape, const),
            pl.BlockSpec((tm, RET_DK), lambda i: (i % pos_blocks, 0)),
            pl.BlockSpec((tm, RET_DK), lambda i: (i % pos_blocks, 0)),
        ],
        out_specs=[hspec] * 6 + [pl.BlockSpec((tm // tk, 512, tk), lambda i: (i, 0, 0)), fspec, fspec],
        out_shape=[half] * 6 + [svt, full, full],
        compiler_params=_params("parallel"),
        name="proj",
    )(x2, g, w, perm, cosf, sinf)


def _ret_consts(tb):
    hd = np.arange(RET_HEADS, dtype=np.float64)
    lg = np.log(1.0 - np.exp2(-5.0 - hd))
    idx = np.arange(tb, dtype=np.float64)
    dist = np.abs(idx[:, None] - idx[None, :])
    chunk = (np.arange(tb) // CHUNK)
    allowed = chunk[None, :] <= chunk[:, None]
    dm = np.exp(lg[:, None, None] * dist) * allowed
    qdec = np.exp(lg[:, None] * (idx + 1.0))[:, :, None] * np.ones((1, 1, RET_DK))
    kdec = np.exp(lg[:, None] * (tb - 1.0 - idx))[:, :, None] * np.ones((1, 1, RET_DK))
    cdec = [float(v) for v in np.exp(lg * tb)]
    return (jnp.asarray(dm, F32), jnp.asarray(qdec, F32), jnp.asarray(kdec, F32), cdec)


def _ret_kernel(cdec, q_ref, k_ref, v_ref, g_ref, dm_ref, qd_ref, kd_ref, gn_ref, o_ref, st_ref):
    @pl.when(pl.program_id(1) == 0)
    def _():
        st_ref[...] = jnp.zeros_like(st_ref)

    for hd in range(RET_HEADS):
        sl = slice(hd * RET_DK, (hd + 1) * RET_DK)
        q = q_ref[:, sl]
        k = k_ref[:, sl]
        v = v_ref[:, sl]
        st = st_ref[hd]
        scores = lax.dot_general(q, k, (((1,), (1,)), ((), ())), preferred_element_type=F32)
        scores = scores * dm_ref[hd]
        qd = (q.astype(F32) * qd_ref[hd]).astype(BF16)
        o = (jnp.dot(scores.astype(BF16), v, preferred_element_type=F32)
             + jnp.dot(qd, st.astype(BF16), preferred_element_type=F32))
        kd = (k.astype(F32) * kd_ref[hd]).astype(BF16)
        st_ref[hd] = cdec[hd] * st + lax.dot_general(
            kd, v, (((0,), (0,)), ((), ())), preferred_element_type=F32)
        mu = jnp.mean(o, axis=-1, keepdims=True)
        oc = o - mu
        var = jnp.mean(oc * oc, axis=-1, keepdims=True)
        on = oc * lax.rsqrt(var + EPS) * gn_ref[:, sl]
        gate = g_ref[:, sl].astype(F32)
        o_ref[:, sl] = (gate * _sigmoid(gate) * on).astype(BF16)


def _retention(rq, rk, rv, rg, gn, batch, seq):
    n, w = rq.shape
    tb = RET_ROWS
    nc = seq // tb
    dm, qdec, kdec, cdec = _ret_consts(tb)
    blk = pl.BlockSpec((tb, w), lambda b, c: (b * nc + c, 0))
    c3 = lambda b, c: (0, 0, 0)
    return pl.pallas_call(
        functools.partial(_ret_kernel, cdec),
        grid=(batch, nc),
        in_specs=[blk, blk, blk, blk,
                  pl.BlockSpec(dm.shape, c3), pl.BlockSpec(qdec.shape, c3), pl.BlockSpec(kdec.shape, c3),
                  pl.BlockSpec((1, w), lambda b, c: (0, 0))],
        out_specs=blk,
        out_shape=jax.ShapeDtypeStruct((n, w), BF16),
        scratch_shapes=[pltpu.VMEM((RET_HEADS, RET_DK, RET_DK), F32)],
        compiler_params=_params("parallel", "arbitrary"),
        name="retention",
    )(rq, rk, rv, rg, dm, qdec, kdec, gn)


def _sb_kernel(q_ref, k_ref, vt_ref, o_ref, acc_ref):
    tq = q_ref.shape[0]
    tk = tq
    grp = tk // SUBLANES
    heads = q_ref.shape[1] // SB_DH
    wide = heads * tq
    assert tq & (tq - 1) == 0
    qi = pl.program_id(1)
    per_group = LANES // SB_DH
    lane = lax.broadcasted_iota(I32, (tq, LANES), 1)
    q_groups = []
    for gi in range(heads // per_group):
        qg = q_ref[:, gi * LANES:(gi + 1) * LANES]
        q_groups.append(jnp.concatenate(
            [jnp.where((lane >= hh * SB_DH) & (lane < (hh + 1) * SB_DH), qg, jnp.zeros_like(qg))
             for hh in range(per_group)], axis=0))
    shp3 = (grp, SUBLANES, wide)
    key_pos = lax.broadcasted_iota(I32, shp3, 1) * grp + lax.broadcasted_iota(I32, shp3, 0)
    causal = key_pos < (lax.broadcasted_iota(I32, shp3, 2) & (tq - 1))
    row8 = lax.broadcasted_iota(I32, (SUBLANES, wide), 0)

    def block(kb, carry, diagonal):
        z = jnp.concatenate(
            [lax.dot_general(k_ref[kb, :, gi * LANES:(gi + 1) * LANES], qg, (((1,), (1,)), ((), ())),
                             preferred_element_type=F32) for gi, qg in enumerate(q_groups)],
            axis=1) * LOG2E
        z3 = z.reshape(shp3)
        if diagonal:
            z3 = jnp.where(causal, z3, SB_MASKED)
        run = jnp.ones((SUBLANES, wide), F32)
        local = [None] * grp
        for g in reversed(range(grp)):
            ez = jnp.exp2(jnp.minimum(z3[g], SB_CLAMP))
            run = run * (1.0 / (1.0 + ez))
            local[g] = ez * run
        log_tot = jnp.log2(run)
        off = jnp.where(row8 + 1 < SUBLANES, pltpu.roll(log_tot, SUBLANES - 1, 0), 0.0)
        for d in (1, 2, 4):
            off = off + jnp.where(row8 + d < SUBLANES, pltpu.roll(off, SUBLANES - d, 0), 0.0)
        scale = jnp.exp2(off + carry)
        a3 = jnp.stack(local, axis=0) * scale[None]
        a = a3.reshape(tk, wide).astype(BF16)
        for hh in range(heads):
            rows = slice(hh * SB_DH, (hh + 1) * SB_DH)
            contrib = jnp.dot(vt_ref[kb, rows, :], a[:, hh * tq:(hh + 1) * tq], preferred_element_type=F32)
            if diagonal:
                acc_ref[rows, :] = contrib
            else:
                acc_ref[rows, :] += contrib
        return carry + jnp.broadcast_to(off[0:1, :] + log_tot[0:1, :], carry.shape)

    carry = block(qi, jnp.zeros((SUBLANES, wide), F32), True)

    def cond(c):
        return (c[0] < qi) & (c[1] > 0)

    def body(c):
        new = block(qi - 1 - c[0], c[2], False)
        go = (jnp.max(new) > SB_SKIP_LOG2).astype(I32)
        return (c[0] + 1, go, new)

    lax.while_loop(cond, body, (jnp.int32(0), jnp.int32(1), carry))
    o_ref[...] = acc_ref[...].T.astype(BF16)


def _stick_breaking(sq, kperm, vtperm, batch, seq):
    n, w = sq.shape
    tq = SB_BLOCK
    nq = seq // tq
    return pl.pallas_call(
        _sb_kernel,
        grid=(batch, nq),
        in_specs=[
            pl.BlockSpec((tq, w), lambda b, i: (b * nq + i, 0)),
            pl.BlockSpec((None, nq, tq, w), lambda b, i: (b, 0, 0, 0)),
            pl.BlockSpec((None, nq, w, tq), lambda b, i: (b, 0, 0, 0)),
        ],
        out_specs=pl.BlockSpec((tq, w), lambda b, i: (b * nq + i, 0)),
        out_shape=jax.ShapeDtypeStruct((n, w), BF16),
        scratch_shapes=[pltpu.VMEM((w, tq), F32)],
        compiler_params=_params("parallel", "arbitrary"),
        name="stickbrk",
    )(sq, kperm, vtperm)


def _merge_kernel(ret_ref, sb_ref, gr_ref, gs_ref, x_ref, wbr_ref, wbs_ref, wo_ref, nf_ref,
                  wr_ref, br_ref, tri_ref,
                  x1_ref, h2_ref, ri_ref, rw_ref, cnt_ref, carry_ref):
    @pl.when(pl.program_id(0) == 0)
    def _():
        carry_ref[...] = jnp.zeros_like(carry_ref)

    tm = x_ref.shape[0]
    merged = (_sigmoid(gr_ref[...].astype(F32)) * jnp.dot(ret_ref[...], wbr_ref[...], preferred_element_type=F32)
              + _sigmoid(gs_ref[...].astype(F32)) * jnp.dot(sb_ref[...], wbs_ref[...], preferred_element_type=F32))
    x1 = x_ref[...] + jnp.dot(merged.astype(BF16), wo_ref[...], preferred_element_type=F32)
    x1_ref[...] = x1
    h2 = _rms(x1, nf_ref[...])
    _to_row_tiles(h2_ref, h2)
    logits = jnp.dot(h2.astype(BF16), wr_ref[...], preferred_element_type=F32) + br_ref[...]
    lt = logits.T
    neg = jnp.float32(-jnp.inf)

    grow = lax.broadcasted_iota(I32, (SUBLANES, tm), 0)
    gl = jnp.where(grow < N_GROUPS, lt[0:SUBLANES], neg)
    gmax = jnp.max(gl, axis=0, keepdims=True)
    g_idx = jnp.min(jnp.where(gl == gmax, grow, N_GROUPS), axis=0, keepdims=True)
    p_grp = 1.0 / jnp.sum(jnp.exp(gl - gmax), axis=0, keepdims=True)

    erow = lax.broadcasted_iota(I32, (N_EXPERTS, tm), 0)
    el = lt[EXPERT_COL0:EXPERT_COL0 + N_EXPERTS]
    el = jnp.where((erow >= g_idx * EXPERTS_PER_GROUP) & (erow < (g_idx + 1) * EXPERTS_PER_GROUP), el, neg)
    m1 = jnp.max(el, axis=0, keepdims=True)
    i1 = jnp.min(jnp.where(el == m1, erow, N_EXPERTS), axis=0, keepdims=True)
    el2 = jnp.where(erow == i1, neg, el)
    m2 = jnp.max(el2, axis=0, keepdims=True)
    i2 = jnp.min(jnp.where(el2 == m2, erow, N_EXPERTS), axis=0, keepdims=True)
    e2 = jnp.exp(m2 - m1)
    w1 = p_grp / (1.0 + e2)
    w2 = p_grp * e2 / (1.0 + e2)

    l1 = i1 - g_idx * EXPERTS_PER_GROUP
    l2 = i2 - g_idx * EXPERTS_PER_GROUP
    lo = jnp.minimum(l1, l2)
    hi = jnp.maximum(l1, l2)
    cls = g_idx * PAIRS_PER_GROUP + ((lo * (2 * EXPERTS_PER_GROUP - 1 - lo)) >> 1) + (hi - lo - 1)
    w_lo = jnp.where(l1 < l2, w1, w2)
    w_hi = jnp.where(l1 < l2, w2, w1)

    crow = lax.broadcasted_iota(I32, (CLASS_ROWS, tm), 0)
    sel = crow == cls
    onehot = sel.astype(BF16)
    cinc = jnp.dot(onehot, tri_ref[...], preferred_element_type=F32)
    carry = carry_ref[...]
    rank = jnp.sum(jnp.where(sel, carry[:, 0:1] + cinc - 1.0, 0.0), axis=0, keepdims=True)
    carry_new = carry + jnp.sum(onehot.astype(F32), axis=1, keepdims=True)
    carry_ref[...] = carry_new
    cnt_ref[...] = carry_new

    ri_ref[...] = jnp.zeros_like(ri_ref)
    ri_ref[0:1, :] = cls
    ri_ref[1:2, :] = rank.astype(I32)
    wrow = lax.broadcasted_iota(I32, (ROUTE_COLS, tm), 0)
    wt = jnp.where(wrow == 0, w_lo, jnp.where(wrow == 1, w_hi, 0.0))
    rw_ref[...] = wt.T


def _merge(ret, sb, g_ret, g_sb, x2, wbr, wbs, wo, nf, wr, br):
    n, d = x2.shape
    tm = MERGE_ROWS
    tri = jnp.asarray(np.triu(np.ones((tm, tm), np.float32)), BF16)
    row = lambda i: (i, 0)
    const = lambda i: (0, 0)
    full = lambda a: pl.BlockSpec(a.shape, const)
    return pl.pallas_call(
        _merge_kernel,
        grid=(n // tm,),
        in_specs=[pl.BlockSpec((tm, 512), row), pl.BlockSpec((tm, 512), row),
                  pl.BlockSpec((tm, d), row), pl.BlockSpec((tm, d), row), pl.BlockSpec((tm, d), row),
                  full(wbr), full(wbs), full(wo), full(nf), full(wr), full(br), full(tri)],
        out_specs=[pl.BlockSpec((tm, d), row), pl.BlockSpec((tm * ROW_TILES, LANES), row),
                   pl.BlockSpec((SUBLANES, tm), lambda i: (0, i)),
                   pl.BlockSpec((tm, ROUTE_COLS), row),
                   pl.BlockSpec((CLASS_ROWS, LANES), const)],
        out_shape=[jax.ShapeDtypeStruct((n, d), F32), jax.ShapeDtypeStruct((n * ROW_TILES, LANES), F32),
                   jax.ShapeDtypeStruct((SUBLANES, n), I32),
                   jax.ShapeDtypeStruct((n, ROUTE_COLS), F32),
                   jax.ShapeDtypeStruct((CLASS_ROWS, LANES), F32)],
        scratch_shapes=[pltpu.VMEM((CLASS_ROWS, LANES), F32)],
        compiler_params=_params("arbitrary"),
        name="merge",
    )(ret, sb, g_ret, g_sb, x2, wbr, wbs, wo, nf, wr, br, tri)


DMA_QUEUES = 2


def _tile_rows(i):
    return pl.ds(pl.multiple_of(i * ROW_TILES, ROW_TILES), ROW_TILES)


def _dispatch_kernel(fill_ref, end_ref, dest_ref, x_ref, zero_ref, dst_hbm, sem):
    block_rows = zero_ref.shape[0]
    i = pl.program_id(0)
    per_step = dest_ref.shape[2]

    def send(i2, c):
        for k in range(DMA_QUEUES):
            r = i2 * DMA_QUEUES + k
            pltpu.make_async_copy(x_ref.at[_tile_rows(r)],
                                  dst_hbm.at[_tile_rows(dest_ref[0, 0, r])], sem).start(priority=k)
        return c
    lax.fori_loop(0, per_step // DMA_QUEUES, send, 0, unroll=4)
    pltpu.make_async_copy(x_ref, dst_hbm.at[pl.ds(0, per_step * ROW_TILES)], sem).wait()

    @pl.when(i == pl.num_programs(0) - 1)
    def _():
        def pad(s):
            return pltpu.make_async_copy(zero_ref.at[pl.ds(0, ROW_TILES)], dst_hbm.at[_tile_rows(s)], sem)

        def unused(b, c):
            blk = pltpu.make_async_copy(
                zero_ref, dst_hbm.at[pl.ds(pl.multiple_of(b * block_rows, block_rows), block_rows)], sem)
            blk.start()
            blk.wait()
            return c
        last = end_ref.shape[0] - 1
        lax.fori_loop(end_ref[last] // (block_rows // ROW_TILES), dst_hbm.shape[0] // block_rows, unused, 0)

        def start(s, c):
            pad(s).start()
            return c

        def wait(s, c):
            pad(s).wait()
            return c

        for cl in range(fill_ref.shape[0]):
            lax.fori_loop(fill_ref[cl], end_ref[cl], start, 0)
        for cl in range(fill_ref.shape[0]):
            lax.fori_loop(fill_ref[cl], end_ref[cl], wait, 0)


def _dispatch(dest, fill, end, h2, slots, tmb):
    n = dest.shape[0]
    per_step = min(DISPATCH_TOKENS, n)
    return pl.pallas_call(
        _dispatch_kernel,
        grid_spec=pltpu.PrefetchScalarGridSpec(
            num_scalar_prefetch=2,
            grid=(n // per_step,),
            in_specs=[pl.BlockSpec((1, 1, per_step), lambda i, f, e: (i, 0, 0), memory_space=pltpu.SMEM),
                      pl.BlockSpec((per_step * ROW_TILES, LANES), lambda i, f, e: (i, 0)),
                      pl.BlockSpec((tmb * ROW_TILES, LANES), lambda i, f, e: (0, 0))],
            out_specs=pl.BlockSpec(memory_space=pl.ANY),
            scratch_shapes=[pltpu.SemaphoreType.DMA(())],
        ),
        out_shape=jax.ShapeDtypeStruct((slots * ROW_TILES, LANES), F32),
        compiler_params=_params("arbitrary"),
        name="dispatch",
    )(fill, end, dest.reshape(n // per_step, 1, per_step), h2, jnp.zeros((tmb * ROW_TILES, LANES), F32))


def _expert_kernel(bl_ref, bh_ref, nu_ref, x_ref,
                   wgl_ref, wul_ref, wdl_ref, wgh_ref, wuh_ref, wdh_ref, o_ref):
    j = pl.program_id(0)
    used = nu_ref[0]
    rows = x_ref.shape[0] // ROW_TILES

    @pl.when(j < used)
    def _():
        xb = _from_row_tiles(x_ref, 0, rows).astype(BF16)

        def ffn(wg_ref, wu_ref, wd_ref):
            gate = jnp.dot(xb, wg_ref[...].astype(BF16), preferred_element_type=F32)
            up = jnp.dot(xb, wu_ref[...].astype(BF16), preferred_element_type=F32)
            hid = (gate * _sigmoid(gate) * up).astype(BF16)
            return jnp.dot(hid, wd_ref[...].astype(BF16), preferred_element_type=F32)

        _to_row_tiles(o_ref.at[0], ffn(wgl_ref, wul_ref, wdl_ref))
        _to_row_tiles(o_ref.at[1], ffn(wgh_ref, wuh_ref, wdh_ref))

    @pl.when(j >= used)
    def _():
        o_ref[...] = jnp.zeros_like(o_ref)


def _experts(block_lo, block_hi, n_used, xs, tmb, wg, wu, wd):
    nb = xs.shape[0] // (tmb * ROW_TILES)
    d = wg.shape[1]
    de = wg.shape[2]
    last = lambda j, nu: jnp.minimum(j, nu[0] - 1)
    lo_map = lambda j, bl, bh, nu: (bl[last(j, nu)], 0, 0)
    hi_map = lambda j, bl, bh, nu: (bh[last(j, nu)], 0, 0)
    return pl.pallas_call(
        _expert_kernel,
        grid_spec=pltpu.PrefetchScalarGridSpec(
            num_scalar_prefetch=3,
            grid=(nb,),
            in_specs=[
                pl.BlockSpec((tmb * ROW_TILES, LANES), lambda j, bl, bh, nu: (last(j, nu), 0)),
                pl.BlockSpec((None, d, de), lo_map),
                pl.BlockSpec((None, d, de), lo_map),
                pl.BlockSpec((None, de, d), lo_map),
                pl.BlockSpec((None, d, de), hi_map),
                pl.BlockSpec((None, d, de), hi_map),
                pl.BlockSpec((None, de, d), hi_map),
            ],
            out_specs=pl.BlockSpec((2, tmb * ROW_TILES, LANES), lambda j, bl, bh, nu: (0, j, 0)),
        ),
        out_shape=jax.ShapeDtypeStruct((2, nb * tmb * ROW_TILES, LANES), F32),
        compiler_params=_params("arbitrary"),
        name="experts",
    )(block_lo, block_hi, n_used, xs, wg, wu, wd, wg, wu, wd)


def _issue_pairs(idx_ref, yb_hbm, dst, sem, rows):
    def body(i, c):
        for k in range(DMA_QUEUES):
            r = i * DMA_QUEUES + k
            pltpu.make_async_copy(yb_hbm.at[:, _tile_rows(idx_ref[0, 0, r]), :], dst.at[:, _tile_rows(r), :],
                                  sem).start(priority=k)
        return c
    lax.fori_loop(0, rows // DMA_QUEUES, body, 0, unroll=4)


def _final_kernel(idx_ref, idxn_ref, x1_ref, rw_ref, p_ref, yb_hbm, npl_ref, wpg_ref, wp_ref, nfin_ref,
                  o_ref, gbuf, sem, x2_ref):
    i = pl.program_id(0)
    nt = pl.num_programs(0)
    tm = x1_ref.shape[0]
    slot = i % 2

    @pl.when(i == 0)
    def _():
        _issue_pairs(idx_ref, yb_hbm, gbuf.at[0], sem.at[0], tm)

    @pl.when(i + 1 < nt)
    def _():
        _issue_pairs(idxn_ref, yb_hbm, gbuf.at[1 - slot], sem.at[1 - slot], tm)

    pltpu.make_async_copy(gbuf.at[slot], gbuf.at[slot], sem.at[slot]).wait()
    w_lo = jnp.broadcast_to(rw_ref[:, 0:1], (tm, LANES))
    w_hi = jnp.broadcast_to(rw_ref[:, 1:2], (tm, LANES))
    for s in range(ROW_TILES):
        cols = slice(s * LANES, (s + 1) * LANES)
        x2_ref[:, cols] = (x1_ref[:, cols]
                           + w_lo * gbuf[slot, 0, pl.ds(s, tm, stride=ROW_TILES), :]
                           + w_hi * gbuf[slot, 1, pl.ds(s, tm, stride=ROW_TILES), :])
    x2 = x2_ref[...]
    hp = _rms(x2, npl_ref[...]).astype(BF16)
    gate = _sigmoid(jnp.dot(hp, wpg_ref[...], preferred_element_type=F32))
    emb = jnp.dot(p_ref[...].astype(BF16), wp_ref[...], preferred_element_type=F32)
    x3 = x2 + gate * emb
    o_ref[...] = _rms(x3, nfin_ref[...])


def _final(dest3, x1, rw, p2, yb, npl, wpg, wp, nfin):
    n, d = x1.shape
    tm = FINAL_ROWS
    nt = n // tm
    row = lambda i: (i, 0)
    const = lambda i: (0, 0)
    full = lambda a: pl.BlockSpec(a.shape, const)
    return pl.pallas_call(
        _final_kernel,
        grid=(nt,),
        in_specs=[
            pl.BlockSpec((1, 1, tm), lambda i: (i, 0, 0), memory_space=pltpu.SMEM),
            pl.BlockSpec((1, 1, tm), lambda i: (jnp.minimum(i + 1, nt - 1), 0, 0), memory_space=pltpu.SMEM),
            pl.BlockSpec((tm, d), row),
            pl.BlockSpec((tm, ROUTE_COLS), row),
            pl.BlockSpec((tm, p2.shape[1]), row),
            pl.BlockSpec(memory_space=pl.ANY),
            full(npl), full(wpg), full(wp), full(nfin),
        ],
        out_specs=pl.BlockSpec((tm, d), row),
        out_shape=jax.ShapeDtypeStruct((n, d), F32),
        scratch_shapes=[pltpu.VMEM((2, 2, tm * ROW_TILES, LANES), F32), pltpu.SemaphoreType.DMA((2,)),
                        pltpu.VMEM((tm, d), F32)],
        compiler_params=_params("arbitrary"),
        name="final",
    )(dest3, dest3, x1, rw, p2, yb, npl, wpg, wp, nfin)


def _layer(x2, p2, batch, seq, cosf, sinf, norm_mix, w_in, ret_gn, w_br_ret, w_br_sb, w_out, norm_ffn,
           w_grp, b_grp, w_exp, b_exp, w_gate, w_up, w_down, norm_ple, w_ple, w_ple_gate, out_gain):
    n, d = x2.shape
    rq, rk, rv, rg, sq, skp, svt, g_ret, g_sb = _proj(x2, norm_mix[None, :], w_in.astype(BF16), cosf, sinf, seq)
    ret = _retention(rq, rk, rv, rg, ret_gn[None, :], batch, seq)

    tk = SB_BLOCK
    nkb = seq // tk
    kperm = skp.reshape(batch, nkb, tk, skp.shape[1])
    vtperm = svt.reshape(batch, nkb, svt.shape[1], tk)
    sb = _stick_breaking(sq, kperm, vtperm, batch, seq)

    wr = jnp.zeros((d, ROUTE_COLS), F32)
    wr = wr.at[:, 0:N_GROUPS].set(w_grp).at[:, EXPERT_COL0:EXPERT_COL0 + N_EXPERTS].set(w_exp)
    br = jnp.zeros((1, ROUTE_COLS), F32)
    br = br.at[0, 0:N_GROUPS].set(b_grp).at[0, EXPERT_COL0:EXPERT_COL0 + N_EXPERTS].set(b_exp)
    x1, h2, ri, rw, cnt = _merge(ret, sb, g_ret, g_sb, x2, w_br_ret.astype(BF16), w_br_sb.astype(BF16),
                                 w_out.astype(BF16), norm_ffn[None, :], wr.astype(BF16), br)

    tmb = EXPERT_ROWS
    counts = cnt[:N_CLASSES, 0].astype(I32)
    padded = (counts + tmb - 1) // tmb * tmb
    pad_end = jnp.cumsum(padded)
    pad_start = pad_end - padded
    cids = jnp.arange(N_CLASSES, dtype=I32)
    dest = jnp.sum(jnp.where(ri[0, :, None] == cids, pad_start, 0), axis=-1) + ri[1]
    nb = n // tmb + N_CLASSES
    block_c = jnp.minimum(
        jnp.sum((pad_end[None, :] <= (jnp.arange(nb, dtype=I32) * tmb)[:, None]).astype(I32), axis=1),
        N_CLASSES - 1)
    pairs = [(lo, hi) for lo in range(EXPERTS_PER_GROUP) for hi in range(lo + 1, EXPERTS_PER_GROUP)]
    class_lo = np.array([g * EXPERTS_PER_GROUP + lo for g in range(N_GROUPS) for lo, _ in pairs], np.int32)
    class_hi = np.array([g * EXPERTS_PER_GROUP + hi for g in range(N_GROUPS) for _, hi in pairs], np.int32)
    block_lo = jnp.sum(jnp.where(block_c[:, None] == cids, class_lo, 0), axis=-1).astype(I32)
    block_hi = jnp.sum(jnp.where(block_c[:, None] == cids, class_hi, 0), axis=-1).astype(I32)
    n_used = (pad_end[-1:] // tmb).astype(I32)
    xs = _dispatch(dest, pad_start + counts, pad_end, h2, nb * tmb, tmb)
    yb = _experts(block_lo, block_hi, n_used, xs, tmb, w_gate, w_up, w_down)

    tf = FINAL_ROWS
    return _final(dest.reshape(n // tf, 1, tf), x1, rw, p2, yb, norm_ple[None, :],
                  w_ple_gate.astype(BF16), w_ple.astype(BF16), out_gain)


def kernel(x, p, norm_mix, w_in, ret_gn, w_br_ret, w_br_sb, w_out, norm_ffn, w_grp, b_grp, w_exp, b_exp,
           w_gate, w_up, w_down, norm_ple, w_ple, w_ple_gate, norm_final):
    batch, seq, d = x.shape
    depth = p.shape[0]
    assert depth == 1, "the final RMSNorm is fused into the last layer's kernel"
    pos = jnp.arange(seq, dtype=F32)
    inv_freq = ROPE_BASE ** (-jnp.arange(0, RET_DK, 2, dtype=F32) / RET_DK)
    ang = pos[:, None] * inv_freq[None, :]
    cos, sin = jnp.cos(ang), jnp.sin(ang)
    cosf = jnp.concatenate([cos, cos], axis=1)
    sinf = jnp.concatenate([-sin, sin], axis=1)
    x2 = x.reshape(batch * seq, d)
    i = 0
    out = _layer(x2, p[i].reshape(batch * seq, -1), batch, seq, cosf, sinf, norm_mix[i], w_in[i], ret_gn[i],
                 w_br_ret[i], w_br_sb[i], w_out[i], norm_ffn[i], w_grp[i], b_grp[i], w_exp[i], b_exp[i],
                 w_gate[i], w_up[i], w_down[i], norm_ple[i], w_ple[i], w_ple_gate[i], norm_final[None, :])
    return out.reshape(batch, seq, d)
```

```python
import functools

import numpy as np
import jax
import jax.numpy as jnp
from jax import lax
from jax.experimental import pallas as pl
from jax.experimental.pallas import tpu as pltpu

F32 = jnp.float32
BF16 = jnp.bfloat16
I32 = jnp.int32

EPS = 1e-6
CHUNK = 64
RET_HEADS = 4
RET_DK = 128
SB_HEADS = 8
SB_DH = 64
ROPE_BASE = 10000.0
N_GROUPS = 4
EXPERTS_PER_GROUP = 4
N_EXPERTS = N_GROUPS * EXPERTS_PER_GROUP
PAIRS_PER_GROUP = EXPERTS_PER_GROUP * (EXPERTS_PER_GROUP - 1) // 2
N_CLASSES = N_GROUPS * PAIRS_PER_GROUP
CLASS_ROWS = 32

LANES = 128
SUBLANES = 8
VMEM_LIMIT = 56 * 1024 * 1024

PROJ_ROWS = 512
RET_ROWS = 512
SB_BLOCK = 256
MERGE_ROWS = 512
EXPERT_ROWS = 512
FINAL_ROWS = 512
DISPATCH_TOKENS = 1024
ROUTE_COLS = 128
ROW_TILES = 8
EXPERT_COL0 = 8
LOG2E = 1.4426950408889634
SB_SKIP_LOG2 = -160.0
SB_CLAMP = 126.0
SB_MASKED = -1e30


def _rms(x, g):
    return x * lax.rsqrt(jnp.mean(x * x, axis=-1, keepdims=True) + EPS) * g


def _sigmoid(x):
    return 0.5 * jnp.tanh(0.5 * x) + 0.5


def _to_row_tiles(ref, val):
    rows = val.shape[0]
    for s in range(ROW_TILES):
        ref[pl.ds(s, rows, stride=ROW_TILES), :] = val[:, s * LANES:(s + 1) * LANES]


def _from_row_tiles(ref, first, rows):
    return jnp.concatenate([ref[pl.ds(first * ROW_TILES + s, rows, stride=ROW_TILES), :]
                            for s in range(ROW_TILES)], axis=1)


def _params(*sem):
    return pltpu.CompilerParams(dimension_semantics=sem, vmem_limit_bytes=VMEM_LIMIT)


def _start_key_order(x_hbm, step, dst, sem):
    tiles = dst.shape[0]
    rows = tiles * SUBLANES
    grp = SB_BLOCK // SUBLANES
    for blk in range(rows // SB_BLOCK):
        for i in range(SUBLANES):
            src = pl.ds(pl.multiple_of(step * rows + blk * SB_BLOCK + i * grp, grp), grp)
            pltpu.make_async_copy(x_hbm.at[src], dst.at[pl.ds(blk * grp, grp), i], sem).start()


def _proj_kernel(x_ref, x_hbm, g_ref, w_ref, cos_ref, sin_ref,
                 rq_ref, rk_ref, rv_ref, rg_ref, sq_ref, sk_ref, svt_ref, gr_ref, gs_ref, xp_ref, sem):
    step = pl.program_id(0)
    slot = step % 2

    @pl.when(step == 0)
    def _():
        _start_key_order(x_hbm, step, xp_ref.at[0], sem.at[0])

    @pl.when(step + 1 < pl.num_programs(0))
    def _():
        _start_key_order(x_hbm, step + 1, xp_ref.at[1 - slot], sem.at[1 - slot])

    h = _rms(x_ref[...], g_ref[...]).astype(BF16)
    pltpu.make_async_copy(xp_ref.at[slot], xp_ref.at[slot], sem.at[slot]).wait()
    hp = _rms(xp_ref[slot].reshape(x_ref.shape), g_ref[...]).astype(BF16)
    cos = cos_ref[...]
    sin = sin_ref[...]

    def seg(lo, hi):
        return jnp.dot(h, w_ref[:, lo:hi], preferred_element_type=F32)

    def rope(t):
        outs = []
        for hd in range(RET_HEADS):
            th = t[:, hd * RET_DK:(hd + 1) * RET_DK]
            outs.append(th * cos + pltpu.roll(th, RET_DK // 2, 1) * sin)
        return jnp.concatenate(outs, axis=1)

    rq_ref[...] = rope(seg(0, 512)).astype(BF16)
    rk_ref[...] = (rope(seg(512, 1024)) * (RET_DK ** -0.5)).astype(BF16)
    rv_ref[...] = seg(1024, 1536).astype(BF16)
    rg_ref[...] = seg(1536, 2048).astype(BF16)
    sq_ref[...] = (seg(2048, 2560) * (SB_DH ** -0.5)).astype(BF16)
    sk_ref[...] = jnp.dot(hp, w_ref[:, 2560:3072], preferred_element_type=F32).astype(BF16)
    svt = lax.dot_general(w_ref[:, 3072:3584], hp, (((0,), (1,)), ((), ())), preferred_element_type=F32)
    tk = svt_ref.shape[2]
    for kb in range(svt_ref.shape[0]):
        svt_ref[kb] = svt[:, kb * tk:(kb + 1) * tk].astype(BF16)
    gr_ref[...] = seg(3584, 4608).astype(BF16)
    gs_ref[...] = seg(4608, 5632).astype(BF16)


def _proj(x2, g, w, cosf, sinf, seq):
    n, d = x2.shape
    tm = PROJ_ROWS
    tk = SB_BLOCK
    pos_blocks = seq // tm
    row = lambda i: (i, 0)
    const = lambda i: (0, 0)
    half = jax.ShapeDtypeStruct((n, 512), BF16)
    full = jax.ShapeDtypeStruct((n, 1024), BF16)
    svt = jax.ShapeDtypeStruct((n // tk, 512, tk), BF16)
    hspec = pl.BlockSpec((tm, 512), row)
    fspec = pl.BlockSpec((tm, 1024), row)
    return pl.pallas_call(
        _proj_kernel,
        grid=(n // tm,),
        in_specs=[
            pl.BlockSpec((tm, d), row),
            pl.BlockSpec(memory_space=pl.ANY),
            pl.BlockSpec((1, d), const),
            pl.BlockSpec(w.shape, const),
            pl.BlockSpec((tm, RET_DK), lambda i: (i % pos_blocks, 0)),
            pl.BlockSpec((tm, RET_DK), lambda i: (i % pos_blocks, 0)),
        ],
        out_specs=[hspec] * 6 + [pl.BlockSpec((tm // tk, 512, tk), lambda i: (i, 0, 0)), fspec, fspec],
        out_shape=[half] * 6 + [svt, full, full],
        scratch_shapes=[pltpu.VMEM((2, tm // SUBLANES, SUBLANES, d), F32), pltpu.SemaphoreType.DMA((2,))],
        compiler_params=_params("arbitrary"),
        name="proj",
    )(x2, x2, g, w, cosf, sinf)


def _ret_consts(tb):
    hd = np.arange(RET_HEADS, dtype=np.float64)
    lg = np.log(1.0 - np.exp2(-5.0 - hd))
    idx = np.arange(tb, dtype=np.float64)
    dist = np.abs(idx[:, None] - idx[None, :])
    chunk = (np.arange(tb) // CHUNK)
    allowed = chunk[None, :] <= chunk[:, None]
    dm = np.exp(lg[:, None, None] * dist) * allowed
    qdec = np.exp(lg[:, None] * (idx + 1.0))[:, :, None] * np.ones((1, 1, RET_DK))
    kdec = np.exp(lg[:, None] * (tb - 1.0 - idx))[:, :, None] * np.ones((1, 1, RET_DK))
    cdec = [float(v) for v in np.exp(lg * tb)]
    return (jnp.asarray(dm, F32), jnp.asarray(qdec, F32), jnp.asarray(kdec, F32), cdec)


def _ret_kernel(cdec, q_ref, k_ref, v_ref, g_ref, dm_ref, qd_ref, kd_ref, gn_ref, o_ref, st_ref):
    @pl.when(pl.program_id(1) == 0)
    def _():
        st_ref[...] = jnp.zeros_like(st_ref)

    for hd in range(RET_HEADS):
        sl = slice(hd * RET_DK, (hd + 1) * RET_DK)
        q = q_ref[:, sl]
        k = k_ref[:, sl]
        v = v_ref[:, sl]
        st = st_ref[hd]
        scores = lax.dot_general(q, k, (((1,), (1,)), ((), ())), preferred_element_type=F32)
        scores = scores * dm_ref[hd]
        qd = (q.astype(F32) * qd_ref[hd]).astype(BF16)
        o = (jnp.dot(scores.astype(BF16), v, preferred_element_type=F32)
             + jnp.dot(qd, st.astype(BF16), preferred_element_type=F32))
        kd = (k.astype(F32) * kd_ref[hd]).astype(BF16)
        st_ref[hd] = cdec[hd] * st + lax.dot_general(
            kd, v, (((0,), (0,)), ((), ())), preferred_element_type=F32)
        mu = jnp.mean(o, axis=-1, keepdims=True)
        oc = o - mu
        var = jnp.mean(oc * oc, axis=-1, keepdims=True)
        on = oc * lax.rsqrt(var + EPS) * gn_ref[:, sl]
        gate = g_ref[:, sl].astype(F32)
        o_ref[:, sl] = (gate * _sigmoid(gate) * on).astype(BF16)


def _retention(rq, rk, rv, rg, gn, batch, seq):
    n, w = rq.shape
    tb = RET_ROWS
    nc = seq // tb
    dm, qdec, kdec, cdec = _ret_consts(tb)
    blk = pl.BlockSpec((tb, w), lambda b, c: (b * nc + c, 0))
    c3 = lambda b, c: (0, 0, 0)
    return pl.pallas_call(
        functools.partial(_ret_kernel, cdec),
        grid=(batch, nc),
        in_specs=[blk, blk, blk, blk,
                  pl.BlockSpec(dm.shape, c3), pl.BlockSpec(qdec.shape, c3), pl.BlockSpec(kdec.shape, c3),
                  pl.BlockSpec((1, w), lambda b, c: (0, 0))],
        out_specs=blk,
        out_shape=jax.ShapeDtypeStruct((n, w), BF16),
        scratch_shapes=[pltpu.VMEM((RET_HEADS, RET_DK, RET_DK), F32)],
        compiler_params=_params("parallel", "arbitrary"),
        name="retention",
    )(rq, rk, rv, rg, dm, qdec, kdec, gn)


def _sb_kernel(q_ref, k_ref, vt_ref, o_ref, acc_ref):
    tq = q_ref.shape[0]
    tk = tq
    grp = tk // SUBLANES
    heads = q_ref.shape[1] // SB_DH
    wide = heads * tq
    assert tq & (tq - 1) == 0
    qi = pl.program_id(1)
    per_group = LANES // SB_DH
    lane = lax.broadcasted_iota(I32, (tq, LANES), 1)
    q_groups = []
    for gi in range(heads // per_group):
        qg = q_ref[:, gi * LANES:(gi + 1) * LANES]
        q_groups.append(jnp.concatenate(
            [jnp.where((lane >= hh * SB_DH) & (lane < (hh + 1) * SB_DH), qg, jnp.zeros_like(qg))
             for hh in range(per_group)], axis=0))
    shp3 = (grp, SUBLANES, wide)
    key_pos = lax.broadcasted_iota(I32, shp3, 1) * grp + lax.broadcasted_iota(I32, shp3, 0)
    causal = key_pos < (lax.broadcasted_iota(I32, shp3, 2) & (tq - 1))
    row8 = lax.broadcasted_iota(I32, (SUBLANES, wide), 0)

    def block(kb, carry, diagonal):
        z = jnp.concatenate(
            [lax.dot_general(k_ref[kb, :, gi * LANES:(gi + 1) * LANES], qg, (((1,), (1,)), ((), ())),
                             preferred_element_type=F32) for gi, qg in enumerate(q_groups)],
            axis=1) * LOG2E
        z3 = z.reshape(shp3)
        if diagonal:
            z3 = jnp.where(causal, z3, SB_MASKED)
        run = jnp.ones((SUBLANES, wide), F32)
        local = [None] * grp
        for g in reversed(range(grp)):
            ez = jnp.exp2(jnp.minimum(z3[g], SB_CLAMP))
            run = run * (1.0 / (1.0 + ez))
            local[g] = ez * run
        log_tot = jnp.log2(run)
        off = jnp.where(row8 + 1 < SUBLANES, pltpu.roll(log_tot, SUBLANES - 1, 0), 0.0)
        for d in (1, 2, 4):
            off = off + jnp.where(row8 + d < SUBLANES, pltpu.roll(off, SUBLANES - d, 0), 0.0)
        scale = jnp.exp2(off + carry)
        a3 = jnp.stack(local, axis=0) * scale[None]
        a = a3.reshape(tk, wide).astype(BF16)
        for hh in range(heads):
            rows = slice(hh * SB_DH, (hh + 1) * SB_DH)
            contrib = jnp.dot(vt_ref[kb, rows, :], a[:, hh * tq:(hh + 1) * tq], preferred_element_type=F32)
            if diagonal:
                acc_ref[rows, :] = contrib
            else:
                acc_ref[rows, :] += contrib
        return carry + jnp.broadcast_to(off[0:1, :] + log_tot[0:1, :], carry.shape)

    carry = block(qi, jnp.zeros((SUBLANES, wide), F32), True)

    def cond(c):
        return (c[0] < qi) & (c[1] > 0)

    def body(c):
        new = block(qi - 1 - c[0], c[2], False)
        go = (jnp.max(new) > SB_SKIP_LOG2).astype(I32)
        return (c[0] + 1, go, new)

    lax.while_loop(cond, body, (jnp.int32(0), jnp.int32(1), carry))
    o_ref[...] = acc_ref[...].T.astype(BF16)


def _stick_breaking(sq, kperm, vtperm, batch, seq):
    n, w = sq.shape
    tq = SB_BLOCK
    nq = seq // tq
    return pl.pallas_call(
        _sb_kernel,
        grid=(batch, nq),
        in_specs=[
            pl.BlockSpec((tq, w), lambda b, i: (b * nq + i, 0)),
            pl.BlockSpec((None, nq, tq, w), lambda b, i: (b, 0, 0, 0)),
            pl.BlockSpec((None, nq, w, tq), lambda b, i: (b, 0, 0, 0)),
        ],
        out_specs=pl.BlockSpec((tq, w), lambda b, i: (b * nq + i, 0)),
        out_shape=jax.ShapeDtypeStruct((n, w), BF16),
        scratch_shapes=[pltpu.VMEM((w, tq), F32)],
        compiler_params=_params("parallel", "arbitrary"),
        name="stickbrk",
    )(sq, kperm, vtperm)


def _merge_kernel(ret_ref, sb_ref, gr_ref, gs_ref, x_ref, wbr_ref, wbs_ref, wo_ref, nf_ref,
                  wr_ref, br_ref, tri_ref,
                  x1_ref, h2_ref, ri_ref, rw_ref, cnt_ref, carry_ref):
    @pl.when(pl.program_id(0) == 0)
    def _():
        carry_ref[...] = jnp.zeros_like(carry_ref)

    tm = x_ref.shape[0]
    merged = (_sigmoid(gr_ref[...].astype(F32)) * jnp.dot(ret_ref[...], wbr_ref[...], preferred_element_type=F32)
              + _sigmoid(gs_ref[...].astype(F32)) * jnp.dot(sb_ref[...], wbs_ref[...], preferred_element_type=F32))
    x1 = x_ref[...] + jnp.dot(merged.astype(BF16), wo_ref[...], preferred_element_type=F32)
    x1_ref[...] = x1
    h2 = _rms(x1, nf_ref[...])
    _to_row_tiles(h2_ref, h2)
    logits = jnp.dot(h2.astype(BF16), wr_ref[...], preferred_element_type=F32) + br_ref[...]
    lt = logits.T
    neg = jnp.float32(-jnp.inf)

    grow = lax.broadcasted_iota(I32, (SUBLANES, tm), 0)
    gl = jnp.where(grow < N_GROUPS, lt[0:SUBLANES], neg)
    gmax = jnp.max(gl, axis=0, keepdims=True)
    g_idx = jnp.min(jnp.where(gl == gmax, grow, N_GROUPS), axis=0, keepdims=True)
    p_grp = 1.0 / jnp.sum(jnp.exp(gl - gmax), axis=0, keepdims=True)

    erow = lax.broadcasted_iota(I32, (N_EXPERTS, tm), 0)
    el = lt[EXPERT_COL0:EXPERT_COL0 + N_EXPERTS]
    el = jnp.where((erow >= g_idx * EXPERTS_PER_GROUP) & (erow < (g_idx + 1) * EXPERTS_PER_GROUP), el, neg)
    m1 = jnp.max(el, axis=0, keepdims=True)
    i1 = jnp.min(jnp.where(el == m1, erow, N_EXPERTS), axis=0, keepdims=True)
    el2 = jnp.where(erow == i1, neg, el)
    m2 = jnp.max(el2, axis=0, keepdims=True)
    i2 = jnp.min(jnp.where(el2 == m2, erow, N_EXPERTS), axis=0, keepdims=True)
    e2 = jnp.exp(m2 - m1)
    w1 = p_grp / (1.0 + e2)
    w2 = p_grp * e2 / (1.0 + e2)

    l1 = i1 - g_idx * EXPERTS_PER_GROUP
    l2 = i2 - g_idx * EXPERTS_PER_GROUP
    lo = jnp.minimum(l1, l2)
    hi = jnp.maximum(l1, l2)
    cls = g_idx * PAIRS_PER_GROUP + ((lo * (2 * EXPERTS_PER_GROUP - 1 - lo)) >> 1) + (hi - lo - 1)
    w_lo = jnp.where(l1 < l2, w1, w2)
    w_hi = jnp.where(l1 < l2, w2, w1)

    crow = lax.broadcasted_iota(I32, (CLASS_ROWS, tm), 0)
    sel = crow == cls
    onehot = sel.astype(BF16)
    cinc = jnp.dot(onehot, tri_ref[...], preferred_element_type=F32)
    carry = carry_ref[...]
    rank = jnp.sum(jnp.where(sel, carry[:, 0:1] + cinc - 1.0, 0.0), axis=0, keepdims=True)
    carry_new = carry + jnp.sum(onehot.astype(F32), axis=1, keepdims=True)
    carry_ref[...] = carry_new
    cnt_ref[...] = carry_new

    ri_ref[...] = jnp.zeros_like(ri_ref)
    ri_ref[0:1, :] = cls
    ri_ref[1:2, :] = rank.astype(I32)
    wrow = lax.broadcasted_iota(I32, (ROUTE_COLS, tm), 0)
    wt = jnp.where(wrow == 0, w_lo, jnp.where(wrow == 1, w_hi, 0.0))
    rw_ref[...] = wt.T


def _merge(ret, sb, g_ret, g_sb, x2, wbr, wbs, wo, nf, wr, br):
    n, d = x2.shape
    tm = MERGE_ROWS
    tri = jnp.asarray(np.triu(np.ones((tm, tm), np.float32)), BF16)
    row = lambda i: (i, 0)
    const = lambda i: (0, 0)
    full = lambda a: pl.BlockSpec(a.shape, const)
    return pl.pallas_call(
        _merge_kernel,
        grid=(n // tm,),
        in_specs=[pl.BlockSpec((tm, 512), row), pl.BlockSpec((tm, 512), row),
                  pl.BlockSpec((tm, d), row), pl.BlockSpec((tm, d), row), pl.BlockSpec((tm, d), row),
                  full(wbr), full(wbs), full(wo), full(nf), full(wr), full(br), full(tri)],
        out_specs=[pl.BlockSpec((tm, d), row), pl.BlockSpec((tm * ROW_TILES, LANES), row),
                   pl.BlockSpec((SUBLANES, tm), lambda i: (0, i)),
                   pl.BlockSpec((tm, ROUTE_COLS), row),
                   pl.BlockSpec((CLASS_ROWS, LANES), const)],
        out_shape=[jax.ShapeDtypeStruct((n, d), F32), jax.ShapeDtypeStruct((n * ROW_TILES, LANES), F32),
                   jax.ShapeDtypeStruct((SUBLANES, n), I32),
                   jax.ShapeDtypeStruct((n, ROUTE_COLS), F32),
                   jax.ShapeDtypeStruct((CLASS_ROWS, LANES), F32)],
        scratch_shapes=[pltpu.VMEM((CLASS_ROWS, LANES), F32)],
        compiler_params=_params("arbitrary"),
        name="merge",
    )(ret, sb, g_ret, g_sb, x2, wbr, wbs, wo, nf, wr, br, tri)


DMA_QUEUES = 2


def _tile_rows(i):
    return pl.ds(pl.multiple_of(i * ROW_TILES, ROW_TILES), ROW_TILES)


def _dispatch_kernel(fill_ref, end_ref, dest_ref, x_ref, zero_ref, dst_hbm, sem):
    block_rows = zero_ref.shape[0]
    i = pl.program_id(0)
    per_step = dest_ref.shape[2]

    def send(i2, c):
        for k in range(DMA_QUEUES):
            r = i2 * DMA_QUEUES + k
            pltpu.make_async_copy(x_ref.at[_tile_rows(r)],
                                  dst_hbm.at[_tile_rows(dest_ref[0, 0, r])], sem).start(priority=k)
        return c
    lax.fori_loop(0, per_step // DMA_QUEUES, send, 0, unroll=4)
    pltpu.make_async_copy(x_ref, dst_hbm.at[pl.ds(0, per_step * ROW_TILES)], sem).wait()

    @pl.when(i == pl.num_programs(0) - 1)
    def _():
        def pad(s):
            return pltpu.make_async_copy(zero_ref.at[pl.ds(0, ROW_TILES)], dst_hbm.at[_tile_rows(s)], sem)

        def unused(b, c):
            blk = pltpu.make_async_copy(
                zero_ref, dst_hbm.at[pl.ds(pl.multiple_of(b * block_rows, block_rows), block_rows)], sem)
            blk.start()
            blk.wait()
            return c
        last = end_ref.shape[0] - 1
        lax.fori_loop(end_ref[last] // (block_rows // ROW_TILES), dst_hbm.shape[0] // block_rows, unused, 0)

        def start(s, c):
            pad(s).start()
            return c

        def wait(s, c):
            pad(s).wait()
            return c

        for cl in range(fill_ref.shape[0]):
            lax.fori_loop(fill_ref[cl], end_ref[cl], start, 0)
        for cl in range(fill_ref.shape[0]):
            lax.fori_loop(fill_ref[cl], end_ref[cl], wait, 0)


def _dispatch(dest, fill, end, h2, slots, tmb):
    n = dest.shape[0]
    per_step = min(DISPATCH_TOKENS, n)
    return pl.pallas_call(
        _dispatch_kernel,
        grid_spec=pltpu.PrefetchScalarGridSpec(
            num_scalar_prefetch=2,
            grid=(n // per_step,),
            in_specs=[pl.BlockSpec((1, 1, per_step), lambda i, f, e: (i, 0, 0), memory_space=pltpu.SMEM),
                      pl.BlockSpec((per_step * ROW_TILES, LANES), lambda i, f, e: (i, 0)),
                      pl.BlockSpec((tmb * ROW_TILES, LANES), lambda i, f, e: (0, 0))],
            out_specs=pl.BlockSpec(memory_space=pl.ANY),
            scratch_shapes=[pltpu.SemaphoreType.DMA(())],
        ),
        out_shape=jax.ShapeDtypeStruct((slots * ROW_TILES, LANES), F32),
        compiler_params=_params("arbitrary"),
        name="dispatch",
    )(fill, end, dest.reshape(n // per_step, 1, per_step), h2, jnp.zeros((tmb * ROW_TILES, LANES), F32))


def _expert_kernel(bl_ref, bh_ref, nu_ref, x_ref,
                   wgl_ref, wul_ref, wdl_ref, wgh_ref, wuh_ref, wdh_ref, o_ref):
    j = pl.program_id(0)
    used = nu_ref[0]
    rows = x_ref.shape[0] // ROW_TILES

    @pl.when(j < used)
    def _():
        xb = _from_row_tiles(x_ref, 0, rows).astype(BF16)

        def ffn(wg_ref, wu_ref, wd_ref):
            gate = jnp.dot(xb, wg_ref[...].astype(BF16), preferred_element_type=F32)
            up = jnp.dot(xb, wu_ref[...].astype(BF16), preferred_element_type=F32)
            hid = (gate * _sigmoid(gate) * up).astype(BF16)
            return jnp.dot(hid, wd_ref[...].astype(BF16), preferred_element_type=F32)

        _to_row_tiles(o_ref.at[0], ffn(wgl_ref, wul_ref, wdl_ref))
        _to_row_tiles(o_ref.at[1], ffn(wgh_ref, wuh_ref, wdh_ref))

    @pl.when(j >= used)
    def _():
        o_ref[...] = jnp.zeros_like(o_ref)


def _experts(block_lo, block_hi, n_used, xs, tmb, wg, wu, wd):
    nb = xs.shape[0] // (tmb * ROW_TILES)
    d = wg.shape[1]
    de = wg.shape[2]
    last = lambda j, nu: jnp.minimum(j, nu[0] - 1)
    lo_map = lambda j, bl, bh, nu: (bl[last(j, nu)], 0, 0)
    hi_map = lambda j, bl, bh, nu: (bh[last(j, nu)], 0, 0)
    return pl.pallas_call(
        _expert_kernel,
        grid_spec=pltpu.PrefetchScalarGridSpec(
            num_scalar_prefetch=3,
            grid=(nb,),
            in_specs=[
                pl.BlockSpec((tmb * ROW_TILES, LANES), lambda j, bl, bh, nu: (last(j, nu), 0)),
                pl.BlockSpec((None, d, de), lo_map),
                pl.BlockSpec((None, d, de), lo_map),
                pl.BlockSpec((None, de, d), lo_map),
                pl.BlockSpec((None, d, de), hi_map),
                pl.BlockSpec((None, d, de), hi_map),
                pl.BlockSpec((None, de, d), hi_map),
            ],
            out_specs=pl.BlockSpec((2, tmb * ROW_TILES, LANES), lambda j, bl, bh, nu: (0, j, 0)),
        ),
        out_shape=jax.ShapeDtypeStruct((2, nb * tmb * ROW_TILES, LANES), F32),
        compiler_params=_params("arbitrary"),
        name="experts",
    )(block_lo, block_hi, n_used, xs, wg, wu, wd, wg, wu, wd)


def _issue_pairs(idx_ref, yb_hbm, dst, sem, rows):
    def body(i, c):
        for k in range(DMA_QUEUES):
            r = i * DMA_QUEUES + k
            pltpu.make_async_copy(yb_hbm.at[:, _tile_rows(idx_ref[0, 0, r]), :], dst.at[:, _tile_rows(r), :],
                                  sem).start(priority=k)
        return c
    lax.fori_loop(0, rows // DMA_QUEUES, body, 0, unroll=4)


def _final_kernel(idx_ref, idxn_ref, x1_ref, rw_ref, p_ref, yb_hbm, npl_ref, wpg_ref, wp_ref, nfin_ref,
                  o_ref, gbuf, sem, x2_ref):
    i = pl.program_id(0)
    nt = pl.num_programs(0)
    tm = x1_ref.shape[0]
    slot = i % 2

    @pl.when(i == 0)
    def _():
        _issue_pairs(idx_ref, yb_hbm, gbuf.at[0], sem.at[0], tm)

    @pl.when(i + 1 < nt)
    def _():
        _issue_pairs(idxn_ref, yb_hbm, gbuf.at[1 - slot], sem.at[1 - slot], tm)

    pltpu.make_async_copy(gbuf.at[slot], gbuf.at[slot], sem.at[slot]).wait()
    w_lo = jnp.broadcast_to(rw_ref[:, 0:1], (tm, LANES))
    w_hi = jnp.broadcast_to(rw_ref[:, 1:2], (tm, LANES))
    for s in range(ROW_TILES):
        cols = slice(s * LANES, (s + 1) * LANES)
        x2_ref[:, cols] = (x1_ref[:, cols]
                           + w_lo * gbuf[slot, 0, pl.ds(s, tm, stride=ROW_TILES), :]
                           + w_hi * gbuf[slot, 1, pl.ds(s, tm, stride=ROW_TILES), :])
    x2 = x2_ref[...]
    hp = _rms(x2, npl_ref[...]).astype(BF16)
    gate = _sigmoid(jnp.dot(hp, wpg_ref[...], preferred_element_type=F32))
    emb = jnp.dot(p_ref[...].astype(BF16), wp_ref[...], preferred_element_type=F32)
    x3 = x2 + gate * emb
    o_ref[...] = _rms(x3, nfin_ref[...])


def _final(dest3, x1, rw, p2, yb, npl, wpg, wp, nfin):
    n, d = x1.shape
    tm = FINAL_ROWS
    nt = n // tm
    row = lambda i: (i, 0)
    const = lambda i: (0, 0)
    full = lambda a: pl.BlockSpec(a.shape, const)
    return pl.pallas_call(
        _final_kernel,
        grid=(nt,),
        in_specs=[
            pl.BlockSpec((1, 1, tm), lambda i: (i, 0, 0), memory_space=pltpu.SMEM),
            pl.BlockSpec((1, 1, tm), lambda i: (jnp.minimum(i + 1, nt - 1), 0, 0), memory_space=pltpu.SMEM),
            pl.BlockSpec((tm, d), row),
            pl.BlockSpec((tm, ROUTE_COLS), row),
            pl.BlockSpec((tm, p2.shape[1]), row),
            pl.BlockSpec(memory_space=pl.ANY),
            full(npl), full(wpg), full(wp), full(nfin),
        ],
        out_specs=pl.BlockSpec((tm, d), row),
        out_shape=jax.ShapeDtypeStruct((n, d), F32),
        scratch_shapes=[pltpu.VMEM((2, 2, tm * ROW_TILES, LANES), F32), pltpu.SemaphoreType.DMA((2,)),
                        pltpu.VMEM((tm, d), F32)],
        compiler_params=_params("arbitrary"),
        name="final",
    )(dest3, dest3, x1, rw, p2, yb, npl, wpg, wp, nfin)


def _layer(x2, p2, batch, seq, cosf, sinf, norm_mix, w_in, ret_gn, w_br_ret, w_br_sb, w_out, norm_ffn,
           w_grp, b_grp, w_exp, b_exp, w_gate, w_up, w_down, norm_ple, w_ple, w_ple_gate, out_gain):
    n, d = x2.shape
    rq, rk, rv, rg, sq, skp, svt, g_ret, g_sb = _proj(x2, norm_mix[None, :], w_in.astype(BF16), cosf, sinf, seq)
    ret = _retention(rq, rk, rv, rg, ret_gn[None, :], batch, seq)

    tk = SB_BLOCK
    nkb = seq // tk
    kperm = skp.reshape(batch, nkb, tk, skp.shape[1])
    vtperm = svt.reshape(batch, nkb, svt.shape[1], tk)
    sb = _stick_breaking(sq, kperm, vtperm, batch, seq)

    wr = jnp.zeros((d, ROUTE_COLS), F32)
    wr = wr.at[:, 0:N_GROUPS].set(w_grp).at[:, EXPERT_COL0:EXPERT_COL0 + N_EXPERTS].set(w_exp)
    br = jnp.zeros((1, ROUTE_COLS), F32)
    br = br.at[0, 0:N_GROUPS].set(b_grp).at[0, EXPERT_COL0:EXPERT_COL0 + N_EXPERTS].set(b_exp)
    x1, h2, ri, rw, cnt = _merge(ret, sb, g_ret, g_sb, x2, w_br_ret.astype(BF16), w_br_sb.astype(BF16),
                                 w_out.astype(BF16), norm_ffn[None, :], wr.astype(BF16), br)

    tmb = EXPERT_ROWS
    counts = cnt[:N_CLASSES, 0].astype(I32)
    padded = (counts + tmb - 1) // tmb * tmb
    pad_end = jnp.cumsum(padded)
    pad_start = pad_end - padded
    cids = jnp.arange(N_CLASSES, dtype=I32)
    dest = jnp.sum(jnp.where(ri[0, :, None] == cids, pad_start, 0), axis=-1) + ri[1]
    nb = n // tmb + N_CLASSES
    block_c = jnp.minimum(
        jnp.sum((pad_end[None, :] <= (jnp.arange(nb, dtype=I32) * tmb)[:, None]).astype(I32), axis=1),
        N_CLASSES - 1)
    pairs = [(lo, hi) for lo in range(EXPERTS_PER_GROUP) for hi in range(lo + 1, EXPERTS_PER_GROUP)]
    class_lo = np.array([g * EXPERTS_PER_GROUP + lo for g in range(N_GROUPS) for lo, _ in pairs], np.int32)
    class_hi = np.array([g * EXPERTS_PER_GROUP + hi for g in range(N_GROUPS) for _, hi in pairs], np.int32)
    block_lo = jnp.sum(jnp.where(block_c[:, None] == cids, class_lo, 0), axis=-1).astype(I32)
    block_hi = jnp.sum(jnp.where(block_c[:, None] == cids, class_hi, 0), axis=-1).astype(I32)
    n_used = (pad_end[-1:] // tmb).astype(I32)
    xs = _dispatch(dest, pad_start + counts, pad_end, h2, nb * tmb, tmb)
    yb = _experts(block_lo, block_hi, n_used, xs, tmb, w_gate, w_up, w_down)

    tf = FINAL_ROWS
    return _final(dest.reshape(n // tf, 1, tf), x1, rw, p2, yb, norm_ple[None, :],
                  w_ple_gate.astype(BF16), w_ple.astype(BF16), out_gain)


def kernel(x, p, norm_mix, w_in, ret_gn, w_br_ret, w_br_sb, w_out, norm_ffn, w_grp, b_grp, w_exp, b_exp,
           w_gate, w_up, w_down, norm_ple, w_ple, w_ple_gate, norm_final):
    batch, seq, d = x.shape
    depth = p.shape[0]
    assert depth == 1, "the final RMSNorm is fused into the last layer's kernel"
    pos = jnp.arange(seq, dtype=F32)
    inv_freq = ROPE_BASE ** (-jnp.arange(0, RET_DK, 2, dtype=F32) / RET_DK)
    ang = pos[:, None] * inv_freq[None, :]
    cos, sin = jnp.cos(ang), jnp.sin(ang)
    cosf = jnp.concatenate([cos, cos], axis=1)
    sinf = jnp.concatenate([-sin, sin], axis=1)
    x2 = x.reshape(batch * seq, d)
    i = 0
    out = _layer(x2, p[i].reshape(batch * seq, -1), batch, seq, cosf, sinf, norm_mix[i], w_in[i], ret_gn[i],
                 w_br_ret[i], w_br_sb[i], w_out[i], norm_ffn[i], w_grp[i], b_grp[i], w_exp[i], b_exp[i],
                 w_gate[i], w_up[i], w_down[i], norm_ple[i], w_ple[i], w_ple_gate[i], norm_final[None, :])
    return out.reshape(batch, seq, d)
```

```python
import functools

import numpy as np
import jax
import jax.numpy as jnp
from jax import lax
from jax.experimental import pallas as pl
from jax.experimental.pallas import tpu as pltpu

F32 = jnp.float32
BF16 = jnp.bfloat16
I32 = jnp.int32

EPS = 1e-6
CHUNK = 64
RET_HEADS = 4
RET_DK = 128
SB_HEADS = 8
SB_DH = 64
ROPE_BASE = 10000.0
N_GROUPS = 4
EXPERTS_PER_GROUP = 4
N_EXPERTS = N_GROUPS * EXPERTS_PER_GROUP
PAIRS_PER_GROUP = EXPERTS_PER_GROUP * (EXPERTS_PER_GROUP - 1) // 2
N_CLASSES = N_GROUPS * PAIRS_PER_GROUP
CLASS_ROWS = 32

LANES = 128
SUBLANES = 8
VMEM_LIMIT = 56 * 1024 * 1024

PROJ_ROWS = 512
RET_ROWS = 512
SB_BLOCK = 256
MERGE_ROWS = 512
EXPERT_ROWS = 512
FINAL_ROWS = 512
DISPATCH_TOKENS = 1024
ROUTE_COLS = 128
ROW_TILES = 8
EXPERT_COL0 = 8
SB_SKIP_LOG2 = -160.0
SB_MASKED = -1e30


def _rms(x, g):
    return x * lax.rsqrt(jnp.mean(x * x, axis=-1, keepdims=True) + EPS) * g


def _sigmoid(x):
    return 0.5 * jnp.tanh(0.5 * x) + 0.5


def _to_row_tiles(ref, val):
    rows = val.shape[0]
    for s in range(ROW_TILES):
        ref[pl.ds(s, rows, stride=ROW_TILES), :] = val[:, s * LANES:(s + 1) * LANES]


def _from_row_tiles(ref, first, rows):
    return jnp.concatenate([ref[pl.ds(first * ROW_TILES + s, rows, stride=ROW_TILES), :]
                            for s in range(ROW_TILES)], axis=1)


def _params(*sem):
    return pltpu.CompilerParams(dimension_semantics=sem, vmem_limit_bytes=VMEM_LIMIT)


def _start_key_order(x_hbm, step, dst, sem):
    tiles = dst.shape[0]
    rows = tiles * SUBLANES
    grp = SB_BLOCK // SUBLANES
    for blk in range(rows // SB_BLOCK):
        for i in range(SUBLANES):
            src = pl.ds(pl.multiple_of(step * rows + blk * SB_BLOCK + i * grp, grp), grp)
            pltpu.make_async_copy(x_hbm.at[src], dst.at[pl.ds(blk * grp, grp), i], sem).start()


def _proj_kernel(x_ref, x_hbm, g_ref, w_ref, cos_ref, sin_ref,
                 rq_ref, rk_ref, rv_ref, rg_ref, sq_ref, sk_ref, svt_ref, gr_ref, gs_ref, xp_ref, sem):
    step = pl.program_id(0)
    slot = step % 2

    @pl.when(step == 0)
    def _():
        _start_key_order(x_hbm, step, xp_ref.at[0], sem.at[0])

    @pl.when(step + 1 < pl.num_programs(0))
    def _():
        _start_key_order(x_hbm, step + 1, xp_ref.at[1 - slot], sem.at[1 - slot])

    h = _rms(x_ref[...], g_ref[...]).astype(BF16)
    pltpu.make_async_copy(xp_ref.at[slot], xp_ref.at[slot], sem.at[slot]).wait()
    hp = _rms(xp_ref[slot].reshape(x_ref.shape), g_ref[...]).astype(BF16)
    cos = cos_ref[...]
    sin = sin_ref[...]

    def seg(lo, hi):
        return jnp.dot(h, w_ref[:, lo:hi], preferred_element_type=F32)

    def rope(t):
        outs = []
        for hd in range(RET_HEADS):
            th = t[:, hd * RET_DK:(hd + 1) * RET_DK]
            outs.append(th * cos + pltpu.roll(th, RET_DK // 2, 1) * sin)
        return jnp.concatenate(outs, axis=1)

    rq_ref[...] = rope(seg(0, 512)).astype(BF16)
    rk_ref[...] = (rope(seg(512, 1024)) * (RET_DK ** -0.5)).astype(BF16)
    rv_ref[...] = seg(1024, 1536).astype(BF16)
    rg_ref[...] = seg(1536, 2048).astype(BF16)
    sq_ref[...] = (seg(2048, 2560) * (0.5 * SB_DH ** -0.5)).astype(BF16)
    sk_ref[...] = jnp.dot(hp, w_ref[:, 2560:3072], preferred_element_type=F32).astype(BF16)
    svt = lax.dot_general(w_ref[:, 3072:3584], hp, (((0,), (1,)), ((), ())), preferred_element_type=F32)
    tk = svt_ref.shape[2]
    for kb in range(svt_ref.shape[0]):
        svt_ref[kb] = svt[:, kb * tk:(kb + 1) * tk].astype(BF16)
    gr_ref[...] = seg(3584, 4608).astype(BF16)
    gs_ref[...] = seg(4608, 5632).astype(BF16)


def _proj(x2, g, w, cosf, sinf, seq):
    n, d = x2.shape
    tm = PROJ_ROWS
    tk = SB_BLOCK
    pos_blocks = seq // tm
    row = lambda i: (i, 0)
    const = lambda i: (0, 0)
    half = jax.ShapeDtypeStruct((n, 512), BF16)
    full = jax.ShapeDtypeStruct((n, 1024), BF16)
    svt = jax.ShapeDtypeStruct((n // tk, 512, tk), BF16)
    hspec = pl.BlockSpec((tm, 512), row)
    fspec = pl.BlockSpec((tm, 1024), row)
    return pl.pallas_call(
        _proj_kernel,
        grid=(n // tm,),
        in_specs=[
            pl.BlockSpec((tm, d), row),
            pl.BlockSpec(memory_space=pl.ANY),
            pl.BlockSpec((1, d), const),
            pl.BlockSpec(w.shape, const),
            pl.BlockSpec((tm, RET_DK), lambda i: (i % pos_blocks, 0)),
            pl.BlockSpec((tm, RET_DK), lambda i: (i % pos_blocks, 0)),
        ],
        out_specs=[hspec] * 6 + [pl.BlockSpec((tm // tk, 512, tk), lambda i: (i, 0, 0)), fspec, fspec],
        out_shape=[half] * 6 + [svt, full, full],
        scratch_shapes=[pltpu.VMEM((2, tm // SUBLANES, SUBLANES, d), F32), pltpu.SemaphoreType.DMA((2,))],
        compiler_params=_params("arbitrary"),
        name="proj",
    )(x2, x2, g, w, cosf, sinf)


def _ret_consts(tb):
    hd = np.arange(RET_HEADS, dtype=np.float64)
    lg = np.log(1.0 - np.exp2(-5.0 - hd))
    idx = np.arange(tb, dtype=np.float64)
    dist = np.abs(idx[:, None] - idx[None, :])
    chunk = (np.arange(tb) // CHUNK)
    allowed = chunk[None, :] <= chunk[:, None]
    dm = np.exp(lg[:, None, None] * dist) * allowed
    qdec = np.exp(lg[:, None] * (idx + 1.0))[:, :, None] * np.ones((1, 1, RET_DK))
    kdec = np.exp(lg[:, None] * (tb - 1.0 - idx))[:, :, None] * np.ones((1, 1, RET_DK))
    cdec = [float(v) for v in np.exp(lg * tb)]
    return (jnp.asarray(dm, F32), jnp.asarray(qdec, F32), jnp.asarray(kdec, F32), cdec)


def _ret_kernel(cdec, q_ref, k_ref, v_ref, g_ref, dm_ref, qd_ref, kd_ref, gn_ref, o_ref, st_ref):
    @pl.when(pl.program_id(1) == 0)
    def _():
        st_ref[...] = jnp.zeros_like(st_ref)

    for hd in range(RET_HEADS):
        sl = slice(hd * RET_DK, (hd + 1) * RET_DK)
        q = q_ref[:, sl]
        k = k_ref[:, sl]
        v = v_ref[:, sl]
        st = st_ref[hd]
        scores = lax.dot_general(q, k, (((1,), (1,)), ((), ())), preferred_element_type=F32)
        scores = scores * dm_ref[hd]
        qd = (q.astype(F32) * qd_ref[hd]).astype(BF16)
        o = (jnp.dot(scores.astype(BF16), v, preferred_element_type=F32)
             + jnp.dot(qd, st.astype(BF16), preferred_element_type=F32))
        kd = (k.astype(F32) * kd_ref[hd]).astype(BF16)
        st_ref[hd] = cdec[hd] * st + lax.dot_general(
            kd, v, (((0,), (0,)), ((), ())), preferred_element_type=F32)
        mu = jnp.mean(o, axis=-1, keepdims=True)
        oc = o - mu
        var = jnp.mean(oc * oc, axis=-1, keepdims=True)
        on = oc * lax.rsqrt(var + EPS) * gn_ref[:, sl]
        gate = g_ref[:, sl].astype(F32)
        o_ref[:, sl] = (gate * _sigmoid(gate) * on).astype(BF16)


def _retention(rq, rk, rv, rg, gn, batch, seq):
    n, w = rq.shape
    tb = RET_ROWS
    nc = seq // tb
    dm, qdec, kdec, cdec = _ret_consts(tb)
    blk = pl.BlockSpec((tb, w), lambda b, c: (b * nc + c, 0))
    c3 = lambda b, c: (0, 0, 0)
    return pl.pallas_call(
        functools.partial(_ret_kernel, cdec),
        grid=(batch, nc),
        in_specs=[blk, blk, blk, blk,
                  pl.BlockSpec(dm.shape, c3), pl.BlockSpec(qdec.shape, c3), pl.BlockSpec(kdec.shape, c3),
                  pl.BlockSpec((1, w), lambda b, c: (0, 0))],
        out_specs=blk,
        out_shape=jax.ShapeDtypeStruct((n, w), BF16),
        scratch_shapes=[pltpu.VMEM((RET_HEADS, RET_DK, RET_DK), F32)],
        compiler_params=_params("parallel", "arbitrary"),
        name="retention",
    )(rq, rk, rv, rg, dm, qdec, kdec, gn)


def _sb_kernel(q_ref, k_ref, vt_ref, o_ref, acc_ref):
    tq = q_ref.shape[0]
    tk = tq
    grp = tk // SUBLANES
    heads = q_ref.shape[1] // SB_DH
    wide = heads * tq
    assert tq & (tq - 1) == 0
    qi = pl.program_id(1)
    per_group = LANES // SB_DH
    lane = lax.broadcasted_iota(I32, (tq, LANES), 1)
    q_groups = []
    for gi in range(heads // per_group):
        qg = q_ref[:, gi * LANES:(gi + 1) * LANES]
        q_groups.append(jnp.concatenate(
            [jnp.where((lane >= hh * SB_DH) & (lane < (hh + 1) * SB_DH), qg, jnp.zeros_like(qg))
             for hh in range(per_group)], axis=0))
    shp3 = (grp, SUBLANES, wide)
    key_pos = lax.broadcasted_iota(I32, shp3, 1) * grp + lax.broadcasted_iota(I32, shp3, 0)
    causal = key_pos < (lax.broadcasted_iota(I32, shp3, 2) & (tq - 1))
    row8 = lax.broadcasted_iota(I32, (SUBLANES, wide), 0)

    def block(kb, carry, diagonal):
        z = jnp.concatenate(
            [lax.dot_general(k_ref[kb, :, gi * LANES:(gi + 1) * LANES], qg, (((1,), (1,)), ((), ())),
                             preferred_element_type=F32) for gi, qg in enumerate(q_groups)],
            axis=1)
        z3 = z.reshape(shp3)
        if diagonal:
            z3 = jnp.where(causal, z3, SB_MASKED)
        run = jnp.ones((SUBLANES, wide), F32)
        local = [None] * grp
        for g in reversed(range(grp)):
            half = 0.5 * jnp.tanh(z3[g])
            local[g] = (0.5 + half) * run
            run = run * (0.5 - half)
        log_tot = jnp.log2(run)
        off = jnp.where(row8 + 1 < SUBLANES, pltpu.roll(log_tot, SUBLANES - 1, 0), 0.0)
        for d in (1, 2, 4):
            off = off + jnp.where(row8 + d < SUBLANES, pltpu.roll(off, SUBLANES - d, 0), 0.0)
        scale = jnp.exp2(off + carry)
        a3 = jnp.stack(local, axis=0) * scale[None]
        a = a3.reshape(tk, wide).astype(BF16)
        for hh in range(heads):
            rows = slice(hh * SB_DH, (hh + 1) * SB_DH)
            contrib = jnp.dot(vt_ref[kb, rows, :], a[:, hh * tq:(hh + 1) * tq], preferred_element_type=F32)
            if diagonal:
                acc_ref[rows, :] = contrib
            else:
                acc_ref[rows, :] += contrib
        return carry + jnp.broadcast_to(off[0:1, :] + log_tot[0:1, :], carry.shape)

    carry = block(qi, jnp.zeros((SUBLANES, wide), F32), True)

    def cond(c):
        return (c[0] < qi) & (c[1] > 0)

    def body(c):
        new = block(qi - 1 - c[0], c[2], False)
        go = (jnp.max(new) > SB_SKIP_LOG2).astype(I32)
        return (c[0] + 1, go, new)

    lax.while_loop(cond, body, (jnp.int32(0), jnp.int32(1), carry))
    o_ref[...] = acc_ref[...].T.astype(BF16)


def _stick_breaking(sq, kperm, vtperm, batch, seq):
    n, w = sq.shape
    tq = SB_BLOCK
    nq = seq // tq
    return pl.pallas_call(
        _sb_kernel,
        grid=(batch, nq),
        in_specs=[
            pl.BlockSpec((tq, w), lambda b, i: (b * nq + i, 0)),
            pl.BlockSpec((None, nq, tq, w), lambda b, i: (b, 0, 0, 0)),
            pl.BlockSpec((None, nq, w, tq), lambda b, i: (b, 0, 0, 0)),
        ],
        out_specs=pl.BlockSpec((tq, w), lambda b, i: (b * nq + i, 0)),
        out_shape=jax.ShapeDtypeStruct((n, w), BF16),
        scratch_shapes=[pltpu.VMEM((w, tq), F32)],
        compiler_params=_params("parallel", "arbitrary"),
        name="stickbrk",
    )(sq, kperm, vtperm)


def _merge_kernel(ret_ref, sb_ref, gr_ref, gs_ref, x_ref, wbr_ref, wbs_ref, wo_ref, nf_ref,
                  wr_ref, br_ref, tri_ref,
                  x1_ref, h2_ref, ri_ref, rw_ref, cnt_ref, carry_ref):
    @pl.when(pl.program_id(0) == 0)
    def _():
        carry_ref[...] = jnp.zeros_like(carry_ref)

    tm = x_ref.shape[0]
    merged = (_sigmoid(gr_ref[...].astype(F32)) * jnp.dot(ret_ref[...], wbr_ref[...], preferred_element_type=F32)
              + _sigmoid(gs_ref[...].astype(F32)) * jnp.dot(sb_ref[...], wbs_ref[...], preferred_element_type=F32))
    x1 = x_ref[...] + jnp.dot(merged.astype(BF16), wo_ref[...], preferred_element_type=F32)
    x1_ref[...] = x1
    h2 = _rms(x1, nf_ref[...])
    _to_row_tiles(h2_ref, h2)
    logits = jnp.dot(h2.astype(BF16), wr_ref[...], preferred_element_type=F32) + br_ref[...]
    lt = logits.T
    neg = jnp.float32(-jnp.inf)

    grow = lax.broadcasted_iota(I32, (SUBLANES, tm), 0)
    gl = jnp.where(grow < N_GROUPS, lt[0:SUBLANES], neg)
    gmax = jnp.max(gl, axis=0, keepdims=True)
    g_idx = jnp.min(jnp.where(gl == gmax, grow, N_GROUPS), axis=0, keepdims=True)
    p_grp = 1.0 / jnp.sum(jnp.exp(gl - gmax), axis=0, keepdims=True)

    erow = lax.broadcasted_iota(I32, (N_EXPERTS, tm), 0)
    el = lt[EXPERT_COL0:EXPERT_COL0 + N_EXPERTS]
    el = jnp.where((erow >= g_idx * EXPERTS_PER_GROUP) & (erow < (g_idx + 1) * EXPERTS_PER_GROUP), el, neg)
    m1 = jnp.max(el, axis=0, keepdims=True)
    i1 = jnp.min(jnp.where(el == m1, erow, N_EXPERTS), axis=0, keepdims=True)
    el2 = jnp.where(erow == i1, neg, el)
    m2 = jnp.max(el2, axis=0, keepdims=True)
    i2 = jnp.min(jnp.where(el2 == m2, erow, N_EXPERTS), axis=0, keepdims=True)
    e2 = jnp.exp(m2 - m1)
    w1 = p_grp / (1.0 + e2)
    w2 = p_grp * e2 / (1.0 + e2)

    l1 = i1 - g_idx * EXPERTS_PER_GROUP
    l2 = i2 - g_idx * EXPERTS_PER_GROUP
    lo = jnp.minimum(l1, l2)
    hi = jnp.maximum(l1, l2)
    cls = g_idx * PAIRS_PER_GROUP + ((lo * (2 * EXPERTS_PER_GROUP - 1 - lo)) >> 1) + (hi - lo - 1)
    w_lo = jnp.where(l1 < l2, w1, w2)
    w_hi = jnp.where(l1 < l2, w2, w1)

    crow = lax.broadcasted_iota(I32, (CLASS_ROWS, tm), 0)
    sel = crow == cls
    onehot = sel.astype(BF16)
    cinc = jnp.dot(onehot, tri_ref[...], preferred_element_type=F32)
    carry = carry_ref[...]
    rank = jnp.sum(jnp.where(sel, carry[:, 0:1] + cinc - 1.0, 0.0), axis=0, keepdims=True)
    carry_new = carry + jnp.sum(onehot.astype(F32), axis=1, keepdims=True)
    carry_ref[...] = carry_new
    cnt_ref[...] = carry_new

    ri_ref[...] = jnp.zeros_like(ri_ref)
    ri_ref[0:1, :] = cls
    ri_ref[1:2, :] = rank.astype(I32)
    wrow = lax.broadcasted_iota(I32, (ROUTE_COLS, tm), 0)
    wt = jnp.where(wrow == 0, w_lo, jnp.where(wrow == 1, w_hi, 0.0))
    rw_ref[...] = wt.T


def _merge(ret, sb, g_ret, g_sb, x2, wbr, wbs, wo, nf, wr, br):
    n, d = x2.shape
    tm = MERGE_ROWS
    tri = jnp.asarray(np.triu(np.ones((tm, tm), np.float32)), BF16)
    row = lambda i: (i, 0)
    const = lambda i: (0, 0)
    full = lambda a: pl.BlockSpec(a.shape, const)
    return pl.pallas_call(
        _merge_kernel,
        grid=(n // tm,),
        in_specs=[pl.BlockSpec((tm, 512), row), pl.BlockSpec((tm, 512), row),
                  pl.BlockSpec((tm, d), row), pl.BlockSpec((tm, d), row), pl.BlockSpec((tm, d), row),
                  full(wbr), full(wbs), full(wo), full(nf), full(wr), full(br), full(tri)],
        out_specs=[pl.BlockSpec((tm, d), row), pl.BlockSpec((tm * ROW_TILES, LANES), row),
                   pl.BlockSpec((SUBLANES, tm), lambda i: (0, i)),
                   pl.BlockSpec((tm, ROUTE_COLS), row),
                   pl.BlockSpec((CLASS_ROWS, LANES), const)],
        out_shape=[jax.ShapeDtypeStruct((n, d), F32), jax.ShapeDtypeStruct((n * ROW_TILES, LANES), F32),
                   jax.ShapeDtypeStruct((SUBLANES, n), I32),
                   jax.ShapeDtypeStruct((n, ROUTE_COLS), F32),
                   jax.ShapeDtypeStruct((CLASS_ROWS, LANES), F32)],
        scratch_shapes=[pltpu.VMEM((CLASS_ROWS, LANES), F32)],
        compiler_params=_params("arbitrary"),
        name="merge",
    )(ret, sb, g_ret, g_sb, x2, wbr, wbs, wo, nf, wr, br, tri)


DMA_QUEUES = 2


def _tile_rows(i):
    return pl.ds(pl.multiple_of(i * ROW_TILES, ROW_TILES), ROW_TILES)


def _dispatch_kernel(fill_ref, end_ref, dest_ref, x_ref, zero_ref, dst_hbm, sem):
    block_rows = zero_ref.shape[0]
    i = pl.program_id(0)
    per_step = dest_ref.shape[2]

    def send(i2, c):
        for k in range(DMA_QUEUES):
            r = i2 * DMA_QUEUES + k
            pltpu.make_async_copy(x_ref.at[_tile_rows(r)],
                                  dst_hbm.at[_tile_rows(dest_ref[0, 0, r])], sem).start(priority=k)
        return c
    lax.fori_loop(0, per_step // DMA_QUEUES, send, 0, unroll=4)
    pltpu.make_async_copy(x_ref, dst_hbm.at[pl.ds(0, per_step * ROW_TILES)], sem).wait()

    @pl.when(i == pl.num_programs(0) - 1)
    def _():
        def pad(s):
            return pltpu.make_async_copy(zero_ref.at[pl.ds(0, ROW_TILES)], dst_hbm.at[_tile_rows(s)], sem)

        def unused(b, c):
            blk = pltpu.make_async_copy(
                zero_ref, dst_hbm.at[pl.ds(pl.multiple_of(b * block_rows, block_rows), block_rows)], sem)
            blk.start()
            blk.wait()
            return c
        last = end_ref.shape[0] - 1
        lax.fori_loop(end_ref[last] // (block_rows // ROW_TILES), dst_hbm.shape[0] // block_rows, unused, 0)

        def start(s, c):
            pad(s).start()
            return c

        def wait(s, c):
            pad(s).wait()
            return c

        for cl in range(fill_ref.shape[0]):
            lax.fori_loop(fill_ref[cl], end_ref[cl], start, 0)
        for cl in range(fill_ref.shape[0]):
            lax.fori_loop(fill_ref[cl], end_ref[cl], wait, 0)


def _dispatch(dest, fill, end, h2, slots, tmb):
    n = dest.shape[0]
    per_step = min(DISPATCH_TOKENS, n)
    return pl.pallas_call(
        _dispatch_kernel,
        grid_spec=pltpu.PrefetchScalarGridSpec(
            num_scalar_prefetch=2,
            grid=(n // per_step,),
            in_specs=[pl.BlockSpec((1, 1, per_step), lambda i, f, e: (i, 0, 0), memory_space=pltpu.SMEM),
                      pl.BlockSpec((per_step * ROW_TILES, LANES), lambda i, f, e: (i, 0)),
                      pl.BlockSpec((tmb * ROW_TILES, LANES), lambda i, f, e: (0, 0))],
            out_specs=pl.BlockSpec(memory_space=pl.ANY),
            scratch_shapes=[pltpu.SemaphoreType.DMA(())],
        ),
        out_shape=jax.ShapeDtypeStruct((slots * ROW_TILES, LANES), F32),
        compiler_params=_params("arbitrary"),
        name="dispatch",
    )(fill, end, dest.reshape(n // per_step, 1, per_step), h2, jnp.zeros((tmb * ROW_TILES, LANES), F32))


def _expert_kernel(bl_ref, bh_ref, nu_ref, x_ref,
                   wgl_ref, wul_ref, wdl_ref, wgh_ref, wuh_ref, wdh_ref, o_ref):
    j = pl.program_id(0)
    used = nu_ref[0]
    rows = x_ref.shape[0] // ROW_TILES

    @pl.when(j < used)
    def _():
        xb = _from_row_tiles(x_ref, 0, rows).astype(BF16)

        def ffn(wg_ref, wu_ref, wd_ref):
            gate = jnp.dot(xb, wg_ref[...].astype(BF16), preferred_element_type=F32)
            up = jnp.dot(xb, wu_ref[...].astype(BF16), preferred_element_type=F32)
            hid = (gate * _sigmoid(gate) * up).astype(BF16)
            return jnp.dot(hid, wd_ref[...].astype(BF16), preferred_element_type=F32)

        _to_row_tiles(o_ref.at[0], ffn(wgl_ref, wul_ref, wdl_ref))
        _to_row_tiles(o_ref.at[1], ffn(wgh_ref, wuh_ref, wdh_ref))

    @pl.when(j >= used)
    def _():
        o_ref[...] = jnp.zeros_like(o_ref)


def _experts(block_lo, block_hi, n_used, xs, tmb, wg, wu, wd):
    nb = xs.shape[0] // (tmb * ROW_TILES)
    d = wg.shape[1]
    de = wg.shape[2]
    last = lambda j, nu: jnp.minimum(j, nu[0] - 1)
    lo_map = lambda j, bl, bh, nu: (bl[last(j, nu)], 0, 0)
    hi_map = lambda j, bl, bh, nu: (bh[last(j, nu)], 0, 0)
    return pl.pallas_call(
        _expert_kernel,
        grid_spec=pltpu.PrefetchScalarGridSpec(
            num_scalar_prefetch=3,
            grid=(nb,),
            in_specs=[
                pl.BlockSpec((tmb * ROW_TILES, LANES), lambda j, bl, bh, nu: (last(j, nu), 0)),
                pl.BlockSpec((None, d, de), lo_map),
                pl.BlockSpec((None, d, de), lo_map),
                pl.BlockSpec((None, de, d), lo_map),
                pl.BlockSpec((None, d, de), hi_map),
                pl.BlockSpec((None, d, de), hi_map),
                pl.BlockSpec((None, de, d), hi_map),
            ],
            out_specs=pl.BlockSpec((2, tmb * ROW_TILES, LANES), lambda j, bl, bh, nu: (0, j, 0)),
        ),
        out_shape=jax.ShapeDtypeStruct((2, nb * tmb * ROW_TILES, LANES), F32),
        compiler_params=_params("arbitrary"),
        name="experts",
    )(block_lo, block_hi, n_used, xs, wg, wu, wd, wg, wu, wd)


def _issue_pairs(idx_ref, yb_hbm, dst, sem, rows):
    def body(i, c):
        for k in range(DMA_QUEUES):
            r = i * DMA_QUEUES + k
            pltpu.make_async_copy(yb_hbm.at[:, _tile_rows(idx_ref[0, 0, r]), :], dst.at[:, _tile_rows(r), :],
                                  sem).start(priority=k)
        return c
    lax.fori_loop(0, rows // DMA_QUEUES, body, 0, unroll=4)


def _final_kernel(idx_ref, idxn_ref, x1_ref, rw_ref, p_ref, yb_hbm, npl_ref, wpg_ref, wp_ref, nfin_ref,
                  o_ref, gbuf, sem, x2_ref):
    i = pl.program_id(0)
    nt = pl.num_programs(0)
    tm = x1_ref.shape[0]
    slot = i % 2

    @pl.when(i == 0)
    def _():
        _issue_pairs(idx_ref, yb_hbm, gbuf.at[0], sem.at[0], tm)

    @pl.when(i + 1 < nt)
    def _():
        _issue_pairs(idxn_ref, yb_hbm, gbuf.at[1 - slot], sem.at[1 - slot], tm)

    pltpu.make_async_copy(gbuf.at[slot], gbuf.at[slot], sem.at[slot]).wait()
    w_lo = jnp.broadcast_to(rw_ref[:, 0:1], (tm, LANES))
    w_hi = jnp.broadcast_to(rw_ref[:, 1:2], (tm, LANES))
    for s in range(ROW_TILES):
        cols = slice(s * LANES, (s + 1) * LANES)
        x2_ref[:, cols] = (x1_ref[:, cols]
                           + w_lo * gbuf[slot, 0, pl.ds(s, tm, stride=ROW_TILES), :]
                           + w_hi * gbuf[slot, 1, pl.ds(s, tm, stride=ROW_TILES), :])
    x2 = x2_ref[...]
    hp = _rms(x2, npl_ref[...]).astype(BF16)
    gate = _sigmoid(jnp.dot(hp, wpg_ref[...], preferred_element_type=F32))
    emb = jnp.dot(p_ref[...].astype(BF16), wp_ref[...], preferred_element_type=F32)
    x3 = x2 + gate * emb
    o_ref[...] = _rms(x3, nfin_ref[...])


def _final(dest3, x1, rw, p2, yb, npl, wpg, wp, nfin):
    n, d = x1.shape
    tm = FINAL_ROWS
    nt = n // tm
    row = lambda i: (i, 0)
    const = lambda i: (0, 0)
    full = lambda a: pl.BlockSpec(a.shape, const)
    return pl.pallas_call(
        _final_kernel,
        grid=(nt,),
        in_specs=[
            pl.BlockSpec((1, 1, tm), lambda i: (i, 0, 0), memory_space=pltpu.SMEM),
            pl.BlockSpec((1, 1, tm), lambda i: (jnp.minimum(i + 1, nt - 1), 0, 0), memory_space=pltpu.SMEM),
            pl.BlockSpec((tm, d), row),
            pl.BlockSpec((tm, ROUTE_COLS), row),
            pl.BlockSpec((tm, p2.shape[1]), row),
            pl.BlockSpec(memory_space=pl.ANY),
            full(npl), full(wpg), full(wp), full(nfin),
        ],
        out_specs=pl.BlockSpec((tm, d), row),
        out_shape=jax.ShapeDtypeStruct((n, d), F32),
        scratch_shapes=[pltpu.VMEM((2, 2, tm * ROW_TILES, LANES), F32), pltpu.SemaphoreType.DMA((2,)),
                        pltpu.VMEM((tm, d), F32)],
        compiler_params=_params("arbitrary"),
        name="final",
    )(dest3, dest3, x1, rw, p2, yb, npl, wpg, wp, nfin)


def _layer(x2, p2, batch, seq, cosf, sinf, norm_mix, w_in, ret_gn, w_br_ret, w_br_sb, w_out, norm_ffn,
           w_grp, b_grp, w_exp, b_exp, w_gate, w_up, w_down, norm_ple, w_ple, w_ple_gate, out_gain):
    n, d = x2.shape
    rq, rk, rv, rg, sq, skp, svt, g_ret, g_sb = _proj(x2, norm_mix[None, :], w_in.astype(BF16), cosf, sinf, seq)
    ret = _retention(rq, rk, rv, rg, ret_gn[None, :], batch, seq)

    tk = SB_BLOCK
    nkb = seq // tk
    kperm = skp.reshape(batch, nkb, tk, skp.shape[1])
    vtperm = svt.reshape(batch, nkb, svt.shape[1], tk)
    sb = _stick_breaking(sq, kperm, vtperm, batch, seq)

    wr = jnp.zeros((d, ROUTE_COLS), F32)
    wr = wr.at[:, 0:N_GROUPS].set(w_grp).at[:, EXPERT_COL0:EXPERT_COL0 + N_EXPERTS].set(w_exp)
    br = jnp.zeros((1, ROUTE_COLS), F32)
    br = br.at[0, 0:N_GROUPS].set(b_grp).at[0, EXPERT_COL0:EXPERT_COL0 + N_EXPERTS].set(b_exp)
    x1, h2, ri, rw, cnt = _merge(ret, sb, g_ret, g_sb, x2, w_br_ret.astype(BF16), w_br_sb.astype(BF16),
                                 w_out.astype(BF16), norm_ffn[None, :], wr.astype(BF16), br)

    tmb = EXPERT_ROWS
    counts = cnt[:N_CLASSES, 0].astype(I32)
    padded = (counts + tmb - 1) // tmb * tmb
    pad_end = jnp.cumsum(padded)
    pad_start = pad_end - padded
    cids = jnp.arange(N_CLASSES, dtype=I32)
    dest = jnp.sum(jnp.where(ri[0, :, None] == cids, pad_start, 0), axis=-1) + ri[1]
    nb = n // tmb + N_CLASSES
    block_c = jnp.minimum(
        jnp.sum((pad_end[None, :] <= (jnp.arange(nb, dtype=I32) * tmb)[:, None]).astype(I32), axis=1),
        N_CLASSES - 1)
    pairs = [(lo, hi) for lo in range(EXPERTS_PER_GROUP) for hi in range(lo + 1, EXPERTS_PER_GROUP)]
    class_lo = np.array([g * EXPERTS_PER_GROUP + lo for g in range(N_GROUPS) for lo, _ in pairs], np.int32)
    class_hi = np.array([g * EXPERTS_PER_GROUP + hi for g in range(N_GROUPS) for _, hi in pairs], np.int32)
    block_lo = jnp.sum(jnp.where(block_c[:, None] == cids, class_lo, 0), axis=-1).astype(I32)
    block_hi = jnp.sum(jnp.where(block_c[:, None] == cids, class_hi, 0), axis=-1).astype(I32)
    n_used = (pad_end[-1:] // tmb).astype(I32)
    xs = _dispatch(dest, pad_start + counts, pad_end, h2, nb * tmb, tmb)
    yb = _experts(block_lo, block_hi, n_used, xs, tmb, w_gate, w_up, w_down)

    tf = FINAL_ROWS
    return _final(dest.reshape(n // tf, 1, tf), x1, rw, p2, yb, norm_ple[None, :],
                  w_ple_gate.astype(BF16), w_ple.astype(BF16), out_gain)


def kernel(x, p, norm_mix, w_in, ret_gn, w_br_ret, w_br_sb, w_out, norm_ffn, w_grp, b_grp, w_exp, b_exp,
           w_gate, w_up, w_down, norm_ple, w_ple, w_ple_gate, norm_final):
    batch, seq, d = x.shape
    depth = p.shape[0]
    assert depth == 1, "the final RMSNorm is fused into the last layer's kernel"
    pos = jnp.arange(seq, dtype=F32)
    inv_freq = ROPE_BASE ** (-jnp.arange(0, RET_DK, 2, dtype=F32) / RET_DK)
    ang = pos[:, None] * inv_freq[None, :]
    cos, sin = jnp.cos(ang), jnp.sin(ang)
    cosf = jnp.concatenate([cos, cos], axis=1)
    sinf = jnp.concatenate([-sin, sin], axis=1)
    x2 = x.reshape(batch * seq, d)
    i = 0
    out = _layer(x2, p[i].reshape(batch * seq, -1), batch, seq, cosf, sinf, norm_mix[i], w_in[i], ret_gn[i],
                 w_br_ret[i], w_br_sb[i], w_out[i], norm_ffn[i], w_grp[i], b_grp[i], w_exp[i], b_exp[i],
                 w_gate[i], w_up[i], w_down[i], norm_ple[i], w_ple[i], w_ple_gate[i], norm_final[None, :])
    return out.reshape(batch, seq, d)
```

```python
import functools

import numpy as np
import jax
import jax.numpy as jnp
from jax import lax
from jax.experimental import pallas as pl
from jax.experimental.pallas import tpu as pltpu

F32 = jnp.float32
BF16 = jnp.bfloat16
I32 = jnp.int32

EPS = 1e-6
CHUNK = 64
RET_HEADS = 4
RET_DK = 128
SB_HEADS = 8
SB_DH = 64
ROPE_BASE = 10000.0
N_GROUPS = 4
EXPERTS_PER_GROUP = 4
N_EXPERTS = N_GROUPS * EXPERTS_PER_GROUP
PAIRS_PER_GROUP = EXPERTS_PER_GROUP * (EXPERTS_PER_GROUP - 1) // 2
N_CLASSES = N_GROUPS * PAIRS_PER_GROUP
CLASS_ROWS = 32

LANES = 128
SUBLANES = 8
VMEM_LIMIT = 56 * 1024 * 1024

PROJ_ROWS = 512
RET_ROWS = 512
SB_BLOCK = 256
SB_STEP_BLOCKS = 2
MERGE_ROWS = 512
EXPERT_ROWS = 512
FINAL_ROWS = 512
DISPATCH_TOKENS = 1024
ROUTE_COLS = 128
ROW_TILES = 8
EXPERT_COL0 = 8
SB_SKIP_LOG2 = -160.0
SB_MASKED = -1e30


def _rms(x, g):
    return x * lax.rsqrt(jnp.mean(x * x, axis=-1, keepdims=True) + EPS) * g


def _sigmoid(x):
    return 0.5 * jnp.tanh(0.5 * x) + 0.5


def _to_row_tiles(ref, val):
    rows = val.shape[0]
    for s in range(ROW_TILES):
        ref[pl.ds(s, rows, stride=ROW_TILES), :] = val[:, s * LANES:(s + 1) * LANES]


def _from_row_tiles(ref, first, rows):
    return jnp.concatenate([ref[pl.ds(first * ROW_TILES + s, rows, stride=ROW_TILES), :]
                            for s in range(ROW_TILES)], axis=1)


def _params(*sem):
    return pltpu.CompilerParams(dimension_semantics=sem, vmem_limit_bytes=VMEM_LIMIT)


def _start_key_order(x_hbm, step, dst, sem):
    tiles = dst.shape[0]
    rows = tiles * SUBLANES
    grp = SB_BLOCK // SUBLANES
    for blk in range(rows // SB_BLOCK):
        for i in range(SUBLANES):
            src = pl.ds(pl.multiple_of(step * rows + blk * SB_BLOCK + i * grp, grp), grp)
            pltpu.make_async_copy(x_hbm.at[src], dst.at[pl.ds(blk * grp, grp), i], sem).start()


def _proj_kernel(x_ref, x_hbm, g_ref, w_ref, cos_ref, sin_ref,
                 rq_ref, rk_ref, rv_ref, rg_ref, sq_ref, sk_ref, svt_ref, gr_ref, gs_ref, xp_ref, sem):
    step = pl.program_id(0)
    slot = step % 2

    @pl.when(step == 0)
    def _():
        _start_key_order(x_hbm, step, xp_ref.at[0], sem.at[0])

    @pl.when(step + 1 < pl.num_programs(0))
    def _():
        _start_key_order(x_hbm, step + 1, xp_ref.at[1 - slot], sem.at[1 - slot])

    h = _rms(x_ref[...], g_ref[...]).astype(BF16)
    pltpu.make_async_copy(xp_ref.at[slot], xp_ref.at[slot], sem.at[slot]).wait()
    hp = _rms(xp_ref[slot].reshape(x_ref.shape), g_ref[...]).astype(BF16)
    cos = cos_ref[...]
    sin = sin_ref[...]

    def seg(lo, hi):
        return jnp.dot(h, w_ref[:, lo:hi], preferred_element_type=F32)

    def rope(t):
        outs = []
        for hd in range(RET_HEADS):
            th = t[:, hd * RET_DK:(hd + 1) * RET_DK]
            outs.append(th * cos + pltpu.roll(th, RET_DK // 2, 1) * sin)
        return jnp.concatenate(outs, axis=1)

    rq_ref[...] = rope(seg(0, 512)).astype(BF16)
    rk_ref[...] = (rope(seg(512, 1024)) * (RET_DK ** -0.5)).astype(BF16)
    rv_ref[...] = seg(1024, 1536).astype(BF16)
    rg_ref[...] = seg(1536, 2048).astype(BF16)
    sq_ref[...] = (seg(2048, 2560) * (0.5 * SB_DH ** -0.5)).astype(BF16)
    sk_ref[...] = jnp.dot(hp, w_ref[:, 2560:3072], preferred_element_type=F32).astype(BF16)
    svt = lax.dot_general(w_ref[:, 3072:3584], hp, (((0,), (1,)), ((), ())), preferred_element_type=F32)
    tk = svt_ref.shape[2]
    for kb in range(svt_ref.shape[0]):
        svt_ref[kb] = svt[:, kb * tk:(kb + 1) * tk].astype(BF16)
    gr_ref[...] = seg(3584, 4608).astype(BF16)
    gs_ref[...] = seg(4608, 5632).astype(BF16)


def _proj(x2, g, w, cosf, sinf, seq):
    n, d = x2.shape
    tm = PROJ_ROWS
    tk = SB_BLOCK
    pos_blocks = seq // tm
    row = lambda i: (i, 0)
    const = lambda i: (0, 0)
    half = jax.ShapeDtypeStruct((n, 512), BF16)
    full = jax.ShapeDtypeStruct((n, 1024), BF16)
    svt = jax.ShapeDtypeStruct((n // tk, 512, tk), BF16)
    hspec = pl.BlockSpec((tm, 512), row)
    fspec = pl.BlockSpec((tm, 1024), row)
    return pl.pallas_call(
        _proj_kernel,
        grid=(n // tm,),
        in_specs=[
            pl.BlockSpec((tm, d), row),
            pl.BlockSpec(memory_space=pl.ANY),
            pl.BlockSpec((1, d), const),
            pl.BlockSpec(w.shape, const),
            pl.BlockSpec((tm, RET_DK), lambda i: (i % pos_blocks, 0)),
            pl.BlockSpec((tm, RET_DK), lambda i: (i % pos_blocks, 0)),
        ],
        out_specs=[hspec] * 6 + [pl.BlockSpec((tm // tk, 512, tk), lambda i: (i, 0, 0)), fspec, fspec],
        out_shape=[half] * 6 + [svt, full, full],
        scratch_shapes=[pltpu.VMEM((2, tm // SUBLANES, SUBLANES, d), F32), pltpu.SemaphoreType.DMA((2,))],
        compiler_params=_params("arbitrary"),
        name="proj",
    )(x2, x2, g, w, cosf, sinf)


def _ret_consts(tb):
    hd = np.arange(RET_HEADS, dtype=np.float64)
    lg = np.log(1.0 - np.exp2(-5.0 - hd))
    idx = np.arange(tb, dtype=np.float64)
    dist = np.abs(idx[:, None] - idx[None, :])
    chunk = (np.arange(tb) // CHUNK)
    allowed = chunk[None, :] <= chunk[:, None]
    dm = np.exp(lg[:, None, None] * dist) * allowed
    qdec = np.exp(lg[:, None] * (idx + 1.0))[:, :, None] * np.ones((1, 1, RET_DK))
    kdec = np.exp(lg[:, None] * (tb - 1.0 - idx))[:, :, None] * np.ones((1, 1, RET_DK))
    cdec = [float(v) for v in np.exp(lg * tb)]
    return (jnp.asarray(dm, F32), jnp.asarray(qdec, F32), jnp.asarray(kdec, F32), cdec)


def _ret_kernel(cdec, q_ref, k_ref, v_ref, g_ref, dm_ref, qd_ref, kd_ref, gn_ref, o_ref, st_ref):
    @pl.when(pl.program_id(1) == 0)
    def _():
        st_ref[...] = jnp.zeros_like(st_ref)

    for hd in range(RET_HEADS):
        sl = slice(hd * RET_DK, (hd + 1) * RET_DK)
        q = q_ref[:, sl]
        k = k_ref[:, sl]
        v = v_ref[:, sl]
        st = st_ref[hd]
        scores = lax.dot_general(q, k, (((1,), (1,)), ((), ())), preferred_element_type=F32)
        scores = scores * dm_ref[hd]
        qd = (q.astype(F32) * qd_ref[hd]).astype(BF16)
        o = (jnp.dot(scores.astype(BF16), v, preferred_element_type=F32)
             + jnp.dot(qd, st.astype(BF16), preferred_element_type=F32))
        kd = (k.astype(F32) * kd_ref[hd]).astype(BF16)
        st_ref[hd] = cdec[hd] * st + lax.dot_general(
            kd, v, (((0,), (0,)), ((), ())), preferred_element_type=F32)
        mu = jnp.mean(o, axis=-1, keepdims=True)
        oc = o - mu
        var = jnp.mean(oc * oc, axis=-1, keepdims=True)
        on = oc * lax.rsqrt(var + EPS) * gn_ref[:, sl]
        gate = g_ref[:, sl].astype(F32)
        o_ref[:, sl] = (gate * _sigmoid(gate) * on).astype(BF16)


def _retention(rq, rk, rv, rg, gn, batch, seq):
    n, w = rq.shape
    tb = RET_ROWS
    nc = seq // tb
    dm, qdec, kdec, cdec = _ret_consts(tb)
    blk = pl.BlockSpec((tb, w), lambda b, c: (b * nc + c, 0))
    c3 = lambda b, c: (0, 0, 0)
    return pl.pallas_call(
        functools.partial(_ret_kernel, cdec),
        grid=(batch, nc),
        in_specs=[blk, blk, blk, blk,
                  pl.BlockSpec(dm.shape, c3), pl.BlockSpec(qdec.shape, c3), pl.BlockSpec(kdec.shape, c3),
                  pl.BlockSpec((1, w), lambda b, c: (0, 0))],
        out_specs=blk,
        out_shape=jax.ShapeDtypeStruct((n, w), BF16),
        scratch_shapes=[pltpu.VMEM((RET_HEADS, RET_DK, RET_DK), F32)],
        compiler_params=_params("parallel", "arbitrary"),
        name="retention",
    )(rq, rk, rv, rg, dm, qdec, kdec, gn)


def _sb_query_block(qi, rows, q_ref, k_ref, vt_ref, o_ref, acc_ref):
    tq = SB_BLOCK
    tk = tq
    grp = tk // SUBLANES
    heads = q_ref.shape[1] // SB_DH
    wide = heads * tq
    assert tq & (tq - 1) == 0
    per_group = LANES // SB_DH
    lane = lax.broadcasted_iota(I32, (tq, LANES), 1)
    q_groups = []
    for gi in range(heads // per_group):
        qg = q_ref[rows, gi * LANES:(gi + 1) * LANES]
        q_groups.append(jnp.concatenate(
            [jnp.where((lane >= hh * SB_DH) & (lane < (hh + 1) * SB_DH), qg, jnp.zeros_like(qg))
             for hh in range(per_group)], axis=0))
    shp3 = (grp, SUBLANES, wide)
    key_pos = lax.broadcasted_iota(I32, shp3, 1) * grp + lax.broadcasted_iota(I32, shp3, 0)
    causal = key_pos < (lax.broadcasted_iota(I32, shp3, 2) & (tq - 1))
    row8 = lax.broadcasted_iota(I32, (SUBLANES, wide), 0)

    def block(kb, carry, diagonal):
        z = jnp.concatenate(
            [lax.dot_general(k_ref[kb, :, gi * LANES:(gi + 1) * LANES], qg, (((1,), (1,)), ((), ())),
                             preferred_element_type=F32) for gi, qg in enumerate(q_groups)],
            axis=1)
        z3 = z.reshape(shp3)
        if diagonal:
            z3 = jnp.where(causal, z3, SB_MASKED)
        run = jnp.ones((SUBLANES, wide), F32)
        local = [None] * grp
        for g in reversed(range(grp)):
            half = 0.5 * jnp.tanh(z3[g])
            local[g] = (0.5 + half) * run
            run = run * (0.5 - half)
        log_tot = jnp.log2(run)
        off = jnp.where(row8 + 1 < SUBLANES, pltpu.roll(log_tot, SUBLANES - 1, 0), 0.0)
        for d in (1, 2, 4):
            off = off + jnp.where(row8 + d < SUBLANES, pltpu.roll(off, SUBLANES - d, 0), 0.0)
        scale = jnp.exp2(off + carry)
        a3 = jnp.stack(local, axis=0) * scale[None]
        a = a3.reshape(tk, wide).astype(BF16)
        for hh in range(heads):
            feat = slice(hh * SB_DH, (hh + 1) * SB_DH)
            contrib = jnp.dot(vt_ref[kb, feat, :], a[:, hh * tq:(hh + 1) * tq], preferred_element_type=F32)
            if diagonal:
                acc_ref[feat, :] = contrib
            else:
                acc_ref[feat, :] += contrib
        return carry + jnp.broadcast_to(off[0:1, :] + log_tot[0:1, :], carry.shape)

    carry = block(qi, jnp.zeros((SUBLANES, wide), F32), True)

    def cond(c):
        return (c[0] < qi) & (c[1] > 0)

    def body(c):
        new = block(qi - 1 - c[0], c[2], False)
        go = (jnp.max(new) > SB_SKIP_LOG2).astype(I32)
        return (c[0] + 1, go, new)

    lax.while_loop(cond, body, (jnp.int32(0), jnp.int32(1), carry))
    o_ref[rows, :] = acc_ref[...].T.astype(BF16)


def _sb_kernel(q_ref, k_ref, vt_ref, o_ref, acc_ref):
    per_step = q_ref.shape[0] // SB_BLOCK

    def query_block(sub, c):
        rows = pl.ds(pl.multiple_of(sub * SB_BLOCK, SB_BLOCK), SB_BLOCK)
        _sb_query_block(pl.program_id(1) * per_step + sub, rows, q_ref, k_ref, vt_ref, o_ref, acc_ref)
        return c

    lax.fori_loop(0, per_step, query_block, 0)


def _stick_breaking(sq, kperm, vtperm, batch, seq):
    n, w = sq.shape
    tq = SB_BLOCK
    nq = seq // tq
    rows = SB_STEP_BLOCKS * tq
    steps = nq // SB_STEP_BLOCKS
    return pl.pallas_call(
        _sb_kernel,
        grid=(batch, steps),
        in_specs=[
            pl.BlockSpec((rows, w), lambda b, i: (b * steps + i, 0)),
            pl.BlockSpec((None, nq, tq, w), lambda b, i: (b, 0, 0, 0)),
            pl.BlockSpec((None, nq, w, tq), lambda b, i: (b, 0, 0, 0)),
        ],
        out_specs=pl.BlockSpec((rows, w), lambda b, i: (b * steps + i, 0)),
        out_shape=jax.ShapeDtypeStruct((n, w), BF16),
        scratch_shapes=[pltpu.VMEM((w, tq), F32)],
        compiler_params=_params("parallel", "arbitrary"),
        name="stickbrk",
    )(sq, kperm, vtperm)


def _merge_kernel(ret_ref, sb_ref, gr_ref, gs_ref, x_ref, wbr_ref, wbs_ref, wo_ref, nf_ref,
                  wr_ref, br_ref, tri_ref,
                  x1_ref, h2_ref, ri_ref, rw_ref, cnt_ref, carry_ref):
    @pl.when(pl.program_id(0) == 0)
    def _():
        carry_ref[...] = jnp.zeros_like(carry_ref)

    tm = x_ref.shape[0]
    merged = (_sigmoid(gr_ref[...].astype(F32)) * jnp.dot(ret_ref[...], wbr_ref[...], preferred_element_type=F32)
              + _sigmoid(gs_ref[...].astype(F32)) * jnp.dot(sb_ref[...], wbs_ref[...], preferred_element_type=F32))
    x1 = x_ref[...] + jnp.dot(merged.astype(BF16), wo_ref[...], preferred_element_type=F32)
    x1_ref[...] = x1
    h2 = _rms(x1, nf_ref[...])
    _to_row_tiles(h2_ref, h2)
    logits = jnp.dot(h2.astype(BF16), wr_ref[...], preferred_element_type=F32) + br_ref[...]
    lt = logits.T
    neg = jnp.float32(-jnp.inf)

    grow = lax.broadcasted_iota(I32, (SUBLANES, tm), 0)
    gl = jnp.where(grow < N_GROUPS, lt[0:SUBLANES], neg)
    gmax = jnp.max(gl, axis=0, keepdims=True)
    g_idx = jnp.min(jnp.where(gl == gmax, grow, N_GROUPS), axis=0, keepdims=True)
    p_grp = 1.0 / jnp.sum(jnp.exp(gl - gmax), axis=0, keepdims=True)

    erow = lax.broadcasted_iota(I32, (N_EXPERTS, tm), 0)
    el = lt[EXPERT_COL0:EXPERT_COL0 + N_EXPERTS]
    el = jnp.where((erow >= g_idx * EXPERTS_PER_GROUP) & (erow < (g_idx + 1) * EXPERTS_PER_GROUP), el, neg)
    m1 = jnp.max(el, axis=0, keepdims=True)
    i1 = jnp.min(jnp.where(el == m1, erow, N_EXPERTS), axis=0, keepdims=True)
    el2 = jnp.where(erow == i1, neg, el)
    m2 = jnp.max(el2, axis=0, keepdims=True)
    i2 = jnp.min(jnp.where(el2 == m2, erow, N_EXPERTS), axis=0, keepdims=True)
    e2 = jnp.exp(m2 - m1)
    w1 = p_grp / (1.0 + e2)
    w2 = p_grp * e2 / (1.0 + e2)

    l1 = i1 - g_idx * EXPERTS_PER_GROUP
    l2 = i2 - g_idx * EXPERTS_PER_GROUP
    lo = jnp.minimum(l1, l2)
    hi = jnp.maximum(l1, l2)
    cls = g_idx * PAIRS_PER_GROUP + ((lo * (2 * EXPERTS_PER_GROUP - 1 - lo)) >> 1) + (hi - lo - 1)
    w_lo = jnp.where(l1 < l2, w1, w2)
    w_hi = jnp.where(l1 < l2, w2, w1)

    crow = lax.broadcasted_iota(I32, (CLASS_ROWS, tm), 0)
    sel = crow == cls
    onehot = sel.astype(BF16)
    cinc = jnp.dot(onehot, tri_ref[...], preferred_element_type=F32)
    carry = carry_ref[...]
    rank = jnp.sum(jnp.where(sel, carry[:, 0:1] + cinc - 1.0, 0.0), axis=0, keepdims=True)
    carry_new = carry + jnp.sum(onehot.astype(F32), axis=1, keepdims=True)
    carry_ref[...] = carry_new
    cnt_ref[...] = carry_new

    ri_ref[...] = jnp.zeros_like(ri_ref)
    ri_ref[0:1, :] = cls
    ri_ref[1:2, :] = rank.astype(I32)
    wrow = lax.broadcasted_iota(I32, (ROUTE_COLS, tm), 0)
    wt = jnp.where(wrow == 0, w_lo, jnp.where(wrow == 1, w_hi, 0.0))
    rw_ref[...] = wt.T


def _merge(ret, sb, g_ret, g_sb, x2, wbr, wbs, wo, nf, wr, br):
    n, d = x2.shape
    tm = MERGE_ROWS
    tri = jnp.asarray(np.triu(np.ones((tm, tm), np.float32)), BF16)
    row = lambda i: (i, 0)
    const = lambda i: (0, 0)
    full = lambda a: pl.BlockSpec(a.shape, const)
    return pl.pallas_call(
        _merge_kernel,
        grid=(n // tm,),
        in_specs=[pl.BlockSpec((tm, 512), row), pl.BlockSpec((tm, 512), row),
                  pl.BlockSpec((tm, d), row), pl.BlockSpec((tm, d), row), pl.BlockSpec((tm, d), row),
                  full(wbr), full(wbs), full(wo), full(nf), full(wr), full(br), full(tri)],
        out_specs=[pl.BlockSpec((tm, d), row), pl.BlockSpec((tm * ROW_TILES, LANES), row),
                   pl.BlockSpec((SUBLANES, tm), lambda i: (0, i)),
                   pl.BlockSpec((tm, ROUTE_COLS), row),
                   pl.BlockSpec((CLASS_ROWS, LANES), const)],
        out_shape=[jax.ShapeDtypeStruct((n, d), F32), jax.ShapeDtypeStruct((n * ROW_TILES, LANES), F32),
                   jax.ShapeDtypeStruct((SUBLANES, n), I32),
                   jax.ShapeDtypeStruct((n, ROUTE_COLS), F32),
                   jax.ShapeDtypeStruct((CLASS_ROWS, LANES), F32)],
        scratch_shapes=[pltpu.VMEM((CLASS_ROWS, LANES), F32)],
        compiler_params=_params("arbitrary"),
        name="merge",
    )(ret, sb, g_ret, g_sb, x2, wbr, wbs, wo, nf, wr, br, tri)


DMA_QUEUES = 2


def _tile_rows(i):
    return pl.ds(pl.multiple_of(i * ROW_TILES, ROW_TILES), ROW_TILES)


def _dispatch_kernel(fill_ref, end_ref, dest_ref, x_ref, zero_ref, dst_hbm, sem):
    block_rows = zero_ref.shape[0]
    i = pl.program_id(0)
    per_step = dest_ref.shape[2]

    def send(i2, c):
        for k in range(DMA_QUEUES):
            r = i2 * DMA_QUEUES + k
            pltpu.make_async_copy(x_ref.at[_tile_rows(r)],
                                  dst_hbm.at[_tile_rows(dest_ref[0, 0, r])], sem).start(priority=k)
        return c
    lax.fori_loop(0, per_step // DMA_QUEUES, send, 0, unroll=4)
    pltpu.make_async_copy(x_ref, dst_hbm.at[pl.ds(0, per_step * ROW_TILES)], sem).wait()

    @pl.when(i == pl.num_programs(0) - 1)
    def _():
        def pad(s):
            return pltpu.make_async_copy(zero_ref.at[pl.ds(0, ROW_TILES)], dst_hbm.at[_tile_rows(s)], sem)

        def unused(b, c):
            blk = pltpu.make_async_copy(
                zero_ref, dst_hbm.at[pl.ds(pl.multiple_of(b * block_rows, block_rows), block_rows)], sem)
            blk.start()
            blk.wait()
            return c
        last = end_ref.shape[0] - 1
        lax.fori_loop(end_ref[last] // (block_rows // ROW_TILES), dst_hbm.shape[0] // block_rows, unused, 0)

        def start(s, c):
            pad(s).start()
            return c

        def wait(s, c):
            pad(s).wait()
            return c

        for cl in range(fill_ref.shape[0]):
            lax.fori_loop(fill_ref[cl], end_ref[cl], start, 0)
        for cl in range(fill_ref.shape[0]):
            lax.fori_loop(fill_ref[cl], end_ref[cl], wait, 0)


def _dispatch(dest, fill, end, h2, slots, tmb):
    n = dest.shape[0]
    per_step = min(DISPATCH_TOKENS, n)
    return pl.pallas_call(
        _dispatch_kernel,
        grid_spec=pltpu.PrefetchScalarGridSpec(
            num_scalar_prefetch=2,
            grid=(n // per_step,),
            in_specs=[pl.BlockSpec((1, 1, per_step), lambda i, f, e: (i, 0, 0), memory_space=pltpu.SMEM),
                      pl.BlockSpec((per_step * ROW_TILES, LANES), lambda i, f, e: (i, 0)),
                      pl.BlockSpec((tmb * ROW_TILES, LANES), lambda i, f, e: (0, 0))],
            out_specs=pl.BlockSpec(memory_space=pl.ANY),
            scratch_shapes=[pltpu.SemaphoreType.DMA(())],
        ),
        out_shape=jax.ShapeDtypeStruct((slots * ROW_TILES, LANES), F32),
        compiler_params=_params("arbitrary"),
        name="dispatch",
    )(fill, end, dest.reshape(n // per_step, 1, per_step), h2, jnp.zeros((tmb * ROW_TILES, LANES), F32))


def _expert_kernel(bl_ref, bh_ref, nu_ref, x_ref,
                   wgl_ref, wul_ref, wdl_ref, wgh_ref, wuh_ref, wdh_ref, o_ref):
    j = pl.program_id(0)
    used = nu_ref[0]
    rows = x_ref.shape[0] // ROW_TILES

    @pl.when(j < used)
    def _():
        xb = _from_row_tiles(x_ref, 0, rows).astype(BF16)

        def ffn(wg_ref, wu_ref, wd_ref):
            gate = jnp.dot(xb, wg_ref[...].astype(BF16), preferred_element_type=F32)
            up = jnp.dot(xb, wu_ref[...].astype(BF16), preferred_element_type=F32)
            hid = (gate * _sigmoid(gate) * up).astype(BF16)
            return jnp.dot(hid, wd_ref[...].astype(BF16), preferred_element_type=F32)

        _to_row_tiles(o_ref.at[0], ffn(wgl_ref, wul_ref, wdl_ref))
        _to_row_tiles(o_ref.at[1], ffn(wgh_ref, wuh_ref, wdh_ref))

    @pl.when(j >= used)
    def _():
        o_ref[...] = jnp.zeros_like(o_ref)


def _experts(block_lo, block_hi, n_used, xs, tmb, wg, wu, wd):
    nb = xs.shape[0] // (tmb * ROW_TILES)
    d = wg.shape[1]
    de = wg.shape[2]
    last = lambda j, nu: jnp.minimum(j, nu[0] - 1)
    lo_map = lambda j, bl, bh, nu: (bl[last(j, nu)], 0, 0)
    hi_map = lambda j, bl, bh, nu: (bh[last(j, nu)], 0, 0)
    return pl.pallas_call(
        _expert_kernel,
        grid_spec=pltpu.PrefetchScalarGridSpec(
            num_scalar_prefetch=3,
            grid=(nb,),
            in_specs=[
                pl.BlockSpec((tmb * ROW_TILES, LANES), lambda j, bl, bh, nu: (last(j, nu), 0)),
                pl.BlockSpec((None, d, de), lo_map),
                pl.BlockSpec((None, d, de), lo_map),
                pl.BlockSpec((None, de, d), lo_map),
                pl.BlockSpec((None, d, de), hi_map),
                pl.BlockSpec((None, d, de), hi_map),
                pl.BlockSpec((None, de, d), hi_map),
            ],
            out_specs=pl.BlockSpec((2, tmb * ROW_TILES, LANES), lambda j, bl, bh, nu: (0, j, 0)),
        ),
        out_shape=jax.ShapeDtypeStruct((2, nb * tmb * ROW_TILES, LANES), F32),
        compiler_params=_params("arbitrary"),
        name="experts",
    )(block_lo, block_hi, n_used, xs, wg, wu, wd, wg, wu, wd)


def _issue_pairs(idx_ref, yb_hbm, dst, sem, rows):
    def body(i, c):
        for k in range(DMA_QUEUES):
            r = i * DMA_QUEUES + k
            pltpu.make_async_copy(yb_hbm.at[:, _tile_rows(idx_ref[0, 0, r]), :], dst.at[:, _tile_rows(r), :],
                                  sem).start(priority=k)
        return c
    lax.fori_loop(0, rows // DMA_QUEUES, body, 0, unroll=4)


def _final_kernel(idx_ref, idxn_ref, x1_ref, rw_ref, p_ref, yb_hbm, npl_ref, wpg_ref, wp_ref, nfin_ref,
                  o_ref, gbuf, sem, x2_ref):
    i = pl.program_id(0)
    nt = pl.num_programs(0)
    tm = x1_ref.shape[0]
    slot = i % 2

    @pl.when(i == 0)
    def _():
        _issue_pairs(idx_ref, yb_hbm, gbuf.at[0], sem.at[0], tm)

    @pl.when(i + 1 < nt)
    def _():
        _issue_pairs(idxn_ref, yb_hbm, gbuf.at[1 - slot], sem.at[1 - slot], tm)

    pltpu.make_async_copy(gbuf.at[slot], gbuf.at[slot], sem.at[slot]).wait()
    w_lo = jnp.broadcast_to(rw_ref[:, 0:1], (tm, LANES))
    w_hi = jnp.broadcast_to(rw_ref[:, 1:2], (tm, LANES))
    for s in range(ROW_TILES):
        cols = slice(s * LANES, (s + 1) * LANES)
        x2_ref[:, cols] = (x1_ref[:, cols]
                           + w_lo * gbuf[slot, 0, pl.ds(s, tm, stride=ROW_TILES), :]
                           + w_hi * gbuf[slot, 1, pl.ds(s, tm, stride=ROW_TILES), :])
    x2 = x2_ref[...]
    hp = _rms(x2, npl_ref[...]).astype(BF16)
    gate = _sigmoid(jnp.dot(hp, wpg_ref[...], preferred_element_type=F32))
    emb = jnp.dot(p_ref[...].astype(BF16), wp_ref[...], preferred_element_type=F32)
    x3 = x2 + gate * emb
    o_ref[...] = _rms(x3, nfin_ref[...])


def _final(dest3, x1, rw, p2, yb, npl, wpg, wp, nfin):
    n, d = x1.shape
    tm = FINAL_ROWS
    nt = n // tm
    row = lambda i: (i, 0)
    const = lambda i: (0, 0)
    full = lambda a: pl.BlockSpec(a.shape, const)
    return pl.pallas_call(
        _final_kernel,
        grid=(nt,),
        in_specs=[
            pl.BlockSpec((1, 1, tm), lambda i: (i, 0, 0), memory_space=pltpu.SMEM),
            pl.BlockSpec((1, 1, tm), lambda i: (jnp.minimum(i + 1, nt - 1), 0, 0), memory_space=pltpu.SMEM),
            pl.BlockSpec((tm, d), row),
            pl.BlockSpec((tm, ROUTE_COLS), row),
            pl.BlockSpec((tm, p2.shape[1]), row),
            pl.BlockSpec(memory_space=pl.ANY),
            full(npl), full(wpg), full(wp), full(nfin),
        ],
        out_specs=pl.BlockSpec((tm, d), row),
        out_shape=jax.ShapeDtypeStruct((n, d), F32),
        scratch_shapes=[pltpu.VMEM((2, 2, tm * ROW_TILES, LANES), F32), pltpu.SemaphoreType.DMA((2,)),
                        pltpu.VMEM((tm, d), F32)],
        compiler_params=_params("arbitrary"),
        name="final",
    )(dest3, dest3, x1, rw, p2, yb, npl, wpg, wp, nfin)


def _layer(x2, p2, batch, seq, cosf, sinf, norm_mix, w_in, ret_gn, w_br_ret, w_br_sb, w_out, norm_ffn,
           w_grp, b_grp, w_exp, b_exp, w_gate, w_up, w_down, norm_ple, w_ple, w_ple_gate, out_gain):
    n, d = x2.shape
    rq, rk, rv, rg, sq, skp, svt, g_ret, g_sb = _proj(x2, norm_mix[None, :], w_in.astype(BF16), cosf, sinf, seq)
    ret = _retention(rq, rk, rv, rg, ret_gn[None, :], batch, seq)

    tk = SB_BLOCK
    nkb = seq // tk
    kperm = skp.reshape(batch, nkb, tk, skp.shape[1])
    vtperm = svt.reshape(batch, nkb, svt.shape[1], tk)
    sb = _stick_breaking(sq, kperm, vtperm, batch, seq)

    wr = jnp.zeros((d, ROUTE_COLS), F32)
    wr = wr.at[:, 0:N_GROUPS].set(w_grp).at[:, EXPERT_COL0:EXPERT_COL0 + N_EXPERTS].set(w_exp)
    br = jnp.zeros((1, ROUTE_COLS), F32)
    br = br.at[0, 0:N_GROUPS].set(b_grp).at[0, EXPERT_COL0:EXPERT_COL0 + N_EXPERTS].set(b_exp)
    x1, h2, ri, rw, cnt = _merge(ret, sb, g_ret, g_sb, x2, w_br_ret.astype(BF16), w_br_sb.astype(BF16),
                                 w_out.astype(BF16), norm_ffn[None, :], wr.astype(BF16), br)

    tmb = EXPERT_ROWS
    counts = cnt[:N_CLASSES, 0].astype(I32)
    padded = (counts + tmb - 1) // tmb * tmb
    pad_end = jnp.cumsum(padded)
    pad_start = pad_end - padded
    cids = jnp.arange(N_CLASSES, dtype=I32)
    dest = jnp.sum(jnp.where(ri[0, :, None] == cids, pad_start, 0), axis=-1) + ri[1]
    nb = n // tmb + N_CLASSES
    block_c = jnp.minimum(
        jnp.sum((pad_end[None, :] <= (jnp.arange(nb, dtype=I32) * tmb)[:, None]).astype(I32), axis=1),
        N_CLASSES - 1)
    pairs = [(lo, hi) for lo in range(EXPERTS_PER_GROUP) for hi in range(lo + 1, EXPERTS_PER_GROUP)]
    class_lo = np.array([g * EXPERTS_PER_GROUP + lo for g in range(N_GROUPS) for lo, _ in pairs], np.int32)
    class_hi = np.array([g * EXPERTS_PER_GROUP + hi for g in range(N_GROUPS) for _, hi in pairs], np.int32)
    block_lo = jnp.sum(jnp.where(block_c[:, None] == cids, class_lo, 0), axis=-1).astype(I32)
    block_hi = jnp.sum(jnp.where(block_c[:, None] == cids, class_hi, 0), axis=-1).astype(I32)
    n_used = (pad_end[-1:] // tmb).astype(I32)
    xs = _dispatch(dest, pad_start + counts, pad_end, h2, nb * tmb, tmb)
    yb = _experts(block_lo, block_hi, n_used, xs, tmb, w_gate, w_up, w_down)

    tf = FINAL_ROWS
    return _final(dest.reshape(n // tf, 1, tf), x1, rw, p2, yb, norm_ple[None, :],
                  w_ple_gate.astype(BF16), w_ple.astype(BF16), out_gain)


def kernel(x, p, norm_mix, w_in, ret_gn, w_br_ret, w_br_sb, w_out, norm_ffn, w_grp, b_grp, w_exp, b_exp,
           w_gate, w_up, w_down, norm_ple, w_ple, w_ple_gate, norm_final):
    batch, seq, d = x.shape
    depth = p.shape[0]
    assert depth == 1, "the final RMSNorm is fused into the last layer's kernel"
    pos = jnp.arange(seq, dtype=F32)
    inv_freq = ROPE_BASE ** (-jnp.arange(0, RET_DK, 2, dtype=F32) / RET_DK)
    ang = pos[:, None] * inv_freq[None, :]
    cos, sin = jnp.cos(ang), jnp.sin(ang)
    cosf = jnp.concatenate([cos, cos], axis=1)
    sinf = jnp.concatenate([-sin, sin], axis=1)
    x2 = x.reshape(batch * seq, d)
    i = 0
    out = _layer(x2, p[i].reshape(batch * seq, -1), batch, seq, cosf, sinf, norm_mix[i], w_in[i], ret_gn[i],
                 w_br_ret[i], w_br_sb[i], w_out[i], norm_ffn[i], w_grp[i], b_grp[i], w_exp[i], b_exp[i],
                 w_gate[i], w_up[i], w_down[i], norm_ple[i], w_ple[i], w_ple_gate[i], norm_final[None, :])
    return out.reshape(batch, seq, d)
```

```python
import functools

import numpy as np
import jax
import jax.numpy as jnp
from jax import lax
from jax.experimental import pallas as pl
from jax.experimental.pallas import tpu as pltpu

F32 = jnp.float32
BF16 = jnp.bfloat16
I32 = jnp.int32

EPS = 1e-6
CHUNK = 64
RET_HEADS = 4
RET_DK = 128
SB_HEADS = 8
SB_DH = 64
ROPE_BASE = 10000.0
N_GROUPS = 4
EXPERTS_PER_GROUP = 4
N_EXPERTS = N_GROUPS * EXPERTS_PER_GROUP
PAIRS_PER_GROUP = EXPERTS_PER_GROUP * (EXPERTS_PER_GROUP - 1) // 2
N_CLASSES = N_GROUPS * PAIRS_PER_GROUP
CLASS_ROWS = 32

LANES = 128
SUBLANES = 8
VMEM_LIMIT = 56 * 1024 * 1024

PROJ_ROWS = 512
RET_ROWS = 512
RET_STEP_BLOCKS = 2
DMA_QUEUES = 2
SB_BLOCK = 256
SB_STEP_BLOCKS = 4
MERGE_ROWS = 512
EXPERT_ROWS = 512
FINAL_ROWS = 512
DISPATCH_TOKENS = 1024
ROUTE_COLS = 128
ROW_TILES = 8
EXPERT_COL0 = 8
SB_SKIP_LOG2 = -160.0
SB_MASKED = -1e30


def _rms(x, g):
    return x * lax.rsqrt(jnp.mean(x * x, axis=-1, keepdims=True) + EPS) * g


def _sigmoid(x):
    return 0.5 * jnp.tanh(0.5 * x) + 0.5


def _to_row_tiles(ref, val):
    rows = val.shape[0]
    for s in range(ROW_TILES):
        ref[pl.ds(s, rows, stride=ROW_TILES), :] = val[:, s * LANES:(s + 1) * LANES]


def _from_row_tiles(ref, first, rows):
    return jnp.concatenate([ref[pl.ds(first * ROW_TILES + s, rows, stride=ROW_TILES), :]
                            for s in range(ROW_TILES)], axis=1)


def _params(*sem):
    return pltpu.CompilerParams(dimension_semantics=sem, vmem_limit_bytes=VMEM_LIMIT)


def _start_key_order(x_hbm, step, dst, sem):
    tiles = dst.shape[0]
    rows = tiles * SUBLANES
    grp = SB_BLOCK // SUBLANES
    for blk in range(rows // SB_BLOCK):
        for i in range(SUBLANES):
            src = pl.ds(pl.multiple_of(step * rows + blk * SB_BLOCK + i * grp, grp), grp)
            pltpu.make_async_copy(x_hbm.at[src], dst.at[pl.ds(blk * grp, grp), i], sem).start()


def _proj_kernel(x_ref, x_hbm, g_ref, w_ref, cos_ref, sin_ref,
                 rq_ref, rk_ref, rv_ref, rg_ref, sq_ref, sk_ref, svt_ref, gr_ref, gs_ref, xp_ref, sem):
    step = pl.program_id(0)
    slot = step % 2

    @pl.when(step == 0)
    def _():
        _start_key_order(x_hbm, step, xp_ref.at[0], sem.at[0])

    @pl.when(step + 1 < pl.num_programs(0))
    def _():
        _start_key_order(x_hbm, step + 1, xp_ref.at[1 - slot], sem.at[1 - slot])

    h = _rms(x_ref[...], g_ref[...]).astype(BF16)
    pltpu.make_async_copy(xp_ref.at[slot], xp_ref.at[slot], sem.at[slot]).wait()
    hp = _rms(xp_ref[slot].reshape(x_ref.shape), g_ref[...]).astype(BF16)
    cos = cos_ref[...]
    sin = sin_ref[...]

    def seg(lo, hi):
        return jnp.dot(h, w_ref[:, lo:hi], preferred_element_type=F32)

    def rope(t):
        outs = []
        for hd in range(RET_HEADS):
            th = t[:, hd * RET_DK:(hd + 1) * RET_DK]
            outs.append(th * cos + pltpu.roll(th, RET_DK // 2, 1) * sin)
        return jnp.concatenate(outs, axis=1)

    rq_ref[...] = rope(seg(0, 512)).astype(BF16)
    rk_ref[...] = (rope(seg(512, 1024)) * (RET_DK ** -0.5)).astype(BF16)
    rv_ref[...] = seg(1024, 1536).astype(BF16)
    rg_ref[...] = seg(1536, 2048).astype(BF16)
    sq_ref[...] = (seg(2048, 2560) * (0.5 * SB_DH ** -0.5)).astype(BF16)
    sk_ref[...] = jnp.dot(hp, w_ref[:, 2560:3072], preferred_element_type=F32).astype(BF16)
    svt = lax.dot_general(w_ref[:, 3072:3584], hp, (((0,), (1,)), ((), ())), preferred_element_type=F32)
    tk = svt_ref.shape[2]
    for kb in range(svt_ref.shape[0]):
        svt_ref[kb] = svt[:, kb * tk:(kb + 1) * tk].astype(BF16)
    gr_ref[...] = seg(3584, 4608).astype(BF16)
    gs_ref[...] = seg(4608, 5632).astype(BF16)


def _proj(x2, g, w, cosf, sinf, seq):
    n, d = x2.shape
    tm = PROJ_ROWS
    tk = SB_BLOCK
    pos_blocks = seq // tm
    row = lambda i: (i, 0)
    const = lambda i: (0, 0)
    half = jax.ShapeDtypeStruct((n, 512), BF16)
    full = jax.ShapeDtypeStruct((n, 1024), BF16)
    svt = jax.ShapeDtypeStruct((n // tk, 512, tk), BF16)
    hspec = pl.BlockSpec((tm, 512), row)
    fspec = pl.BlockSpec((tm, 1024), row)
    return pl.pallas_call(
        _proj_kernel,
        grid=(n // tm,),
        in_specs=[
            pl.BlockSpec((tm, d), row),
            pl.BlockSpec(memory_space=pl.ANY),
            pl.BlockSpec((1, d), const),
            pl.BlockSpec(w.shape, const),
            pl.BlockSpec((tm, RET_DK), lambda i: (i % pos_blocks, 0)),
            pl.BlockSpec((tm, RET_DK), lambda i: (i % pos_blocks, 0)),
        ],
        out_specs=[hspec] * 6 + [pl.BlockSpec((tm // tk, 512, tk), lambda i: (i, 0, 0)), fspec, fspec],
        out_shape=[half] * 6 + [svt, full, full],
        scratch_shapes=[pltpu.VMEM((2, tm // SUBLANES, SUBLANES, d), F32), pltpu.SemaphoreType.DMA((2,))],
        compiler_params=_params("arbitrary"),
        name="proj",
    )(x2, x2, g, w, cosf, sinf)


def _ret_consts(tb):
    hd = np.arange(RET_HEADS, dtype=np.float64)
    lg = np.log(1.0 - np.exp2(-5.0 - hd))
    idx = np.arange(tb, dtype=np.float64)
    dist = np.abs(idx[:, None] - idx[None, :])
    chunk = (np.arange(tb) // CHUNK)
    allowed = chunk[None, :] <= chunk[:, None]
    dm = np.exp(lg[:, None, None] * dist) * allowed
    qdec = np.exp(lg[:, None] * (idx + 1.0))[:, :, None] * np.ones((1, 1, RET_DK))
    kdec = np.exp(lg[:, None] * (tb - 1.0 - idx))[:, :, None] * np.ones((1, 1, RET_DK))
    cdec = [float(v) for v in np.exp(lg * tb)]
    return (jnp.asarray(dm, F32), jnp.asarray(qdec, F32), jnp.asarray(kdec, F32), cdec)


def _ret_kernel(cdec, q_ref, k_ref, v_ref, g_ref, dm_ref, qd_ref, kd_ref, gn_ref, o_ref, st_ref):
    @pl.when(pl.program_id(1) == 0)
    def _():
        st_ref[...] = jnp.zeros_like(st_ref)

    tb = dm_ref.shape[1]

    def block(sub, c):
        rows = pl.ds(pl.multiple_of(sub * tb, tb), tb)
        for hd in range(RET_HEADS):
            sl = slice(hd * RET_DK, (hd + 1) * RET_DK)
            q = q_ref[rows, sl]
            k = k_ref[rows, sl]
            v = v_ref[rows, sl]
            st = st_ref[hd]
            scores = lax.dot_general(q, k, (((1,), (1,)), ((), ())), preferred_element_type=F32)
            scores = scores * dm_ref[hd]
            qd = (q.astype(F32) * qd_ref[hd]).astype(BF16)
            o = (jnp.dot(scores.astype(BF16), v, preferred_element_type=F32)
                 + jnp.dot(qd, st.astype(BF16), preferred_element_type=F32))
            kd = (k.astype(F32) * kd_ref[hd]).astype(BF16)
            st_ref[hd] = cdec[hd] * st + lax.dot_general(
                kd, v, (((0,), (0,)), ((), ())), preferred_element_type=F32)
            mu = jnp.mean(o, axis=-1, keepdims=True)
            oc = o - mu
            var = jnp.mean(oc * oc, axis=-1, keepdims=True)
            on = oc * lax.rsqrt(var + EPS) * gn_ref[:, sl]
            gate = g_ref[rows, sl].astype(F32)
            o_ref[rows, sl] = (gate * _sigmoid(gate) * on).astype(BF16)
        return c

    lax.fori_loop(0, q_ref.shape[0] // tb, block, 0)


def _retention(rq, rk, rv, rg, gn, batch, seq):
    n, w = rq.shape
    tb = RET_ROWS
    nc = seq // tb
    dm, qdec, kdec, cdec = _ret_consts(tb)
    steps = nc // RET_STEP_BLOCKS
    blk = pl.BlockSpec((RET_STEP_BLOCKS * tb, w), lambda b, c: (b * steps + c, 0))
    c3 = lambda b, c: (0, 0, 0)
    return pl.pallas_call(
        functools.partial(_ret_kernel, cdec),
        grid=(batch, steps),
        in_specs=[blk, blk, blk, blk,
                  pl.BlockSpec(dm.shape, c3), pl.BlockSpec(qdec.shape, c3), pl.BlockSpec(kdec.shape, c3),
                  pl.BlockSpec((1, w), lambda b, c: (0, 0))],
        out_specs=blk,
        out_shape=jax.ShapeDtypeStruct((n, w), BF16),
        scratch_shapes=[pltpu.VMEM((RET_HEADS, RET_DK, RET_DK), F32)],
        compiler_params=_params("parallel", "arbitrary"),
        name="retention",
    )(rq, rk, rv, rg, dm, qdec, kdec, gn)


def _sb_query_block(qi, rows, q_ref, k_ref, vt_ref, o_ref, acc_ref):
    tq = SB_BLOCK
    tk = tq
    grp = tk // SUBLANES
    heads = q_ref.shape[1] // SB_DH
    wide = heads * tq
    assert tq & (tq - 1) == 0
    per_group = LANES // SB_DH
    lane = lax.broadcasted_iota(I32, (tq, LANES), 1)
    q_groups = []
    for gi in range(heads // per_group):
        qg = q_ref[rows, gi * LANES:(gi + 1) * LANES]
        q_groups.append(jnp.concatenate(
            [jnp.where((lane >= hh * SB_DH) & (lane < (hh + 1) * SB_DH), qg, jnp.zeros_like(qg))
             for hh in range(per_group)], axis=0))
    shp3 = (grp, SUBLANES, wide)
    key_pos = lax.broadcasted_iota(I32, shp3, 1) * grp + lax.broadcasted_iota(I32, shp3, 0)
    causal = key_pos < (lax.broadcasted_iota(I32, shp3, 2) & (tq - 1))
    row8 = lax.broadcasted_iota(I32, (SUBLANES, wide), 0)

    def block(kb, carry, diagonal):
        z = jnp.concatenate(
            [lax.dot_general(k_ref[kb, :, gi * LANES:(gi + 1) * LANES], qg, (((1,), (1,)), ((), ())),
                             preferred_element_type=F32) for gi, qg in enumerate(q_groups)],
            axis=1)
        z3 = z.reshape(shp3)
        if diagonal:
            z3 = jnp.where(causal, z3, SB_MASKED)
        run = jnp.ones((SUBLANES, wide), F32)
        local = [None] * grp
        for g in reversed(range(grp)):
            half = 0.5 * jnp.tanh(z3[g])
            local[g] = (0.5 + half) * run
            run = run * (0.5 - half)
        log_tot = jnp.log2(run)
        off = jnp.where(row8 + 1 < SUBLANES, pltpu.roll(log_tot, SUBLANES - 1, 0), 0.0)
        for d in (1, 2, 4):
            off = off + jnp.where(row8 + d < SUBLANES, pltpu.roll(off, SUBLANES - d, 0), 0.0)
        scale = jnp.exp2(off + carry)
        a3 = jnp.stack(local, axis=0) * scale[None]
        a = a3.reshape(tk, wide).astype(BF16)
        for hh in range(heads):
            feat = slice(hh * SB_DH, (hh + 1) * SB_DH)
            contrib = jnp.dot(vt_ref[kb, feat, :], a[:, hh * tq:(hh + 1) * tq], preferred_element_type=F32)
            if diagonal:
                acc_ref[feat, :] = contrib
            else:
                acc_ref[feat, :] += contrib
        return carry + jnp.broadcast_to(off[0:1, :] + log_tot[0:1, :], carry.shape)

    carry = block(qi, jnp.zeros((SUBLANES, wide), F32), True)

    def cond(c):
        return (c[0] < qi) & (c[1] > 0)

    def body(c):
        new = block(qi - 1 - c[0], c[2], False)
        go = (jnp.max(new) > SB_SKIP_LOG2).astype(I32)
        return (c[0] + 1, go, new)

    lax.while_loop(cond, body, (jnp.int32(0), jnp.int32(1), carry))
    o_ref[rows, :] = acc_ref[...].T.astype(BF16)


def _sb_kernel(q_ref, k_ref, vt_ref, o_ref, acc_ref):
    per_step = q_ref.shape[0] // SB_BLOCK

    def query_block(sub, c):
        rows = pl.ds(pl.multiple_of(sub * SB_BLOCK, SB_BLOCK), SB_BLOCK)
        _sb_query_block(pl.program_id(1) * per_step + sub, rows, q_ref, k_ref, vt_ref, o_ref, acc_ref)
        return c

    lax.fori_loop(0, per_step, query_block, 0)


def _stick_breaking(sq, kperm, vtperm, batch, seq):
    n, w = sq.shape
    tq = SB_BLOCK
    nq = seq // tq
    rows = SB_STEP_BLOCKS * tq
    steps = nq // SB_STEP_BLOCKS
    return pl.pallas_call(
        _sb_kernel,
        grid=(batch, steps),
        in_specs=[
            pl.BlockSpec((rows, w), lambda b, i: (b * steps + i, 0)),
            pl.BlockSpec((None, nq, tq, w), lambda b, i: (b, 0, 0, 0)),
            pl.BlockSpec((None, nq, w, tq), lambda b, i: (b, 0, 0, 0)),
        ],
        out_specs=pl.BlockSpec((rows, w), lambda b, i: (b * steps + i, 0)),
        out_shape=jax.ShapeDtypeStruct((n, w), BF16),
        scratch_shapes=[pltpu.VMEM((w, tq), F32)],
        compiler_params=_params("parallel", "arbitrary"),
        name="stickbrk",
    )(sq, kperm, vtperm)


def _merge_kernel(ret_ref, sb_ref, gr_ref, gs_ref, x_ref, wbr_ref, wbs_ref, wo_ref, nf_ref,
                  wr_ref, br_ref, tri_ref,
                  x1_ref, h2_ref, ri_ref, rw_ref, cnt_ref, carry_ref):
    @pl.when(pl.program_id(0) == 0)
    def _():
        carry_ref[...] = jnp.zeros_like(carry_ref)

    tm = x_ref.shape[0]
    merged = (_sigmoid(gr_ref[...].astype(F32)) * jnp.dot(ret_ref[...], wbr_ref[...], preferred_element_type=F32)
              + _sigmoid(gs_ref[...].astype(F32)) * jnp.dot(sb_ref[...], wbs_ref[...], preferred_element_type=F32))
    x1 = x_ref[...] + jnp.dot(merged.astype(BF16), wo_ref[...], preferred_element_type=F32)
    x1_ref[...] = x1
    h2 = _rms(x1, nf_ref[...])
    _to_row_tiles(h2_ref, h2)
    logits = jnp.dot(h2.astype(BF16), wr_ref[...], preferred_element_type=F32) + br_ref[...]
    lt = logits.T
    neg = jnp.float32(-jnp.inf)

    grow = lax.broadcasted_iota(I32, (SUBLANES, tm), 0)
    gl = jnp.where(grow < N_GROUPS, lt[0:SUBLANES], neg)
    gmax = jnp.max(gl, axis=0, keepdims=True)
    g_idx = jnp.min(jnp.where(gl == gmax, grow, N_GROUPS), axis=0, keepdims=True)
    p_grp = 1.0 / jnp.sum(jnp.exp(gl - gmax), axis=0, keepdims=True)

    erow = lax.broadcasted_iota(I32, (N_EXPERTS, tm), 0)
    el = lt[EXPERT_COL0:EXPERT_COL0 + N_EXPERTS]
    el = jnp.where((erow >= g_idx * EXPERTS_PER_GROUP) & (erow < (g_idx + 1) * EXPERTS_PER_GROUP), el, neg)
    m1 = jnp.max(el, axis=0, keepdims=True)
    i1 = jnp.min(jnp.where(el == m1, erow, N_EXPERTS), axis=0, keepdims=True)
    el2 = jnp.where(erow == i1, neg, el)
    m2 = jnp.max(el2, axis=0, keepdims=True)
    i2 = jnp.min(jnp.where(el2 == m2, erow, N_EXPERTS), axis=0, keepdims=True)
    e2 = jnp.exp(m2 - m1)
    w1 = p_grp / (1.0 + e2)
    w2 = p_grp * e2 / (1.0 + e2)

    l1 = i1 - g_idx * EXPERTS_PER_GROUP
    l2 = i2 - g_idx * EXPERTS_PER_GROUP
    lo = jnp.minimum(l1, l2)
    hi = jnp.maximum(l1, l2)
    cls = g_idx * PAIRS_PER_GROUP + ((lo * (2 * EXPERTS_PER_GROUP - 1 - lo)) >> 1) + (hi - lo - 1)
    w_lo = jnp.where(l1 < l2, w1, w2)
    w_hi = jnp.where(l1 < l2, w2, w1)

    crow = lax.broadcasted_iota(I32, (CLASS_ROWS, tm), 0)
    sel = crow == cls
    onehot = sel.astype(BF16)
    cinc = jnp.dot(onehot, tri_ref[...], preferred_element_type=F32)
    carry = carry_ref[...]
    rank = jnp.sum(jnp.where(sel, carry[:, 0:1] + cinc - 1.0, 0.0), axis=0, keepdims=True)
    carry_new = carry + jnp.sum(onehot.astype(F32), axis=1, keepdims=True)
    carry_ref[...] = carry_new
    cnt_ref[...] = carry_new

    ri_ref[...] = jnp.zeros_like(ri_ref)
    ri_ref[0:1, :] = cls
    ri_ref[1:2, :] = rank.astype(I32)
    wrow = lax.broadcasted_iota(I32, (ROUTE_COLS, tm), 0)
    wt = jnp.where(wrow == 0, w_lo, jnp.where(wrow == 1, w_hi, 0.0))
    rw_ref[...] = wt.T


def _merge(ret, sb, g_ret, g_sb, x2, wbr, wbs, wo, nf, wr, br):
    n, d = x2.shape
    tm = MERGE_ROWS
    tri = jnp.asarray(np.triu(np.ones((tm, tm), np.float32)), BF16)
    row = lambda i: (i, 0)
    const = lambda i: (0, 0)
    full = lambda a: pl.BlockSpec(a.shape, const)
    return pl.pallas_call(
        _merge_kernel,
        grid=(n // tm,),
        in_specs=[pl.BlockSpec((tm, 512), row), pl.BlockSpec((tm, 512), row),
                  pl.BlockSpec((tm, d), row), pl.BlockSpec((tm, d), row), pl.BlockSpec((tm, d), row),
                  full(wbr), full(wbs), full(wo), full(nf), full(wr), full(br), full(tri)],
        out_specs=[pl.BlockSpec((tm, d), row), pl.BlockSpec((tm * ROW_TILES, LANES), row),
                   pl.BlockSpec((SUBLANES, tm), lambda i: (0, i)),
                   pl.BlockSpec((tm, ROUTE_COLS), row),
                   pl.BlockSpec((CLASS_ROWS, LANES), const)],
        out_shape=[jax.ShapeDtypeStruct((n, d), F32), jax.ShapeDtypeStruct((n * ROW_TILES, LANES), F32),
                   jax.ShapeDtypeStruct((SUBLANES, n), I32),
                   jax.ShapeDtypeStruct((n, ROUTE_COLS), F32),
                   jax.ShapeDtypeStruct((CLASS_ROWS, LANES), F32)],
        scratch_shapes=[pltpu.VMEM((CLASS_ROWS, LANES), F32)],
        compiler_params=_params("arbitrary"),
        name="merge",
    )(ret, sb, g_ret, g_sb, x2, wbr, wbs, wo, nf, wr, br, tri)


def _tile_rows(i):
    return pl.ds(pl.multiple_of(i * ROW_TILES, ROW_TILES), ROW_TILES)


def _dispatch_kernel(fill_ref, end_ref, dest_ref, x_ref, zero_ref, dst_hbm, sem):
    block_rows = zero_ref.shape[0]
    i = pl.program_id(0)
    per_step = dest_ref.shape[2]

    def send(i2, c):
        for k in range(DMA_QUEUES):
            r = i2 * DMA_QUEUES + k
            pltpu.make_async_copy(x_ref.at[_tile_rows(r)],
                                  dst_hbm.at[_tile_rows(dest_ref[0, 0, r])], sem).start(priority=k)
        return c
    lax.fori_loop(0, per_step // DMA_QUEUES, send, 0, unroll=4)
    pltpu.make_async_copy(x_ref, dst_hbm.at[pl.ds(0, per_step * ROW_TILES)], sem).wait()

    @pl.when(i == pl.num_programs(0) - 1)
    def _():
        def pad(s):
            return pltpu.make_async_copy(zero_ref.at[pl.ds(0, ROW_TILES)], dst_hbm.at[_tile_rows(s)], sem)

        def unused(b, c):
            blk = pltpu.make_async_copy(
                zero_ref, dst_hbm.at[pl.ds(pl.multiple_of(b * block_rows, block_rows), block_rows)], sem)
            blk.start()
            blk.wait()
            return c
        last = end_ref.shape[0] - 1
        lax.fori_loop(end_ref[last] // (block_rows // ROW_TILES), dst_hbm.shape[0] // block_rows, unused, 0)

        def start(s, c):
            pad(s).start()
            return c

        def wait(s, c):
            pad(s).wait()
            return c

        for cl in range(fill_ref.shape[0]):
            lax.fori_loop(fill_ref[cl], end_ref[cl], start, 0)
        for cl in range(fill_ref.shape[0]):
            lax.fori_loop(fill_ref[cl], end_ref[cl], wait, 0)


def _dispatch(dest, fill, end, h2, slots, tmb):
    n = dest.shape[0]
    per_step = min(DISPATCH_TOKENS, n)
    return pl.pallas_call(
        _dispatch_kernel,
        grid_spec=pltpu.PrefetchScalarGridSpec(
            num_scalar_prefetch=2,
            grid=(n // per_step,),
            in_specs=[pl.BlockSpec((1, 1, per_step), lambda i, f, e: (i, 0, 0), memory_space=pltpu.SMEM),
                      pl.BlockSpec((per_step * ROW_TILES, LANES), lambda i, f, e: (i, 0)),
                      pl.BlockSpec((tmb * ROW_TILES, LANES), lambda i, f, e: (0, 0))],
            out_specs=pl.BlockSpec(memory_space=pl.ANY),
            scratch_shapes=[pltpu.SemaphoreType.DMA(())],
        ),
        out_shape=jax.ShapeDtypeStruct((slots * ROW_TILES, LANES), F32),
        compiler_params=_params("arbitrary"),
        name="dispatch",
    )(fill, end, dest.reshape(n // per_step, 1, per_step), h2, jnp.zeros((tmb * ROW_TILES, LANES), F32))


def _expert_kernel(bl_ref, bh_ref, nu_ref, x_ref,
                   wgl_ref, wul_ref, wdl_ref, wgh_ref, wuh_ref, wdh_ref, o_ref):
    j = pl.program_id(0)
    used = nu_ref[0]
    rows = x_ref.shape[0] // ROW_TILES

    @pl.when(j < used)
    def _():
        xb = _from_row_tiles(x_ref, 0, rows).astype(BF16)

        def ffn(wg_ref, wu_ref, wd_ref):
            gate = jnp.dot(xb, wg_ref[...].astype(BF16), preferred_element_type=F32)
            up = jnp.dot(xb, wu_ref[...].astype(BF16), preferred_element_type=F32)
            hid = (gate * _sigmoid(gate) * up).astype(BF16)
            return jnp.dot(hid, wd_ref[...].astype(BF16), preferred_element_type=F32)

        _to_row_tiles(o_ref.at[0], ffn(wgl_ref, wul_ref, wdl_ref))
        _to_row_tiles(o_ref.at[1], ffn(wgh_ref, wuh_ref, wdh_ref))

    @pl.when(j >= used)
    def _():
        o_ref[...] = jnp.zeros_like(o_ref)


def _experts(block_lo, block_hi, n_used, xs, tmb, wg, wu, wd):
    nb = xs.shape[0] // (tmb * ROW_TILES)
    d = wg.shape[1]
    de = wg.shape[2]
    last = lambda j, nu: jnp.minimum(j, nu[0] - 1)
    lo_map = lambda j, bl, bh, nu: (bl[last(j, nu)], 0, 0)
    hi_map = lambda j, bl, bh, nu: (bh[last(j, nu)], 0, 0)
    return pl.pallas_call(
        _expert_kernel,
        grid_spec=pltpu.PrefetchScalarGridSpec(
            num_scalar_prefetch=3,
            grid=(nb,),
            in_specs=[
                pl.BlockSpec((tmb * ROW_TILES, LANES), lambda j, bl, bh, nu: (last(j, nu), 0)),
                pl.BlockSpec((None, d, de), lo_map),
                pl.BlockSpec((None, d, de), lo_map),
                pl.BlockSpec((None, de, d), lo_map),
                pl.BlockSpec((None, d, de), hi_map),
                pl.BlockSpec((None, d, de), hi_map),
                pl.BlockSpec((None, de, d), hi_map),
            ],
            out_specs=pl.BlockSpec((2, tmb * ROW_TILES, LANES), lambda j, bl, bh, nu: (0, j, 0)),
        ),
        out_shape=jax.ShapeDtypeStruct((2, nb * tmb * ROW_TILES, LANES), F32),
        compiler_params=_params("arbitrary"),
        name="experts",
    )(block_lo, block_hi, n_used, xs, wg, wu, wd, wg, wu, wd)


def _issue_pairs(idx_ref, yb_hbm, dst, sem, rows):
    def body(i, c):
        for k in range(DMA_QUEUES):
            r = i * DMA_QUEUES + k
            pltpu.make_async_copy(yb_hbm.at[:, _tile_rows(idx_ref[0, 0, r]), :], dst.at[:, _tile_rows(r), :],
                                  sem).start(priority=k)
        return c
    lax.fori_loop(0, rows // DMA_QUEUES, body, 0, unroll=4)


def _final_kernel(idx_ref, idxn_ref, x1_ref, rw_ref, p_ref, yb_hbm, npl_ref, wpg_ref, wp_ref, nfin_ref,
                  o_ref, gbuf, sem, x2_ref):
    i = pl.program_id(0)
    nt = pl.num_programs(0)
    tm = x1_ref.shape[0]
    slot = i % 2

    @pl.when(i == 0)
    def _():
        _issue_pairs(idx_ref, yb_hbm, gbuf.at[0], sem.at[0], tm)

    @pl.when(i + 1 < nt)
    def _():
        _issue_pairs(idxn_ref, yb_hbm, gbuf.at[1 - slot], sem.at[1 - slot], tm)

    pltpu.make_async_copy(gbuf.at[slot], gbuf.at[slot], sem.at[slot]).wait()
    w_lo = jnp.broadcast_to(rw_ref[:, 0:1], (tm, LANES))
    w_hi = jnp.broadcast_to(rw_ref[:, 1:2], (tm, LANES))
    for s in range(ROW_TILES):
        cols = slice(s * LANES, (s + 1) * LANES)
        x2_ref[:, cols] = (x1_ref[:, cols]
                           + w_lo * gbuf[slot, 0, pl.ds(s, tm, stride=ROW_TILES), :]
                           + w_hi * gbuf[slot, 1, pl.ds(s, tm, stride=ROW_TILES), :])
    x2 = x2_ref[...]
    hp = _rms(x2, npl_ref[...]).astype(BF16)
    gate = _sigmoid(jnp.dot(hp, wpg_ref[...], preferred_element_type=F32))
    emb = jnp.dot(p_ref[...].astype(BF16), wp_ref[...], preferred_element_type=F32)
    x3 = x2 + gate * emb
    o_ref[...] = _rms(x3, nfin_ref[...])


def _final(dest3, x1, rw, p2, yb, npl, wpg, wp, nfin):
    n, d = x1.shape
    tm = FINAL_ROWS
    nt = n // tm
    row = lambda i: (i, 0)
    const = lambda i: (0, 0)
    full = lambda a: pl.BlockSpec(a.shape, const)
    return pl.pallas_call(
        _final_kernel,
        grid=(nt,),
        in_specs=[
            pl.BlockSpec((1, 1, tm), lambda i: (i, 0, 0), memory_space=pltpu.SMEM),
            pl.BlockSpec((1, 1, tm), lambda i: (jnp.minimum(i + 1, nt - 1), 0, 0), memory_space=pltpu.SMEM),
            pl.BlockSpec((tm, d), row),
            pl.BlockSpec((tm, ROUTE_COLS), row),
            pl.BlockSpec((tm, p2.shape[1]), row),
            pl.BlockSpec(memory_space=pl.ANY),
            full(npl), full(wpg), full(wp), full(nfin),
        ],
        out_specs=pl.BlockSpec((tm, d), row),
        out_shape=jax.ShapeDtypeStruct((n, d), F32),
        scratch_shapes=[pltpu.VMEM((2, 2, tm * ROW_TILES, LANES), F32), pltpu.SemaphoreType.DMA((2,)),
                        pltpu.VMEM((tm, d), F32)],
        compiler_params=_params("arbitrary"),
        name="final",
    )(dest3, dest3, x1, rw, p2, yb, npl, wpg, wp, nfin)


def _layer(x2, p2, batch, seq, cosf, sinf, norm_mix, w_in, ret_gn, w_br_ret, w_br_sb, w_out, norm_ffn,
           w_grp, b_grp, w_exp, b_exp, w_gate, w_up, w_down, norm_ple, w_ple, w_ple_gate, out_gain):
    n, d = x2.shape
    rq, rk, rv, rg, sq, skp, svt, g_ret, g_sb = _proj(x2, norm_mix[None, :], w_in.astype(BF16), cosf, sinf, seq)
    ret = _retention(rq, rk, rv, rg, ret_gn[None, :], batch, seq)

    tk = SB_BLOCK
    nkb = seq // tk
    kperm = skp.reshape(batch, nkb, tk, skp.shape[1])
    vtperm = svt.reshape(batch, nkb, svt.shape[1], tk)
    sb = _stick_breaking(sq, kperm, vtperm, batch, seq)

    wr = jnp.zeros((d, ROUTE_COLS), F32)
    wr = wr.at[:, 0:N_GROUPS].set(w_grp).at[:, EXPERT_COL0:EXPERT_COL0 + N_EXPERTS].set(w_exp)
    br = jnp.zeros((1, ROUTE_COLS), F32)
    br = br.at[0, 0:N_GROUPS].set(b_grp).at[0, EXPERT_COL0:EXPERT_COL0 + N_EXPERTS].set(b_exp)
    x1, h2, ri, rw, cnt = _merge(ret, sb, g_ret, g_sb, x2, w_br_ret.astype(BF16), w_br_sb.astype(BF16),
                                 w_out.astype(BF16), norm_ffn[None, :], wr.astype(BF16), br)

    tmb = EXPERT_ROWS
    counts = cnt[:N_CLASSES, 0].astype(I32)
    padded = (counts + tmb - 1) // tmb * tmb
    pad_end = jnp.cumsum(padded)
    pad_start = pad_end - padded
    cids = jnp.arange(N_CLASSES, dtype=I32)
    dest = jnp.sum(jnp.where(ri[0, :, None] == cids, pad_start, 0), axis=-1) + ri[1]
    nb = n // tmb + N_CLASSES
    block_c = jnp.minimum(
        jnp.sum((pad_end[None, :] <= (jnp.arange(nb, dtype=I32) * tmb)[:, None]).astype(I32), axis=1),
        N_CLASSES - 1)
    pairs = [(lo, hi) for lo in range(EXPERTS_PER_GROUP) for hi in range(lo + 1, EXPERTS_PER_GROUP)]
    class_lo = np.array([g * EXPERTS_PER_GROUP + lo for g in range(N_GROUPS) for lo, _ in pairs], np.int32)
    class_hi = np.array([g * EXPERTS_PER_GROUP + hi for g in range(N_GROUPS) for _, hi in pairs], np.int32)
    block_lo = jnp.sum(jnp.where(block_c[:, None] == cids, class_lo, 0), axis=-1).astype(I32)
    block_hi = jnp.sum(jnp.where(block_c[:, None] == cids, class_hi, 0), axis=-1).astype(I32)
    n_used = (pad_end[-1:] // tmb).astype(I32)
    xs = _dispatch(dest, pad_start + counts, pad_end, h2, nb * tmb, tmb)
    yb = _experts(block_lo, block_hi, n_used, xs, tmb, w_gate, w_up, w_down)

    tf = FINAL_ROWS
    return _final(dest.reshape(n // tf, 1, tf), x1, rw, p2, yb, norm_ple[None, :],
                  w_ple_gate.astype(BF16), w_ple.astype(BF16), out_gain)


def kernel(x, p, norm_mix, w_in, ret_gn, w_br_ret, w_br_sb, w_out, norm_ffn, w_grp, b_grp, w_exp, b_exp,
           w_gate, w_up, w_down, norm_ple, w_ple, w_ple_gate, norm_final):
    batch, seq, d = x.shape
    depth = p.shape[0]
    assert depth == 1, "the final RMSNorm is fused into the last layer's kernel"
    pos = jnp.arange(seq, dtype=F32)
    inv_freq = ROPE_BASE ** (-jnp.arange(0, RET_DK, 2, dtype=F32) / RET_DK)
    ang = pos[:, None] * inv_freq[None, :]
    cos, sin = jnp.cos(ang), jnp.sin(ang)
    cosf = jnp.concatenate([cos, cos], axis=1)
    sinf = jnp.concatenate([-sin, sin], axis=1)
    x2 = x.reshape(batch * seq, d)
    i = 0
    out = _layer(x2, p[i].reshape(batch * seq, -1), batch, seq, cosf, sinf, norm_mix[i], w_in[i], ret_gn[i],
                 w_br_ret[i], w_br_sb[i], w_out[i], norm_ffn[i], w_grp[i], b_grp[i], w_exp[i], b_exp[i],
                 w_gate[i], w_up[i], w_down[i], norm_ple[i], w_ple[i], w_ple_gate[i], norm_final[None, :])
    return out.reshape(batch, seq, d)
```

```python
import functools

import numpy as np
import jax
import jax.numpy as jnp
from jax import lax
from jax.experimental import pallas as pl
from jax.experimental.pallas import tpu as pltpu

F32 = jnp.float32
BF16 = jnp.bfloat16
I32 = jnp.int32

EPS = 1e-6
CHUNK = 64
RET_HEADS = 4
RET_DK = 128
SB_HEADS = 8
SB_DH = 64
ROPE_BASE = 10000.0
N_GROUPS = 4
EXPERTS_PER_GROUP = 4
N_EXPERTS = N_GROUPS * EXPERTS_PER_GROUP
PAIRS_PER_GROUP = EXPERTS_PER_GROUP * (EXPERTS_PER_GROUP - 1) // 2
N_CLASSES = N_GROUPS * PAIRS_PER_GROUP
CLASS_ROWS = 32

LANES = 128
SUBLANES = 8
VMEM_LIMIT = 56 * 1024 * 1024

PROJ_ROWS = 512
RET_ROWS = 512
RET_STEP_BLOCKS = 2
DMA_QUEUES = 2
SB_BLOCK = 256
SB_STEP_BLOCKS = 4
MERGE_ROWS = 512
EXPERT_ROWS = 512
FINAL_ROWS = 1024
DISPATCH_TOKENS = 1024
ROUTE_COLS = 128
ROW_TILES = 8
EXPERT_COL0 = 8
SB_SKIP_LOG2 = -160.0
SB_MASKED = -1e30


def _rms(x, g):
    return x * lax.rsqrt(jnp.mean(x * x, axis=-1, keepdims=True) + EPS) * g


def _sigmoid(x):
    return 0.5 * jnp.tanh(0.5 * x) + 0.5


def _to_row_tiles(ref, val):
    rows = val.shape[0]
    for s in range(ROW_TILES):
        ref[pl.ds(s, rows, stride=ROW_TILES), :] = val[:, s * LANES:(s + 1) * LANES]


def _from_row_tiles(ref, first, rows):
    return jnp.concatenate([ref[pl.ds(first * ROW_TILES + s, rows, stride=ROW_TILES), :]
                            for s in range(ROW_TILES)], axis=1)


def _params(*sem):
    return pltpu.CompilerParams(dimension_semantics=sem, vmem_limit_bytes=VMEM_LIMIT)


def _start_key_order(x_hbm, step, dst, sem):
    tiles = dst.shape[0]
    rows = tiles * SUBLANES
    grp = SB_BLOCK // SUBLANES
    for blk in range(rows // SB_BLOCK):
        for i in range(SUBLANES):
            src = pl.ds(pl.multiple_of(step * rows + blk * SB_BLOCK + i * grp, grp), grp)
            pltpu.make_async_copy(x_hbm.at[src], dst.at[pl.ds(blk * grp, grp), i], sem).start()


def _proj_kernel(x_ref, x_hbm, g_ref, w_ref, cos_ref, sin_ref,
                 rq_ref, rk_ref, rv_ref, rg_ref, sq_ref, sk_ref, svt_ref, gr_ref, gs_ref, xp_ref, sem):
    step = pl.program_id(0)
    slot = step % 2

    @pl.when(step == 0)
    def _():
        _start_key_order(x_hbm, step, xp_ref.at[0], sem.at[0])

    @pl.when(step + 1 < pl.num_programs(0))
    def _():
        _start_key_order(x_hbm, step + 1, xp_ref.at[1 - slot], sem.at[1 - slot])

    h = _rms(x_ref[...], g_ref[...]).astype(BF16)
    pltpu.make_async_copy(xp_ref.at[slot], xp_ref.at[slot], sem.at[slot]).wait()
    hp = _rms(xp_ref[slot].reshape(x_ref.shape), g_ref[...]).astype(BF16)
    cos = cos_ref[...]
    sin = sin_ref[...]

    def seg(lo, hi):
        return jnp.dot(h, w_ref[:, lo:hi], preferred_element_type=F32)

    def rope(t):
        outs = []
        for hd in range(RET_HEADS):
            th = t[:, hd * RET_DK:(hd + 1) * RET_DK]
            outs.append(th * cos + pltpu.roll(th, RET_DK // 2, 1) * sin)
        return jnp.concatenate(outs, axis=1)

    rq_ref[...] = rope(seg(0, 512)).astype(BF16)
    rk_ref[...] = (rope(seg(512, 1024)) * (RET_DK ** -0.5)).astype(BF16)
    rv_ref[...] = seg(1024, 1536).astype(BF16)
    rg_ref[...] = seg(1536, 2048).astype(BF16)
    sq_ref[...] = (seg(2048, 2560) * (0.5 * SB_DH ** -0.5)).astype(BF16)
    sk_ref[...] = jnp.dot(hp, w_ref[:, 2560:3072], preferred_element_type=F32).astype(BF16)
    svt = lax.dot_general(w_ref[:, 3072:3584], hp, (((0,), (1,)), ((), ())), preferred_element_type=F32)
    tk = svt_ref.shape[2]
    for kb in range(svt_ref.shape[0]):
        svt_ref[kb] = svt[:, kb * tk:(kb + 1) * tk].astype(BF16)
    gr_ref[...] = seg(3584, 4608).astype(BF16)
    gs_ref[...] = seg(4608, 5632).astype(BF16)


def _proj(x2, g, w, cosf, sinf, seq):
    n, d = x2.shape
    tm = PROJ_ROWS
    tk = SB_BLOCK
    pos_blocks = seq // tm
    row = lambda i: (i, 0)
    const = lambda i: (0, 0)
    half = jax.ShapeDtypeStruct((n, 512), BF16)
    full = jax.ShapeDtypeStruct((n, 1024), BF16)
    svt = jax.ShapeDtypeStruct((n // tk, 512, tk), BF16)
    hspec = pl.BlockSpec((tm, 512), row)
    fspec = pl.BlockSpec((tm, 1024), row)
    return pl.pallas_call(
        _proj_kernel,
        grid=(n // tm,),
        in_specs=[
            pl.BlockSpec((tm, d), row),
            pl.BlockSpec(memory_space=pl.ANY),
            pl.BlockSpec((1, d), const),
            pl.BlockSpec(w.shape, const),
            pl.BlockSpec((tm, RET_DK), lambda i: (i % pos_blocks, 0)),
            pl.BlockSpec((tm, RET_DK), lambda i: (i % pos_blocks, 0)),
        ],
        out_specs=[hspec] * 6 + [pl.BlockSpec((tm // tk, 512, tk), lambda i: (i, 0, 0)), fspec, fspec],
        out_shape=[half] * 6 + [svt, full, full],
        scratch_shapes=[pltpu.VMEM((2, tm // SUBLANES, SUBLANES, d), F32), pltpu.SemaphoreType.DMA((2,))],
        compiler_params=_params("arbitrary"),
        name="proj",
    )(x2, x2, g, w, cosf, sinf)


def _ret_consts(tb):
    hd = np.arange(RET_HEADS, dtype=np.float64)
    lg = np.log(1.0 - np.exp2(-5.0 - hd))
    idx = np.arange(tb, dtype=np.float64)
    dist = np.abs(idx[:, None] - idx[None, :])
    chunk = (np.arange(tb) // CHUNK)
    allowed = chunk[None, :] <= chunk[:, None]
    dm = np.exp(lg[:, None, None] * dist) * allowed
    qdec = np.exp(lg[:, None] * (idx + 1.0))[:, :, None] * np.ones((1, 1, RET_DK))
    kdec = np.exp(lg[:, None] * (tb - 1.0 - idx))[:, :, None] * np.ones((1, 1, RET_DK))
    cdec = [float(v) for v in np.exp(lg * tb)]
    return (jnp.asarray(dm, F32), jnp.asarray(qdec, F32), jnp.asarray(kdec, F32), cdec)


def _ret_kernel(cdec, q_ref, k_ref, v_ref, g_ref, dm_ref, qd_ref, kd_ref, gn_ref, o_ref, st_ref):
    @pl.when(pl.program_id(1) == 0)
    def _():
        st_ref[...] = jnp.zeros_like(st_ref)

    tb = dm_ref.shape[1]

    def block(sub, c):
        rows = pl.ds(pl.multiple_of(sub * tb, tb), tb)
        for hd in range(RET_HEADS):
            sl = slice(hd * RET_DK, (hd + 1) * RET_DK)
            q = q_ref[rows, sl]
            k = k_ref[rows, sl]
            v = v_ref[rows, sl]
            st = st_ref[hd]
            scores = lax.dot_general(q, k, (((1,), (1,)), ((), ())), preferred_element_type=F32)
            scores = scores * dm_ref[hd]
            qd = (q.astype(F32) * qd_ref[hd]).astype(BF16)
            o = (jnp.dot(scores.astype(BF16), v, preferred_element_type=F32)
                 + jnp.dot(qd, st.astype(BF16), preferred_element_type=F32))
            kd = (k.astype(F32) * kd_ref[hd]).astype(BF16)
            st_ref[hd] = cdec[hd] * st + lax.dot_general(
                kd, v, (((0,), (0,)), ((), ())), preferred_element_type=F32)
            mu = jnp.mean(o, axis=-1, keepdims=True)
            oc = o - mu
            var = jnp.mean(oc * oc, axis=-1, keepdims=True)
            on = oc * lax.rsqrt(var + EPS) * gn_ref[:, sl]
            gate = g_ref[rows, sl].astype(F32)
            o_ref[rows, sl] = (gate * _sigmoid(gate) * on).astype(BF16)
        return c

    lax.fori_loop(0, q_ref.shape[0] // tb, block, 0)


def _retention(rq, rk, rv, rg, gn, batch, seq):
    n, w = rq.shape
    tb = RET_ROWS
    nc = seq // tb
    dm, qdec, kdec, cdec = _ret_consts(tb)
    steps = nc // RET_STEP_BLOCKS
    blk = pl.BlockSpec((RET_STEP_BLOCKS * tb, w), lambda b, c: (b * steps + c, 0))
    c3 = lambda b, c: (0, 0, 0)
    return pl.pallas_call(
        functools.partial(_ret_kernel, cdec),
        grid=(batch, steps),
        in_specs=[blk, blk, blk, blk,
                  pl.BlockSpec(dm.shape, c3), pl.BlockSpec(qdec.shape, c3), pl.BlockSpec(kdec.shape, c3),
                  pl.BlockSpec((1, w), lambda b, c: (0, 0))],
        out_specs=blk,
        out_shape=jax.ShapeDtypeStruct((n, w), BF16),
        scratch_shapes=[pltpu.VMEM((RET_HEADS, RET_DK, RET_DK), F32)],
        compiler_params=_params("parallel", "arbitrary"),
        name="retention",
    )(rq, rk, rv, rg, dm, qdec, kdec, gn)


def _sb_query_block(qi, rows, q_ref, k_ref, vt_ref, o_ref, acc_ref):
    tq = SB_BLOCK
    tk = tq
    grp = tk // SUBLANES
    heads = q_ref.shape[1] // SB_DH
    wide = heads * tq
    assert tq & (tq - 1) == 0
    per_group = LANES // SB_DH
    lane = lax.broadcasted_iota(I32, (tq, LANES), 1)
    q_groups = []
    for gi in range(heads // per_group):
        qg = q_ref[rows, gi * LANES:(gi + 1) * LANES]
        q_groups.append(jnp.concatenate(
            [jnp.where((lane >= hh * SB_DH) & (lane < (hh + 1) * SB_DH), qg, jnp.zeros_like(qg))
             for hh in range(per_group)], axis=0))
    shp3 = (grp, SUBLANES, wide)
    key_pos = lax.broadcasted_iota(I32, shp3, 1) * grp + lax.broadcasted_iota(I32, shp3, 0)
    causal = key_pos < (lax.broadcasted_iota(I32, shp3, 2) & (tq - 1))
    row8 = lax.broadcasted_iota(I32, (SUBLANES, wide), 0)

    def block(kb, carry, diagonal):
        z = jnp.concatenate(
            [lax.dot_general(k_ref[kb, :, gi * LANES:(gi + 1) * LANES], qg, (((1,), (1,)), ((), ())),
                             preferred_element_type=F32) for gi, qg in enumerate(q_groups)],
            axis=1)
        z3 = z.reshape(shp3)
        if diagonal:
            z3 = jnp.where(causal, z3, SB_MASKED)
        run = jnp.ones((SUBLANES, wide), F32)
        local = [None] * grp
        for g in reversed(range(grp)):
            half = 0.5 * jnp.tanh(z3[g])
            local[g] = (0.5 + half) * run
            run = run * (0.5 - half)
        log_tot = jnp.log2(run)
        off = jnp.where(row8 + 1 < SUBLANES, pltpu.roll(log_tot, SUBLANES - 1, 0), 0.0)
        for d in (1, 2, 4):
            off = off + jnp.where(row8 + d < SUBLANES, pltpu.roll(off, SUBLANES - d, 0), 0.0)
        scale = jnp.exp2(off + carry)
        a3 = jnp.stack(local, axis=0) * scale[None]
        a = a3.reshape(tk, wide).astype(BF16)
        for hh in range(heads):
            feat = slice(hh * SB_DH, (hh + 1) * SB_DH)
            contrib = jnp.dot(vt_ref[kb, feat, :], a[:, hh * tq:(hh + 1) * tq], preferred_element_type=F32)
            if diagonal:
                acc_ref[feat, :] = contrib
            else:
                acc_ref[feat, :] += contrib
        return carry + jnp.broadcast_to(off[0:1, :] + log_tot[0:1, :], carry.shape)

    carry = block(qi, jnp.zeros((SUBLANES, wide), F32), True)

    def cond(c):
        return (c[0] < qi) & (c[1] > 0)

    def body(c):
        new = block(qi - 1 - c[0], c[2], False)
        go = (jnp.max(new) > SB_SKIP_LOG2).astype(I32)
        return (c[0] + 1, go, new)

    lax.while_loop(cond, body, (jnp.int32(0), jnp.int32(1), carry))
    o_ref[rows, :] = acc_ref[...].T.astype(BF16)


def _sb_kernel(q_ref, k_ref, vt_ref, o_ref, acc_ref):
    per_step = q_ref.shape[0] // SB_BLOCK

    def query_block(sub, c):
        rows = pl.ds(pl.multiple_of(sub * SB_BLOCK, SB_BLOCK), SB_BLOCK)
        _sb_query_block(pl.program_id(1) * per_step + sub, rows, q_ref, k_ref, vt_ref, o_ref, acc_ref)
        return c

    lax.fori_loop(0, per_step, query_block, 0)


def _stick_breaking(sq, kperm, vtperm, batch, seq):
    n, w = sq.shape
    tq = SB_BLOCK
    nq = seq // tq
    rows = SB_STEP_BLOCKS * tq
    steps = nq // SB_STEP_BLOCKS
    return pl.pallas_call(
        _sb_kernel,
        grid=(batch, steps),
        in_specs=[
            pl.BlockSpec((rows, w), lambda b, i: (b * steps + i, 0)),
            pl.BlockSpec((None, nq, tq, w), lambda b, i: (b, 0, 0, 0)),
            pl.BlockSpec((None, nq, w, tq), lambda b, i: (b, 0, 0, 0)),
        ],
        out_specs=pl.BlockSpec((rows, w), lambda b, i: (b * steps + i, 0)),
        out_shape=jax.ShapeDtypeStruct((n, w), BF16),
        scratch_shapes=[pltpu.VMEM((w, tq), F32)],
        compiler_params=_params("parallel", "arbitrary"),
        name="stickbrk",
    )(sq, kperm, vtperm)


def _merge_kernel(ret_ref, sb_ref, gr_ref, gs_ref, x_ref, wbr_ref, wbs_ref, wo_ref, nf_ref,
                  wr_ref, br_ref, tri_ref,
                  x1_ref, h2_ref, ri_ref, rw_ref, cnt_ref, carry_ref):
    @pl.when(pl.program_id(0) == 0)
    def _():
        carry_ref[...] = jnp.zeros_like(carry_ref)

    tm = x_ref.shape[0]
    merged = (_sigmoid(gr_ref[...].astype(F32)) * jnp.dot(ret_ref[...], wbr_ref[...], preferred_element_type=F32)
              + _sigmoid(gs_ref[...].astype(F32)) * jnp.dot(sb_ref[...], wbs_ref[...], preferred_element_type=F32))
    x1 = x_ref[...] + jnp.dot(merged.astype(BF16), wo_ref[...], preferred_element_type=F32)
    x1_ref[...] = x1
    h2 = _rms(x1, nf_ref[...])
    _to_row_tiles(h2_ref, h2)
    logits = jnp.dot(h2.astype(BF16), wr_ref[...], preferred_element_type=F32) + br_ref[...]
    lt = logits.T
    neg = jnp.float32(-jnp.inf)

    grow = lax.broadcasted_iota(I32, (SUBLANES, tm), 0)
    gl = jnp.where(grow < N_GROUPS, lt[0:SUBLANES], neg)
    gmax = jnp.max(gl, axis=0, keepdims=True)
    g_idx = jnp.min(jnp.where(gl == gmax, grow, N_GROUPS), axis=0, keepdims=True)
    p_grp = 1.0 / jnp.sum(jnp.exp(gl - gmax), axis=0, keepdims=True)

    erow = lax.broadcasted_iota(I32, (N_EXPERTS, tm), 0)
    el = lt[EXPERT_COL0:EXPERT_COL0 + N_EXPERTS]
    el = jnp.where((erow >= g_idx * EXPERTS_PER_GROUP) & (erow < (g_idx + 1) * EXPERTS_PER_GROUP), el, neg)
    m1 = jnp.max(el, axis=0, keepdims=True)
    i1 = jnp.min(jnp.where(el == m1, erow, N_EXPERTS), axis=0, keepdims=True)
    el2 = jnp.where(erow == i1, neg, el)
    m2 = jnp.max(el2, axis=0, keepdims=True)
    i2 = jnp.min(jnp.where(el2 == m2, erow, N_EXPERTS), axis=0, keepdims=True)
    e2 = jnp.exp(m2 - m1)
    w1 = p_grp / (1.0 + e2)
    w2 = p_grp * e2 / (1.0 + e2)

    l1 = i1 - g_idx * EXPERTS_PER_GROUP
    l2 = i2 - g_idx * EXPERTS_PER_GROUP
    lo = jnp.minimum(l1, l2)
    hi = jnp.maximum(l1, l2)
    cls = g_idx * PAIRS_PER_GROUP + ((lo * (2 * EXPERTS_PER_GROUP - 1 - lo)) >> 1) + (hi - lo - 1)
    w_lo = jnp.where(l1 < l2, w1, w2)
    w_hi = jnp.where(l1 < l2, w2, w1)

    crow = lax.broadcasted_iota(I32, (CLASS_ROWS, tm), 0)
    sel = crow == cls
    onehot = sel.astype(BF16)
    cinc = jnp.dot(onehot, tri_ref[...], preferred_element_type=F32)
    carry = carry_ref[...]
    rank = jnp.sum(jnp.where(sel, carry[:, 0:1] + cinc - 1.0, 0.0), axis=0, keepdims=True)
    carry_new = carry + jnp.sum(onehot.astype(F32), axis=1, keepdims=True)
    carry_ref[...] = carry_new
    cnt_ref[...] = carry_new

    ri_ref[...] = jnp.zeros_like(ri_ref)
    ri_ref[0:1, :] = cls
    ri_ref[1:2, :] = rank.astype(I32)
    wrow = lax.broadcasted_iota(I32, (ROUTE_COLS, tm), 0)
    wt = jnp.where(wrow == 0, w_lo, jnp.where(wrow == 1, w_hi, 0.0))
    rw_ref[...] = wt.T


def _merge(ret, sb, g_ret, g_sb, x2, wbr, wbs, wo, nf, wr, br):
    n, d = x2.shape
    tm = MERGE_ROWS
    tri = jnp.asarray(np.triu(np.ones((tm, tm), np.float32)), BF16)
    row = lambda i: (i, 0)
    const = lambda i: (0, 0)
    full = lambda a: pl.BlockSpec(a.shape, const)
    return pl.pallas_call(
        _merge_kernel,
        grid=(n // tm,),
        in_specs=[pl.BlockSpec((tm, 512), row), pl.BlockSpec((tm, 512), row),
                  pl.BlockSpec((tm, d), row), pl.BlockSpec((tm, d), row), pl.BlockSpec((tm, d), row),
                  full(wbr), full(wbs), full(wo), full(nf), full(wr), full(br), full(tri)],
        out_specs=[pl.BlockSpec((tm, d), row), pl.BlockSpec((tm * ROW_TILES, LANES), row),
                   pl.BlockSpec((SUBLANES, tm), lambda i: (0, i)),
                   pl.BlockSpec((tm, ROUTE_COLS), row),
                   pl.BlockSpec((CLASS_ROWS, LANES), const)],
        out_shape=[jax.ShapeDtypeStruct((n, d), F32), jax.ShapeDtypeStruct((n * ROW_TILES, LANES), F32),
                   jax.ShapeDtypeStruct((SUBLANES, n), I32),
                   jax.ShapeDtypeStruct((n, ROUTE_COLS), F32),
                   jax.ShapeDtypeStruct((CLASS_ROWS, LANES), F32)],
        scratch_shapes=[pltpu.VMEM((CLASS_ROWS, LANES), F32)],
        compiler_params=_params("arbitrary"),
        name="merge",
    )(ret, sb, g_ret, g_sb, x2, wbr, wbs, wo, nf, wr, br, tri)


def _tile_rows(i):
    return pl.ds(pl.multiple_of(i * ROW_TILES, ROW_TILES), ROW_TILES)


def _dispatch_kernel(fill_ref, end_ref, dest_ref, x_ref, zero_ref, dst_hbm, sem):
    block_rows = zero_ref.shape[0]
    i = pl.program_id(0)
    per_step = dest_ref.shape[2]

    def send(i2, c):
        for k in range(DMA_QUEUES):
            r = i2 * DMA_QUEUES + k
            pltpu.make_async_copy(x_ref.at[_tile_rows(r)],
                                  dst_hbm.at[_tile_rows(dest_ref[0, 0, r])], sem).start(priority=k)
        return c
    lax.fori_loop(0, per_step // DMA_QUEUES, send, 0, unroll=4)
    pltpu.make_async_copy(x_ref, dst_hbm.at[pl.ds(0, per_step * ROW_TILES)], sem).wait()

    @pl.when(i == pl.num_programs(0) - 1)
    def _():
        def pad(s):
            return pltpu.make_async_copy(zero_ref.at[pl.ds(0, ROW_TILES)], dst_hbm.at[_tile_rows(s)], sem)

        def unused(b, c):
            blk = pltpu.make_async_copy(
                zero_ref, dst_hbm.at[pl.ds(pl.multiple_of(b * block_rows, block_rows), block_rows)], sem)
            blk.start()
            blk.wait()
            return c
        last = end_ref.shape[0] - 1
        lax.fori_loop(end_ref[last] // (block_rows // ROW_TILES), dst_hbm.shape[0] // block_rows, unused, 0)

        def start(s, c):
            pad(s).start()
            return c

        def wait(s, c):
            pad(s).wait()
            return c

        for cl in range(fill_ref.shape[0]):
            lax.fori_loop(fill_ref[cl], end_ref[cl], start, 0)
        for cl in range(fill_ref.shape[0]):
            lax.fori_loop(fill_ref[cl], end_ref[cl], wait, 0)


def _dispatch(dest, fill, end, h2, slots, tmb):
    n = dest.shape[0]
    per_step = min(DISPATCH_TOKENS, n)
    return pl.pallas_call(
        _dispatch_kernel,
        grid_spec=pltpu.PrefetchScalarGridSpec(
            num_scalar_prefetch=2,
            grid=(n // per_step,),
            in_specs=[pl.BlockSpec((1, 1, per_step), lambda i, f, e: (i, 0, 0), memory_space=pltpu.SMEM),
                      pl.BlockSpec((per_step * ROW_TILES, LANES), lambda i, f, e: (i, 0)),
                      pl.BlockSpec((tmb * ROW_TILES, LANES), lambda i, f, e: (0, 0))],
            out_specs=pl.BlockSpec(memory_space=pl.ANY),
            scratch_shapes=[pltpu.SemaphoreType.DMA(())],
        ),
        out_shape=jax.ShapeDtypeStruct((slots * ROW_TILES, LANES), F32),
        compiler_params=_params("arbitrary"),
        name="dispatch",
    )(fill, end, dest.reshape(n // per_step, 1, per_step), h2, jnp.zeros((tmb * ROW_TILES, LANES), F32))


def _expert_kernel(bl_ref, bh_ref, nu_ref, x_ref,
                   wgl_ref, wul_ref, wdl_ref, wgh_ref, wuh_ref, wdh_ref, o_ref):
    j = pl.program_id(0)
    used = nu_ref[0]
    rows = x_ref.shape[0] // ROW_TILES

    @pl.when(j < used)
    def _():
        xb = _from_row_tiles(x_ref, 0, rows).astype(BF16)

        def ffn(wg_ref, wu_ref, wd_ref):
            gate = jnp.dot(xb, wg_ref[...].astype(BF16), preferred_element_type=F32)
            up = jnp.dot(xb, wu_ref[...].astype(BF16), preferred_element_type=F32)
            hid = (gate * _sigmoid(gate) * up).astype(BF16)
            return jnp.dot(hid, wd_ref[...].astype(BF16), preferred_element_type=F32)

        _to_row_tiles(o_ref.at[0], ffn(wgl_ref, wul_ref, wdl_ref))
        _to_row_tiles(o_ref.at[1], ffn(wgh_ref, wuh_ref, wdh_ref))

    @pl.when(j >= used)
    def _():
        o_ref[...] = jnp.zeros_like(o_ref)


def _experts(block_lo, block_hi, n_used, xs, tmb, wg, wu, wd):
    nb = xs.shape[0] // (tmb * ROW_TILES)
    d = wg.shape[1]
    de = wg.shape[2]
    last = lambda j, nu: jnp.minimum(j, nu[0] - 1)
    lo_map = lambda j, bl, bh, nu: (bl[last(j, nu)], 0, 0)
    hi_map = lambda j, bl, bh, nu: (bh[last(j, nu)], 0, 0)
    return pl.pallas_call(
        _expert_kernel,
        grid_spec=pltpu.PrefetchScalarGridSpec(
            num_scalar_prefetch=3,
            grid=(nb,),
            in_specs=[
                pl.BlockSpec((tmb * ROW_TILES, LANES), lambda j, bl, bh, nu: (last(j, nu), 0)),
                pl.BlockSpec((None, d, de), lo_map),
                pl.BlockSpec((None, d, de), lo_map),
                pl.BlockSpec((None, de, d), lo_map),
                pl.BlockSpec((None, d, de), hi_map),
                pl.BlockSpec((None, d, de), hi_map),
                pl.BlockSpec((None, de, d), hi_map),
            ],
            out_specs=pl.BlockSpec((2, tmb * ROW_TILES, LANES), lambda j, bl, bh, nu: (0, j, 0)),
        ),
        out_shape=jax.ShapeDtypeStruct((2, nb * tmb * ROW_TILES, LANES), F32),
        compiler_params=_params("arbitrary"),
        name="experts",
    )(block_lo, block_hi, n_used, xs, wg, wu, wd, wg, wu, wd)


def _issue_pairs(idx_ref, yb_hbm, dst, sem, rows):
    def body(i, c):
        for k in range(DMA_QUEUES):
            r = i * DMA_QUEUES + k
            pltpu.make_async_copy(yb_hbm.at[:, _tile_rows(idx_ref[0, 0, r]), :], dst.at[:, _tile_rows(r), :],
                                  sem).start(priority=k)
        return c
    lax.fori_loop(0, rows // DMA_QUEUES, body, 0, unroll=4)


def _final_kernel(idx_ref, idxn_ref, x1_ref, rw_ref, p_ref, yb_hbm, npl_ref, wpg_ref, wp_ref, nfin_ref,
                  o_ref, gbuf, sem, x2_ref):
    i = pl.program_id(0)
    nt = pl.num_programs(0)
    tm = x1_ref.shape[0]
    slot = i % 2

    @pl.when(i == 0)
    def _():
        _issue_pairs(idx_ref, yb_hbm, gbuf.at[0], sem.at[0], tm)

    @pl.when(i + 1 < nt)
    def _():
        _issue_pairs(idxn_ref, yb_hbm, gbuf.at[1 - slot], sem.at[1 - slot], tm)

    pltpu.make_async_copy(gbuf.at[slot], gbuf.at[slot], sem.at[slot]).wait()
    w_lo = jnp.broadcast_to(rw_ref[:, 0:1], (tm, LANES))
    w_hi = jnp.broadcast_to(rw_ref[:, 1:2], (tm, LANES))
    for s in range(ROW_TILES):
        cols = slice(s * LANES, (s + 1) * LANES)
        x2_ref[:, cols] = (x1_ref[:, cols]
                           + w_lo * gbuf[slot, 0, pl.ds(s, tm, stride=ROW_TILES), :]
                           + w_hi * gbuf[slot, 1, pl.ds(s, tm, stride=ROW_TILES), :])
    x2 = x2_ref[...]
    hp = _rms(x2, npl_ref[...]).astype(BF16)
    gate = _sigmoid(jnp.dot(hp, wpg_ref[...], preferred_element_type=F32))
    emb = jnp.dot(p_ref[...].astype(BF16), wp_ref[...], preferred_element_type=F32)
    x3 = x2 + gate * emb
    o_ref[...] = _rms(x3, nfin_ref[...])


def _final(dest3, x1, rw, p2, yb, npl, wpg, wp, nfin):
    n, d = x1.shape
    tm = FINAL_ROWS
    nt = n // tm
    row = lambda i: (i, 0)
    const = lambda i: (0, 0)
    full = lambda a: pl.BlockSpec(a.shape, const)
    return pl.pallas_call(
        _final_kernel,
        grid=(nt,),
        in_specs=[
            pl.BlockSpec((1, 1, tm), lambda i: (i, 0, 0), memory_space=pltpu.SMEM),
            pl.BlockSpec((1, 1, tm), lambda i: (jnp.minimum(i + 1, nt - 1), 0, 0), memory_space=pltpu.SMEM),
            pl.BlockSpec((tm, d), row),
            pl.BlockSpec((tm, ROUTE_COLS), row),
            pl.BlockSpec((tm, p2.shape[1]), row),
            pl.BlockSpec(memory_space=pl.ANY),
            full(npl), full(wpg), full(wp), full(nfin),
        ],
        out_specs=pl.BlockSpec((tm, d), row),
        out_shape=jax.ShapeDtypeStruct((n, d), F32),
        scratch_shapes=[pltpu.VMEM((2, 2, tm * ROW_TILES, LANES), F32), pltpu.SemaphoreType.DMA((2,)),
                        pltpu.VMEM((tm, d), F32)],
        compiler_params=_params("arbitrary"),
        name="final",
    )(dest3, dest3, x1, rw, p2, yb, npl, wpg, wp, nfin)


def _layer(x2, p2, batch, seq, cosf, sinf, norm_mix, w_in, ret_gn, w_br_ret, w_br_sb, w_out, norm_ffn,
           w_grp, b_grp, w_exp, b_exp, w_gate, w_up, w_down, norm_ple, w_ple, w_ple_gate, out_gain):
    n, d = x2.shape
    rq, rk, rv, rg, sq, skp, svt, g_ret, g_sb = _proj(x2, norm_mix[None, :], w_in.astype(BF16), cosf, sinf, seq)
    ret = _retention(rq, rk, rv, rg, ret_gn[None, :], batch, seq)

    tk = SB_BLOCK
    nkb = seq // tk
    kperm = skp.reshape(batch, nkb, tk, skp.shape[1])
    vtperm = svt.reshape(batch, nkb, svt.shape[1], tk)
    sb = _stick_breaking(sq, kperm, vtperm, batch, seq)

    wr = jnp.zeros((d, ROUTE_COLS), F32)
    wr = wr.at[:, 0:N_GROUPS].set(w_grp).at[:, EXPERT_COL0:EXPERT_COL0 + N_EXPERTS].set(w_exp)
    br = jnp.zeros((1, ROUTE_COLS), F32)
    br = br.at[0, 0:N_GROUPS].set(b_grp).at[0, EXPERT_COL0:EXPERT_COL0 + N_EXPERTS].set(b_exp)
    x1, h2, ri, rw, cnt = _merge(ret, sb, g_ret, g_sb, x2, w_br_ret.astype(BF16), w_br_sb.astype(BF16),
                                 w_out.astype(BF16), norm_ffn[None, :], wr.astype(BF16), br)

    tmb = EXPERT_ROWS
    counts = cnt[:N_CLASSES, 0].astype(I32)
    padded = (counts + tmb - 1) // tmb * tmb
    pad_end = jnp.cumsum(padded)
    pad_start = pad_end - padded
    cids = jnp.arange(N_CLASSES, dtype=I32)
    dest = jnp.sum(jnp.where(ri[0, :, None] == cids, pad_start, 0), axis=-1) + ri[1]
    nb = n // tmb + N_CLASSES
    block_c = jnp.minimum(
        jnp.sum((pad_end[None, :] <= (jnp.arange(nb, dtype=I32) * tmb)[:, None]).astype(I32), axis=1),
        N_CLASSES - 1)
    pairs = [(lo, hi) for lo in range(EXPERTS_PER_GROUP) for hi in range(lo + 1, EXPERTS_PER_GROUP)]
    class_lo = np.array([g * EXPERTS_PER_GROUP + lo for g in range(N_GROUPS) for lo, _ in pairs], np.int32)
    class_hi = np.array([g * EXPERTS_PER_GROUP + hi for g in range(N_GROUPS) for _, hi in pairs], np.int32)
    block_lo = jnp.sum(jnp.where(block_c[:, None] == cids, class_lo, 0), axis=-1).astype(I32)
    block_hi = jnp.sum(jnp.where(block_c[:, None] == cids, class_hi, 0), axis=-1).astype(I32)
    n_used = (pad_end[-1:] // tmb).astype(I32)
    xs = _dispatch(dest, pad_start + counts, pad_end, h2, nb * tmb, tmb)
    yb = _experts(block_lo, block_hi, n_used, xs, tmb, w_gate, w_up, w_down)

    tf = FINAL_ROWS
    return _final(dest.reshape(n // tf, 1, tf), x1, rw, p2, yb, norm_ple[None, :],
                  w_ple_gate.astype(BF16), w_ple.astype(BF16), out_gain)


def kernel(x, p, norm_mix, w_in, ret_gn, w_br_ret, w_br_sb, w_out, norm_ffn, w_grp, b_grp, w_exp, b_exp,
           w_gate, w_up, w_down, norm_ple, w_ple, w_ple_gate, norm_final):
    batch, seq, d = x.shape
    depth = p.shape[0]
    assert depth == 1, "the final RMSNorm is fused into the last layer's kernel"
    pos = jnp.arange(seq, dtype=F32)
    inv_freq = ROPE_BASE ** (-jnp.arange(0, RET_DK, 2, dtype=F32) / RET_DK)
    ang = pos[:, None] * inv_freq[None, :]
    cos, sin = jnp.cos(ang), jnp.sin(ang)
    cosf = jnp.concatenate([cos, cos], axis=1)
    sinf = jnp.concatenate([-sin, sin], axis=1)
    x2 = x.reshape(batch * seq, d)
    i = 0
    out = _layer(x2, p[i].reshape(batch * seq, -1), batch, seq, cosf, sinf, norm_mix[i], w_in[i], ret_gn[i],
                 w_br_ret[i], w_br_sb[i], w_out[i], norm_ffn[i], w_grp[i], b_grp[i], w_exp[i], b_exp[i],
                 w_gate[i], w_up[i], w_down[i], norm_ple[i], w_ple[i], w_ple_gate[i], norm_final[None, :])
    return out.reshape(batch, seq, d)
```
